```python
import jax
import jax.numpy as jnp
from jax import lax
import numpy as np

D_MODEL = 1024
BATCH = 16
SEQ = 256
DEPTH = 4
DEC_BATCH = 4
DEC_SEQ = 2048
PAST_LEN = 256

GRID_W = 64
HEAD_DIM = 64
NA_WIDTH = D_MODEL // 4
NA_HEADS = NA_WIDTH // HEAD_DIM
NA_WIN_R = 8
NA_WIN_C = 16
MLA_WIDTH = D_MODEL // 2
MLA_V = 64
MLA_HEADS = MLA_WIDTH // MLA_V
MLA_NOPE = 64
MLA_ROPE = 32
MLA_Q_RANK = 3 * D_MODEL // 8
MLA_KV_RANK = D_MODEL // 4
SGU_WIDTH = D_MODEL // 4
SGU_GROUPS = 4
SGU_CHUNK = 128
D_FF = 11 * D_MODEL // 4
CONV_W = 3
ROPE_THETA = 10000.0
EPS = 1e-6
Q_BLOCK = 128
NEG_INF = -1e30
NA_SCALE = HEAD_DIM ** -0.5
MLA_SCALE = (MLA_NOPE + MLA_ROPE) ** -0.5
PROJ_SIZES = (NA_WIDTH, NA_WIDTH, NA_WIDTH, MLA_Q_RANK, MLA_KV_RANK, MLA_ROPE, 2 * SGU_WIDTH)
IN_WIDTH = sum(PROJ_SIZES)

kernel_name = 'hybrid_na_mla_sgu_diffusion_step'


def rmsnorm(x, g):
    xf = x.astype(jnp.float32)
    y = xf * lax.rsqrt(jnp.mean(xf * xf, axis=-1, keepdims=True) + EPS)
    return (y * g.astype(jnp.float32)).astype(x.dtype)


def split_heads(a, n):
    return a.reshape(a.shape[:-1] + (n, a.shape[-1] // n))


def merge_heads(a):
    return a.reshape(a.shape[:-2] + (-1,))


def axial_rope_tables(n_tokens):
    t = jnp.arange(n_tokens)
    n_freq = MLA_ROPE // 4
    inv_freq = ROPE_THETA ** (-jnp.arange(n_freq, dtype=jnp.float32) / n_freq)
    ang_r = (t // GRID_W).astype(jnp.float32)[:, None] * inv_freq
    ang_c = (t % GRID_W).astype(jnp.float32)[:, None] * inv_freq
    return (jnp.cos(ang_r), jnp.sin(ang_r), jnp.cos(ang_c), jnp.sin(ang_c))


def _rotate(x, cos, sin):
    x1, x2 = jnp.split(x, 2, axis=-1)
    cos = cos.astype(x.dtype)
    sin = sin.astype(x.dtype)
    return jnp.concatenate([x1 * cos - x2 * sin, x1 * sin + x2 * cos], axis=-1)


def axial_rope(x, tables):
    cos_r, sin_r, cos_c, sin_c = tables
    x_row, x_col = jnp.split(x, 2, axis=-1)
    return jnp.concatenate([_rotate(x_row, cos_r, sin_r), _rotate(x_col, cos_c, sin_c)], axis=-1)


def blocked_attention(q, k, v, scale):
    b, tq, h, dk = q.shape
    nb = tq // Q_BLOCK
    qb = q.reshape(b, nb, Q_BLOCK, h, dk).transpose(1, 0, 2, 3, 4)

    def one_block(q_blk):
        s = jnp.einsum('bqhd,bkhd->bhqk', q_blk, k).astype(jnp.float32) * scale
        p = jax.nn.softmax(s, axis=-1).astype(v.dtype)
        return jnp.einsum('bhqk,bkhd->bqhd', p, v)

    o = lax.map(one_block, qb)
    return o.transpose(1, 0, 2, 3, 4).reshape(b, tq, h, v.shape[-1])


def neighbourhood_attention(q, k, v, k_ctx, v_ctx, rpb, rows):
    b, t, h, dh = q.shape
    kr = min(NA_WIN_R, rows)
    qg = q.reshape(b, rows, GRID_W, h, dh)
    kg = k.reshape(b, rows, GRID_W, h, dh)
    vg = v.reshape(b, rows, GRID_W, h, dh)
    r = jnp.arange(rows)
    row_idx = jnp.clip(r - kr // 2, 0, rows - kr)[:, None] + jnp.arange(kr)[None, :]
    kb = kg[:, row_idx]
    vb = vg[:, row_idx]
    cols = jnp.arange(GRID_W)
    col_start = jnp.clip(cols - NA_WIN_C // 2, 0, GRID_W - NA_WIN_C)
    col_mask = (cols[None, :] >= col_start[:, None]) & (cols[None, :] < col_start[:, None] + NA_WIN_C)
    row_off = row_idx - r[:, None] + (NA_WIN_R - 1)
    col_off = jnp.clip(cols[None, :] - cols[:, None], -(NA_WIN_C - 1), NA_WIN_C - 1) + (NA_WIN_C - 1)
    bias = rpb[:, row_off[:, None, :, None], col_off[None, :, None, :]]
    s_loc = jnp.einsum('brqhd,brkchd->bhrqkc', qg, kb).astype(jnp.float32) * NA_SCALE
    s_loc = s_loc + bias.astype(jnp.float32)[None]
    s_loc = jnp.where(col_mask[None, None, None, :, None, :], s_loc, NEG_INF)
    n_loc = kr * GRID_W
    s_loc = s_loc.reshape(b, h, rows, GRID_W, n_loc)
    s_ctx = jnp.einsum('brqhd,bkhd->bhrqk', qg, k_ctx).astype(jnp.float32) * NA_SCALE
    p = jax.nn.softmax(jnp.concatenate([s_loc, s_ctx], axis=-1), axis=-1).astype(v.dtype)
    p_loc = p[..., :n_loc].reshape(b, h, rows, GRID_W, kr, GRID_W)
    p_ctx = p[..., n_loc:]
    o = jnp.einsum('bhrqkc,brkchd->brqhd', p_loc, vb) + jnp.einsum('bhrqk,bkhd->brqhd', p_ctx, v_ctx)
    return o.reshape(b, t, h * dh)


def spatial_gating(uv, g_sgu, w_s, b_s):
    u, v = jnp.split(jax.nn.gelu(uv), 2, axis=-1)
    v = rmsnorm(v, g_sgu)
    b, t, _ = v.shape
    vc = v.reshape(b, t // SGU_CHUNK, SGU_CHUNK, SGU_GROUPS, SGU_WIDTH // SGU_GROUPS)
    mixed = jnp.einsum('gpk,bnkgd->bnpgd', w_s, vc) + b_s.T[:, :, None]
    return u * mixed.reshape(b, t, SGU_WIDTH)


def conv_ffn(h, w_in, conv_w, conv_b, w_out):
    a = h @ w_in
    t = a.shape[1]
    ap = jnp.pad(a, ((0, 0), (CONV_W // 2, CONV_W // 2), (0, 0)))
    a = sum(ap[:, i:i + t] * conv_w[i] for i in range(CONV_W)) + conv_b
    gate, val = jnp.split(a, 2, axis=-1)
    return (jax.nn.silu(gate) * val) @ w_out


def modulation(cond, w_mod, b_mod):
    m = jax.nn.silu(cond) @ w_mod + b_mod
    return jnp.split(m[:, None, :], 6, axis=-1)


def _split_proj(z):
    offs = np.cumsum(PROJ_SIZES)[:-1].tolist()
    return jnp.split(z, offs, axis=-1)


def _mla_queries(cq, g_cq, w_uq):
    q = split_heads(rmsnorm(cq, g_cq) @ w_uq, MLA_HEADS)
    return q[..., :MLA_NOPE], q[..., MLA_NOPE:]


def _mla_attend(q_nope, q_rope, ckv, k_rope, w_ukv):
    kv = split_heads(ckv @ w_ukv, MLA_HEADS)
    k_nope, v = kv[..., :MLA_NOPE], kv[..., MLA_NOPE:]
    b, tk, h, _ = k_nope.shape
    k = jnp.concatenate([k_nope, jnp.broadcast_to(k_rope[:, :, None, :], (b, tk, h, MLA_ROPE))], axis=-1)
    q = jnp.concatenate([q_nope, q_rope], axis=-1)
    return merge_heads(blocked_attention(q, k, v, MLA_SCALE))


def context_mixer(h, lp):
    qa, ka, va, cq, ckv, krope, uv = _split_proj(h @ lp['w_in'])
    qa, ka, va = split_heads(qa, NA_HEADS), split_heads(ka, NA_HEADS), split_heads(va, NA_HEADS)
    ckv = rmsnorm(ckv, lp['g_ckv'])
    o_a = merge_heads(blocked_attention(qa, ka, va, NA_SCALE))
    q_nope, q_rope = _mla_queries(cq, lp['g_cq'], lp['w_uq'])
    o_b = _mla_attend(q_nope, q_rope, ckv, krope, lp['w_ukv'])
    o_c = spatial_gating(uv, lp['g_sgu'], lp['w_sgu'], lp['b_sgu'])
    return jnp.concatenate([o_a, o_b, o_c], axis=-1), (ka, va, ckv, krope)


def latent_mixer(h, lp, ctx, rows, rope_q, rope_k):
    ka_ctx, va_ctx, ckv_ctx, krope_ctx = ctx
    qa, ka, va, cq, ckv, krope, uv = _split_proj(h @ lp['w_in'])
    qa, ka, va = split_heads(qa, NA_HEADS), split_heads(ka, NA_HEADS), split_heads(va, NA_HEADS)
    o_a = neighbourhood_attention(qa, ka, va, ka_ctx, va_ctx, lp['na_rpb'], rows)
    ckv = rmsnorm(ckv, lp['g_ckv'])
    q_nope, q_rope = _mla_queries(cq, lp['g_cq'], lp['w_uq'])
    q_rope = axial_rope(q_rope, rope_q)
    krope = axial_rope(krope, rope_k)
    o_b = _mla_attend(q_nope, q_rope, jnp.concatenate([ckv_ctx, ckv], axis=1),
                      jnp.concatenate([krope_ctx, krope], axis=1), lp['w_ukv'])
    o_c = spatial_gating(uv, lp['g_sgu'], lp['w_sgu'], lp['b_sgu'])
    return jnp.concatenate([o_a, o_b, o_c], axis=-1), None


def residual_block(x, mods, lp, mixer):
    sh1, sc1, g1, sh2, sc2, g2 = mods
    h = rmsnorm(x, lp['g_mix']) * (1 + sc1) + sh1
    mix, ctx_tensors = mixer(h)
    x = x + g1 * (mix @ lp['w_out'])
    h = rmsnorm(x, lp['g_ffn']) * (1 + sc2) + sh2
    x = x + g2 * conv_ffn(h, lp['w_ffn_in'], lp['ffn_conv_w'], lp['ffn_conv_b'], lp['w_ffn_out'])
    return x, ctx_tensors


def setup_inputs(seed: int = 0) -> dict:
    key = jax.random.key(seed)
    ks = jax.random.split(key, 27)
    L = DEPTH

    def nrm(k, shape, s):
        return jax.random.normal(k, shape, jnp.float32) * s

    def gain(k, shape):
        return 1.0 + 0.05 * jax.random.normal(k, shape, jnp.float32)

    return {
        'x_prompt': nrm(ks[0], (BATCH, SEQ, D_MODEL), 1.0),
        'x_sample': nrm(ks[1], (DEC_BATCH, DEC_SEQ, D_MODEL), 1.0),
        'cache_na_k': nrm(ks[2], (DEC_BATCH, L, PAST_LEN, NA_HEADS, HEAD_DIM), 1.0),
        'cache_na_v': nrm(ks[3], (DEC_BATCH, L, PAST_LEN, NA_HEADS, HEAD_DIM), 1.0),
        'cache_mla_ckv': nrm(ks[4], (DEC_BATCH, L, PAST_LEN, MLA_KV_RANK), 1.0),
        'cache_mla_krope': nrm(ks[5], (DEC_BATCH, L, PAST_LEN, MLA_ROPE), 1.0),
        'c': nrm(ks[6], (DEC_BATCH, D_MODEL), 1.0),
        'c_ctx': nrm(ks[7], (D_MODEL,), 1.0),
        'w_mod': nrm(ks[8], (L, D_MODEL, 6 * D_MODEL), 0.5 * D_MODEL ** -0.5),
        'b_mod': nrm(ks[9], (L, 6 * D_MODEL), 0.02),
        'g_mix': gain(ks[10], (L, D_MODEL)),
        'w_in': nrm(ks[11], (L, D_MODEL, IN_WIDTH), D_MODEL ** -0.5),
        'na_rpb': nrm(ks[12], (L, NA_HEADS, 2 * NA_WIN_R - 1, 2 * NA_WIN_C - 1), 0.5),
        'g_cq': gain(ks[13], (L, MLA_Q_RANK)),
        'w_uq': nrm(ks[14], (L, MLA_Q_RANK, MLA_HEADS * (MLA_NOPE + MLA_ROPE)), MLA_Q_RANK ** -0.5),
        'g_ckv': gain(ks[15], (L, MLA_KV_RANK)),
        'w_ukv': nrm(ks[16], (L, MLA_KV_RANK, MLA_HEADS * (MLA_NOPE + MLA_V)), MLA_KV_RANK ** -0.5),
        'g_sgu': gain(ks[17], (L, SGU_WIDTH)),
        'w_sgu': nrm(ks[18], (L, SGU_GROUPS, SGU_CHUNK, SGU_CHUNK), SGU_CHUNK ** -0.5),
        'b_sgu': gain(ks[19], (L, SGU_GROUPS, SGU_CHUNK)),
        'w_out': nrm(ks[20], (L, D_MODEL, D_MODEL), D_MODEL ** -0.5),
        'g_ffn': gain(ks[21], (L, D_MODEL)),
        'w_ffn_in': nrm(ks[22], (L, D_MODEL, 2 * D_FF), D_MODEL ** -0.5),
        'ffn_conv_w': nrm(ks[23], (L, CONV_W, 2 * D_FF), CONV_W ** -0.5),
        'ffn_conv_b': nrm(ks[24], (L, 2 * D_FF), 0.02),
        'w_ffn_out': nrm(ks[25], (L, D_FF, D_MODEL), D_FF ** -0.5),
        'g_final': gain(ks[26], (D_MODEL,)),
    }


def reference(x_prompt, x_sample, cache_na_k, cache_na_v, cache_mla_ckv, cache_mla_krope, c, c_ctx,
              w_mod, b_mod, g_mix, w_in, na_rpb, g_cq, w_uq, g_ckv, w_ukv, g_sgu, w_sgu, b_sgu,
              w_out, g_ffn, w_ffn_in, ffn_conv_w, ffn_conv_b, w_ffn_out, g_final):
    rows = x_sample.shape[1] // GRID_W
    rope_k = axial_rope_tables(x_sample.shape[1])
    rope_q = tuple(tab[:, None, :] for tab in rope_k)
    xp, xs = x_prompt, x_sample
    new_k, new_v, new_ckv, new_kr = [], [], [], []
    for l in range(DEPTH):
        lp = {
            'g_mix': g_mix[l], 'w_in': w_in[l], 'na_rpb': na_rpb[l], 'g_cq': g_cq[l], 'w_uq': w_uq[l],
            'g_ckv': g_ckv[l], 'w_ukv': w_ukv[l], 'g_sgu': g_sgu[l], 'w_sgu': w_sgu[l], 'b_sgu': b_sgu[l],
            'w_out': w_out[l], 'g_ffn': g_ffn[l], 'w_ffn_in': w_ffn_in[l], 'ffn_conv_w': ffn_conv_w[l],
            'ffn_conv_b': ffn_conv_b[l], 'w_ffn_out': w_ffn_out[l],
        }
        mods_ctx = modulation(c_ctx[None, :], w_mod[l], b_mod[l])
        xp, (ka, va, ckv, kr) = residual_block(xp, mods_ctx, lp, lambda h: context_mixer(h, lp))
        new_k.append(ka)
        new_v.append(va)
        new_ckv.append(ckv)
        new_kr.append(kr)
        mods_lat = modulation(c, w_mod[l], b_mod[l])
        ctx_l = (cache_na_k[:, l], cache_na_v[:, l], cache_mla_ckv[:, l], cache_mla_krope[:, l])
        xs, _ = residual_block(xs, mods_lat, lp,
                               lambda h: latent_mixer(h, lp, ctx_l, rows, rope_q, rope_k))
    y_prompt = rmsnorm(xp, g_final)
    y_sample = rmsnorm(xs, g_final)
    new_na_k = jnp.stack(new_k, axis=1)
    new_na_v = jnp.stack(new_v, axis=1)
    new_mla_ckv = jnp.stack(new_ckv, axis=1)
    new_mla_krope = jnp.stack(new_kr, axis=1)
    return (y_prompt, y_sample, new_na_k, new_na_v, new_mla_ckv, new_mla_krope)
```

```python
import functools

import jax
import jax.numpy as jnp
from jax import lax
from jax.experimental import pallas as pl
from jax.experimental.pallas import tpu as pltpu

F32 = jnp.float32
BF16 = jnp.bfloat16

D_MODEL = 1024
DEPTH = 4
SEQ = 256
DEC_SEQ = 2048
PAST_LEN = 256
GRID_W = 64
HEAD_DIM = 64
NA_WIDTH = 256
NA_HEADS = 4
NA_WIN_R = 8
NA_WIN_C = 16
MLA_HEADS = 8
MLA_NOPE = 64
MLA_ROPE = 32
MLA_V = 64
MLA_Q_RANK = 384
MLA_KV_RANK = 256
SGU_WIDTH = 256
SGU_GROUPS = 4
SGU_CHUNK = 128
D_FF = 2816
ROPE_THETA = 10000.0
EPS = 1e-6
NEG_INF = -1e30
NA_SCALE = HEAD_DIM ** -0.5
MLA_SCALE = (MLA_NOPE + MLA_ROPE) ** -0.5

LANES = 128
BF16_ROWS = 16
HALF = LANES // 2

OFF_QA, OFF_KA, OFF_VA = 0, 256, 512
OFF_CQ = 768
OFF_CKV = OFF_CQ + MLA_Q_RANK
OFF_KR = OFF_CKV + MLA_KV_RANK
OFF_UV = OFF_KR + LANES
IN_COLS = OFF_UV + 2 * SGU_WIDTH
KV_COLS = MLA_HEADS * LANES + MLA_HEADS * MLA_V

FF_CHUNK = 256
N_FF_CHUNKS = D_FF // FF_CHUNK

TM = 512
NA_QROWS = 4
NA_QBLK = NA_QROWS * GRID_W
NA_KBLKS = 3
VMEM_LIMIT = 56 * 1024 * 1024


def _cparams(n_axes):
    return pltpu.CompilerParams(dimension_semantics=("arbitrary",) * n_axes,
                                vmem_limit_bytes=VMEM_LIMIT)


def _rms(x, g):
    ms = jnp.mean(x * x, axis=-1, keepdims=True)
    return x * lax.rsqrt(ms + EPS) * g


def _dot(a, b):
    return jnp.dot(a, b, preferred_element_type=F32)


def _dot_nt(a, b):
    return lax.dot_general(a, b, (((1,), (1,)), ((), ())), preferred_element_type=F32)


def _half_mask(parity):
    lane = lax.broadcasted_iota(jnp.int32, (1, LANES), 1)
    return (lane // HALF) == parity


def _mod_kernel(c_ref, w_ref, b_ref, o_ref):
    c = c_ref[...]
    s = c * jax.nn.sigmoid(c)
    o_ref[...] = jnp.dot(s, w_ref[...], preferred_element_type=F32,
                         precision=lax.Precision.HIGHEST) + b_ref[...]


def _modulation(cond, w_mod, b_mod):
    n = 6
    out = pl.pallas_call(
        _mod_kernel,
        grid=(DEPTH, n),
        in_specs=[
            pl.BlockSpec((8, D_MODEL), lambda l, j: (0, 0)),
            pl.BlockSpec((None, D_MODEL, D_MODEL), lambda l, j: (l, 0, j)),
            pl.BlockSpec((None, 1, D_MODEL), lambda l, j: (l, 0, j)),
        ],
        out_specs=pl.BlockSpec((None, 8, D_MODEL), lambda l, j: (l, 0, j)),
        out_shape=jax.ShapeDtypeStruct((DEPTH, 8, n * D_MODEL), F32),
        compiler_params=_cparams(2),
        name="modulation",
    )(cond, w_mod, b_mod.reshape(DEPTH, 1, n * D_MODEL))
    return out.reshape(DEPTH, 8, n, D_MODEL)


def _rope(x, cos, sin, lane_lo):
    up = pltpu.roll(x, LANES - 8, axis=1)
    dn = pltpu.roll(x, 8, axis=1)
    return x * cos + jnp.where(lane_lo, up, dn) * sin


def _proj_kernel(rope, emit_f32, x_ref, mod_ref, gmix_ref, win_ref, gcq_ref, wuq_ref, gckv_ref,
                 wukv_ref, gsgu_ref, wsgu_ref, bsg_ref, *refs):
    if rope:
        cos_ref, sin_ref = refs[:2]
        refs = refs[2:]
    qa_ref, ka_ref, va_ref, qm_ref, kp_ref, vp_ref, oc_ref = refs[:7]
    if emit_f32:
        kaf_ref, vaf_ref, ckvf_ref, krf_ref = refs[7:]

    x = x_ref[...]
    h = _rms(x, gmix_ref[...]) * (1.0 + mod_ref[1:2, :]) + mod_ref[0:1, :]
    hb = h.astype(BF16)
    tm = x.shape[0]
    lane = lax.broadcasted_iota(jnp.int32, (1, LANES), 1)
    lane_lo = (lane % 16) < 8
    if rope:
        cos = cos_ref[...]
        sin = sin_ref[...]

    qa = _dot(hb, win_ref[:, OFF_QA:OFF_QA + NA_WIDTH]) * NA_SCALE
    ka = _dot(hb, win_ref[:, OFF_KA:OFF_KA + NA_WIDTH])
    va = _dot(hb, win_ref[:, OFF_VA:OFF_VA + NA_WIDTH])
    for p in range(NA_HEADS // 2):
        sl = slice(p * LANES, (p + 1) * LANES)
        qa_ref[p] = qa[:, sl].astype(BF16)
        ka_ref[p] = ka[:, sl].astype(BF16)
        va_ref[p] = va[:, sl].astype(BF16)
    if emit_f32:
        kaf_ref[...] = ka
        vaf_ref[...] = va

    cq = _dot(hb, win_ref[:, OFF_CQ:OFF_CQ + MLA_Q_RANK])
    cqn = _rms(cq, gcq_ref[...]).astype(BF16)
    qm = _dot(cqn, wuq_ref[...])
    for hd in range(MLA_HEADS):
        qh = qm[:, hd * LANES:(hd + 1) * LANES]
        if rope:
            qh = _rope(qh, cos, sin, lane_lo)
        qm_ref[hd] = (qh * MLA_SCALE).astype(BF16)

    ckv = _dot(hb, win_ref[:, OFF_CKV:OFF_CKV + MLA_KV_RANK])
    ckvn = _rms(ckv, gckv_ref[...])
    kr = _dot(hb, win_ref[:, OFF_KR:OFF_KR + LANES])
    if emit_f32:
        ckvf_ref[...] = ckvn
        krf_ref[...] = kr
    if rope:
        kr = _rope(kr, cos, sin, lane_lo)
    kv = _dot(ckvn.astype(BF16), wukv_ref[...])
    for hd in range(MLA_HEADS):
        kp_ref[hd] = (kv[:, hd * LANES:(hd + 1) * LANES] + kr).astype(BF16)
    voff = MLA_HEADS * LANES
    for p in range(MLA_HEADS // 2):
        vp_ref[p] = kv[:, voff + p * LANES:voff + (p + 1) * LANES].astype(BF16)

    uv = jax.nn.gelu(_dot(hb, win_ref[:, OFF_UV:OFF_UV + 2 * SGU_WIDTH]))
    u = uv[:, :SGU_WIDTH]
    vn = _rms(uv[:, SGU_WIDTH:], gsgu_ref[...])
    even = _half_mask(0)
    for ch in range(tm // SGU_CHUNK):
        rows = slice(ch * SGU_CHUNK, (ch + 1) * SGU_CHUNK)
        for p in range(SGU_GROUPS // 2):
            sl = slice(p * LANES, (p + 1) * LANES)
            vc = vn[rows, sl]
            mixed = (_dot(wsgu_ref[2 * p], jnp.where(even, vc, 0.0).astype(BF16))
                     + _dot(wsgu_ref[2 * p + 1], jnp.where(even, 0.0, vc).astype(BF16))
                     + bsg_ref[:, sl])
            oc_ref[rows, sl] = (u[rows, sl] * mixed).astype(BF16)


def _project(x, mods, mod_row, lw, rope_tabs, emit_f32, tiles_per_seq):
    t = x.shape[0]
    nt = t // TM
    rope = rope_tabs is not None
    const = lambda *shape: pl.BlockSpec(shape, lambda i: (0,) * len(shape))
    in_specs = [
        pl.BlockSpec((TM, D_MODEL), lambda i: (i, 0)),
        pl.BlockSpec((None, 6, D_MODEL), lambda i: (mod_row(i), 0, 0)),
        const(1, D_MODEL),
        const(D_MODEL, IN_COLS),
        const(1, MLA_Q_RANK),
        const(MLA_Q_RANK, MLA_HEADS * LANES),
        const(1, MLA_KV_RANK),
        const(MLA_KV_RANK, KV_COLS),
        const(1, SGU_WIDTH),
        const(SGU_GROUPS, SGU_CHUNK, SGU_CHUNK),
        const(SGU_CHUNK, SGU_WIDTH),
    ]
    args = [x, mods, lw["g_mix"], lw["w_in"], lw["g_cq"], lw["w_uq"], lw["g_ckv"], lw["w_ukv"],
            lw["g_sgu"], lw["w_sgu"], lw["b_sgu"]]
    if rope:
        tab_spec = pl.BlockSpec((TM, LANES), lambda i: (i % tiles_per_seq, 0))
        in_specs += [tab_spec, tab_spec]
        args += list(rope_tabs)

    def heads(n):
        return (pl.BlockSpec((n, TM, LANES), lambda i: (0, i, 0)),
                jax.ShapeDtypeStruct((n, t, LANES), BF16))

    def flat(w, dt):
        return (pl.BlockSpec((TM, w), lambda i: (i, 0)), jax.ShapeDtypeStruct((t, w), dt))

    outs = [heads(2), heads(2), heads(2), heads(MLA_HEADS), heads(MLA_HEADS), heads(MLA_HEADS // 2),
            flat(SGU_WIDTH, BF16)]
    if emit_f32:
        outs += [flat(NA_WIDTH, F32), flat(NA_WIDTH, F32), flat(MLA_KV_RANK, F32), flat(LANES, F32)]
    return pl.pallas_call(
        functools.partial(_proj_kernel, rope, emit_f32),
        grid=(nt,),
        in_specs=in_specs,
        out_specs=[o[0] for o in outs],
        out_shape=[o[1] for o in outs],
        compiler_params=_cparams(1),
        name="project_lat" if rope else "project_ctx",
    )(*args)


def _cache_kv_kernel(ckv_ref, kr_ref, wukv_ref, kp_ref, vp_ref):
    kv = _dot(ckv_ref[...].astype(BF16), wukv_ref[...])
    kr = kr_ref[...]
    for hd in range(MLA_HEADS):
        kp_ref[hd] = (kv[:, hd * LANES:(hd + 1) * LANES] + kr).astype(BF16)
    voff = MLA_HEADS * LANES
    for p in range(MLA_HEADS // 2):
        vp_ref[p] = kv[:, voff + p * LANES:voff + (p + 1) * LANES].astype(BF16)


def _cache_kv(cache_ckv, cache_kr_pad, w_ukv):
    b = cache_ckv.shape[0]
    return pl.pallas_call(
        _cache_kv_kernel,
        grid=(DEPTH, b),
        in_specs=[
            pl.BlockSpec((None, None, PAST_LEN, MLA_KV_RANK), lambda l, i: (i, l, 0, 0)),
            pl.BlockSpec((None, None, PAST_LEN, LANES), lambda l, i: (i, l, 0, 0)),
            pl.BlockSpec((None, MLA_KV_RANK, KV_COLS), lambda l, i: (l, 0, 0)),
        ],
        out_specs=[
            pl.BlockSpec((None, MLA_HEADS, PAST_LEN, LANES), lambda l, i: (l, 0, i, 0)),
            pl.BlockSpec((None, MLA_HEADS // 2, PAST_LEN, LANES), lambda l, i: (l, 0, i, 0)),
        ],
        out_shape=[
            jax.ShapeDtypeStruct((DEPTH, MLA_HEADS, b * PAST_LEN, LANES), BF16),
            jax.ShapeDtypeStruct((DEPTH, MLA_HEADS // 2, b * PAST_LEN, LANES), BF16),
        ],
        compiler_params=_cparams(2),
        name="cache_kv",
    )(cache_ckv, cache_kr_pad, w_ukv)


def _attn_kernel(nseg, q_packed, q_ref, *refs):
    k_refs = refs[0:2 * nseg:2]
    v_refs = refs[1:2 * nseg:2]
    o_ref = refs[2 * nseg]
    own = _half_mask(pl.program_id(1) % 2)
    q = q_ref[...]
    if q_packed:
        q = jnp.where(own, q, jnp.zeros_like(q))
    scores = [_dot_nt(q, k[...]) for k in k_refs]
    m = functools.reduce(jnp.maximum, [jnp.max(s, axis=-1, keepdims=True) for s in scores])
    probs = [jnp.exp(s - m) for s in scores]
    denom = functools.reduce(jnp.add, [jnp.sum(p, axis=-1, keepdims=True) for p in probs])
    acc = None
    for p, v_ref in zip(probs, v_refs):
        v = v_ref[...]
        part = _dot(p.astype(BF16), jnp.where(own, v, jnp.zeros_like(v)))
        acc = part if acc is None else acc + part
    o_ref[...] = (acc / denom).astype(BF16)


def _attention(q, segs, n_batch, n_heads, tq_total, tq, q_packed, name):
    nq = tq_total // tq
    hq = (lambda h: h // 2) if q_packed else (lambda h: h)
    in_specs = [pl.BlockSpec((None, tq, LANES), lambda b, h, i: (hq(h), b * nq + i, 0))]
    args = [q]
    for k, v, tk in segs:
        in_specs.append(pl.BlockSpec((None, tk, LANES), lambda b, h, i: (hq(h), b, 0)))
        in_specs.append(pl.BlockSpec((None, tk, LANES), lambda b, h, i: (h // 2, b, 0)))
        args += [k, v]
    return pl.pallas_call(
        functools.partial(_attn_kernel, len(segs), q_packed),
        grid=(n_batch, n_heads, nq),
        in_specs=in_specs,
        out_specs=pl.BlockSpec((None, tq, LANES), lambda b, h, i: (h, b * nq + i, 0)),
        out_shape=jax.ShapeDtypeStruct((n_heads, n_batch * tq_total, LANES), BF16),
        compiler_params=_cparams(3),
        name=name,
    )(*args)


def _na_bias_kernel(rpb_ref, o_ref, tab_ref):
    l = pl.program_id(0)
    hd = pl.program_id(1)
    n_dr = 2 * NA_WIN_R - 1
    n_dc = 2 * NA_WIN_C - 1
    base = (l * NA_HEADS + hd) * n_dr * n_dc
    qc = lax.broadcasted_iota(jnp.int32, (GRID_W, LANES), 0)
    kc = lax.broadcasted_iota(jnp.int32, (GRID_W, LANES), 1) % GRID_W
    diff = kc - qc
    cs = jnp.clip(qc - NA_WIN_C // 2, 0, GRID_W - NA_WIN_C)
    in_win = (kc >= cs) & (kc < cs + NA_WIN_C)
    neg = jnp.full((GRID_W, LANES), NEG_INF, F32)
    for dr in range(n_dr):
        acc = neg
        for dc in range(n_dc):
            acc = jnp.where(diff == dc - (NA_WIN_C - 1), rpb_ref[base + dr * n_dc + dc], acc)
        tab_ref[dr] = jnp.where(in_win, acc, NEG_INF)
    left = lax.broadcasted_iota(jnp.int32, (GRID_W, LANES), 1) < GRID_W
    n_krows = NA_KBLKS * NA_QBLK // GRID_W
    cases = ((0, lambda qr: 0), (-NA_QROWS, lambda qr: qr), (-2 * NA_QROWS, lambda qr: NA_QROWS))
    for c, (shift, first) in enumerate(cases):
        for qr in range(NA_QROWS):
            for kp in range(n_krows // 2):
                tiles = []
                for kr in (2 * kp, 2 * kp + 1):
                    ok = first(qr) <= kr < first(qr) + NA_WIN_R
                    tiles.append(tab_ref[kr - qr + shift + NA_WIN_R - 1] if ok else neg)
                o_ref[c, qr * GRID_W:(qr + 1) * GRID_W, kp * LANES:(kp + 1) * LANES] = (
                    jnp.where(left, tiles[0], tiles[1]))


def _na_bias(na_rpb):
    n_k = NA_KBLKS * NA_QBLK
    return pl.pallas_call(
        _na_bias_kernel,
        grid=(DEPTH, NA_HEADS),
        in_specs=[pl.BlockSpec(memory_space=pltpu.SMEM)],
        out_specs=pl.BlockSpec((None, 3, None, NA_QBLK, n_k), lambda l, h: (l, 0, h, 0, 0)),
        out_shape=jax.ShapeDtypeStruct((DEPTH, 3, NA_HEADS, NA_QBLK, n_k), F32),
        scratch_shapes=[pltpu.VMEM((2 * NA_WIN_R - 1, GRID_W, LANES), F32)],
        compiler_params=_cparams(2),
        name="na_bias",
    )(na_rpb.reshape(-1))


def _na_kernel(q_ref, k0_ref, k1_ref, k2_ref, v0_ref, v1_ref, v2_ref, kc_ref, vc_ref, bias_ref, o_ref):
    k_refs = (k0_ref, k1_ref, k2_ref)
    v_refs = (v0_ref, v1_ref, v2_ref)
    for p in range(NA_HEADS // 2):
        sl = slice(p * LANES, (p + 1) * LANES)
        kc = kc_ref[:, sl].astype(BF16)
        vc = vc_ref[:, sl].astype(BF16)
        out = None
        for half in range(2):
            hd = 2 * p + half
            own = _half_mask(half)
            q = q_ref[p]
            q = jnp.where(own, q, jnp.zeros_like(q))
            scores = [_dot_nt(q, k_refs[i][p]) + bias_ref[hd, :, i * NA_QBLK:(i + 1) * NA_QBLK]
                      for i in range(NA_KBLKS)]
            scores.append(_dot_nt(q, kc))
            m = functools.reduce(jnp.maximum, [jnp.max(s, axis=-1, keepdims=True) for s in scores])
            probs = [jnp.exp(s - m) for s in scores]
            denom = functools.reduce(jnp.add, [jnp.sum(pr, axis=-1, keepdims=True) for pr in probs])
            vals = [v_refs[i][p] for i in range(NA_KBLKS)] + [vc]
            acc = None
            for pr, v in zip(probs, vals):
                part = _dot(pr.astype(BF16), jnp.where(own, v, jnp.zeros_like(v)))
                acc = part if acc is None else acc + part
            acc = acc / denom
            out = acc if out is None else out + acc
        o_ref[:, sl] = out.astype(BF16)


def _na_attention(qa, ka, va, cache_k, cache_v, bias, l, n_batch):
    nblk = DEC_SEQ // NA_QBLK
    max_start = nblk - NA_KBLKS

    def kspec(i):
        return pl.BlockSpec((NA_HEADS // 2, NA_QBLK, LANES),
                            lambda j, b: (0, b * nblk + jnp.clip(j - 1, 0, max_start) + i, 0))

    case = lambda j: jnp.where(j == 0, 0, jnp.where(j == nblk - 1, 2, 1))
    cache_spec = pl.BlockSpec((None, None, PAST_LEN, NA_WIDTH), lambda j, b: (b, l, 0, 0))
    return pl.pallas_call(
        _na_kernel,
        grid=(nblk, n_batch),
        in_specs=[pl.BlockSpec((NA_HEADS // 2, NA_QBLK, LANES), lambda j, b: (0, b * nblk + j, 0)),
                  kspec(0), kspec(1), kspec(2), kspec(0), kspec(1), kspec(2),
                  cache_spec, cache_spec,
                  pl.BlockSpec((None, NA_HEADS, NA_QBLK, NA_KBLKS * NA_QBLK),
                               lambda j, b: (case(j), 0, 0, 0))],
        out_specs=pl.BlockSpec((NA_QBLK, NA_WIDTH), lambda j, b: (b * nblk + j, 0)),
        out_shape=jax.ShapeDtypeStruct((n_batch * DEC_SEQ, NA_WIDTH), BF16),
        compiler_params=_cparams(2),
        name="na_attention",
    )(qa, ka, ka, ka, va, va, va, cache_k, cache_v, bias)


def _outproj_kernel(parts, x_ref, mod_ref, gffn_ref, wout_ref, *refs):
    part_refs = refs[:len(parts)]
    x1_ref, h2_ref, mix_ref = refs[len(parts):]
    off = 0
    for (kind, n), ref in zip(parts, part_refs):
        if kind == "heads":
            for p in range(n // 2):
                mix_ref[:, off:off + LANES] = ref[2 * p] + ref[2 * p + 1]
                off += LANES
        else:
            mix_ref[:, off:off + n] = ref[...]
            off += n
    x1 = x_ref[...] + mod_ref[2:3, :] * _dot(mix_ref[...], wout_ref[...])
    x1_ref[...] = x1
    h2 = _rms(x1, gffn_ref[...]) * (1.0 + mod_ref[4:5, :]) + mod_ref[3:4, :]
    h2_ref[...] = h2.astype(BF16)


def _out_project(x, mods, mod_row, lw, parts, arrays, name):
    t = x.shape[0]
    in_specs = [
        pl.BlockSpec((TM, D_MODEL), lambda i: (i, 0)),
        pl.BlockSpec((None, 6, D_MODEL), lambda i: (mod_row(i), 0, 0)),
        pl.BlockSpec((1, D_MODEL), lambda i: (0, 0)),
        pl.BlockSpec((D_MODEL, D_MODEL), lambda i: (0, 0)),
    ]
    for kind, n in parts:
        if kind == "heads":
            in_specs.append(pl.BlockSpec((n, TM, LANES), lambda i: (0, i, 0)))
        else:
            in_specs.append(pl.BlockSpec((TM, n), lambda i: (i, 0)))
    return pl.pallas_call(
        functools.partial(_outproj_kernel, parts),
        grid=(t // TM,),
        in_specs=in_specs,
        out_specs=[pl.BlockSpec((TM, D_MODEL), lambda i: (i, 0)),
                   pl.BlockSpec((TM, D_MODEL), lambda i: (i, 0))],
        out_shape=[jax.ShapeDtypeStruct((t, D_MODEL), F32), jax.ShapeDtypeStruct((t, D_MODEL), BF16)],
        scratch_shapes=[pltpu.VMEM((TM, D_MODEL), BF16)],
        compiler_params=_cparams(1),
        name=name,
    )(x, mods, lw["g_ffn"], lw["w_out"], *arrays)


def _ffn_kernel(seq_len, final, x1_ref, hp_ref, hm_ref, hn_ref, mod_ref, win_ref, cw_ref, cb_ref, wout_ref,
                *refs):
    if final:
        gfin_ref, o_ref, hext_ref, acc_ref = refs
    else:
        o_ref, hext_ref, acc_ref = refs
    tm = hm_ref.shape[0]
    ext = tm + 2 * BF16_ROWS
    hext_ref[0:BF16_ROWS] = hp_ref[...]
    hext_ref[BF16_ROWS:BF16_ROWS + tm] = hm_ref[...]
    hext_ref[BF16_ROWS + tm:ext] = hn_ref[...]
    row = lax.broadcasted_iota(jnp.int32, (ext, 1), 0) + (pl.program_id(0) * tm - BF16_ROWS)
    pos = row & (seq_len - 1)
    is_first = pos == 0
    is_last = pos == seq_len - 1
    hext = hext_ref[...]
    for c in range(N_FF_CHUNKS):
        cols = slice(c * 2 * FF_CHUNK, (c + 1) * 2 * FF_CHUNK)
        a = _dot(hext, win_ref[:, cols])
        a_prev = jnp.where(is_first, 0.0, pltpu.roll(a, 1, axis=0))
        a_next = jnp.where(is_last, 0.0, pltpu.roll(a, ext - 1, axis=0))
        a = a_prev * cw_ref[0:1, cols] + a * cw_ref[1:2, cols] + a_next * cw_ref[2:3, cols] + cb_ref[:, cols]
        a = a[BF16_ROWS:BF16_ROWS + tm]
        gate = a[:, :FF_CHUNK]
        act = (gate * jax.nn.sigmoid(gate) * a[:, FF_CHUNK:]).astype(BF16)
        part = _dot(act, wout_ref[c * FF_CHUNK:(c + 1) * FF_CHUNK, :])
        if c == 0:
            acc_ref[...] = part
        else:
            acc_ref[...] += part
    y = x1_ref[...] + mod_ref[5:6, :] * acc_ref[...]
    if final:
        y = _rms(y, gfin_ref[...])
    o_ref[...] = y


def _ffn(x1, h2, mods, mod_row, lw, seq_len, g_final, name):
    t = x1.shape[0]
    nt = t // TM
    hb = TM // BF16_ROWS
    n_hblk = t // BF16_ROWS
    final = g_final is not None
    const = lambda *shape: pl.BlockSpec(shape, lambda i: (0,) * len(shape), pipeline_mode=pl.Buffered(1))
    in_specs = [
        pl.BlockSpec((TM, D_MODEL), lambda i: (i, 0)),
        pl.BlockSpec((BF16_ROWS, D_MODEL), lambda i: (jnp.maximum(i * hb - 1, 0), 0)),
        pl.BlockSpec((TM, D_MODEL), lambda i: (i, 0)),
        pl.BlockSpec((BF16_ROWS, D_MODEL), lambda i: (jnp.minimum((i + 1) * hb, n_hblk - 1), 0)),
        pl.BlockSpec((None, 6, D_MODEL), lambda i: (mod_row(i), 0, 0)),
        const(D_MODEL, 2 * D_FF),
        const(3, 2 * D_FF),
        const(1, 2 * D_FF),
        const(D_FF, D_MODEL),
    ]
    args = [x1, h2, h2, h2, mods, lw["w_ffn_in"], lw["conv_w"], lw["conv_b"], lw["w_ffn_out"]]
    if final:
        in_specs.append(pl.BlockSpec((1, D_MODEL), lambda i: (0, 0)))
        args.append(g_final)
    return pl.pallas_call(
        functools.partial(_ffn_kernel, seq_len, final),
        grid=(nt,),
        in_specs=in_specs,
        out_specs=pl.BlockSpec((TM, D_MODEL), lambda i: (i, 0)),
        out_shape=jax.ShapeDtypeStruct((t, D_MODEL), F32),
        scratch_shapes=[pltpu.VMEM((TM + 2 * BF16_ROWS, D_MODEL), BF16), pltpu.VMEM((TM, D_MODEL), F32)],
        compiler_params=_cparams(1),
        name=name,
    )(*args)


def _pack_weights(w_in, w_uq, w_ukv, w_sgu, b_sgu, w_out, w_ffn_in, ffn_conv_w, ffn_conv_b, w_ffn_out,
                  g_mix, g_cq, g_ckv, g_sgu, g_ffn):
    nl = w_in.shape[0]
    z = lambda n: jnp.zeros((nl, D_MODEL, n), F32)
    kr0 = OFF_KR - 0
    w_in_p = jnp.concatenate([w_in[..., :kr0], z(HALF), w_in[..., kr0:kr0 + MLA_ROPE],
                              z(LANES - HALF - MLA_ROPE), w_in[..., kr0 + MLA_ROPE:]], axis=-1).astype(BF16)
    dq = MLA_NOPE + MLA_ROPE
    w_uq_p = jnp.pad(w_uq.reshape(nl, MLA_Q_RANK, MLA_HEADS, dq),
                     ((0, 0), (0, 0), (0, 0), (0, LANES - dq))).reshape(nl, MLA_Q_RANK, MLA_HEADS * LANES)
    kvh = w_ukv.reshape(nl, MLA_KV_RANK, MLA_HEADS, MLA_NOPE + MLA_V)
    k_part = jnp.pad(kvh[..., :MLA_NOPE], ((0, 0), (0, 0), (0, 0), (0, LANES - MLA_NOPE)))
    w_ukv_p = jnp.concatenate([k_part.reshape(nl, MLA_KV_RANK, MLA_HEADS * LANES),
                               kvh[..., MLA_NOPE:].reshape(nl, MLA_KV_RANK, MLA_HEADS * MLA_V)], axis=-1)
    b_sgu_p = jnp.repeat(jnp.swapaxes(b_sgu, 1, 2), SGU_WIDTH // SGU_GROUPS, axis=-1)

    def chunked(a):
        lead = a.shape[:-1]
        a = a.reshape(lead + (2, N_FF_CHUNKS, FF_CHUNK))
        return jnp.swapaxes(a, -3, -2).reshape(lead + (2 * D_FF,))

    per_layer = dict(
        w_in=w_in_p, w_uq=w_uq_p.astype(BF16), w_ukv=w_ukv_p.astype(BF16), w_sgu=w_sgu.astype(BF16),
        b_sgu=b_sgu_p, w_out=w_out.astype(BF16), w_ffn_in=chunked(w_ffn_in).astype(BF16),
        conv_w=chunked(ffn_conv_w), conv_b=chunked(ffn_conv_b)[:, None, :], w_ffn_out=w_ffn_out.astype(BF16),
        g_mix=g_mix[:, None, :], g_cq=g_cq[:, None, :], g_ckv=g_ckv[:, None, :], g_sgu=g_sgu[:, None, :],
        g_ffn=g_ffn[:, None, :])
    return [{k: v[l] for k, v in per_layer.items()} for l in range(nl)]


def _rope_tables(n_tokens):
    t = jnp.arange(n_tokens)
    n_freq = MLA_ROPE // 4
    inv_freq = ROPE_THETA ** (-jnp.arange(n_freq, dtype=F32) / n_freq)
    ang_r = (t // GRID_W).astype(F32)[:, None] * inv_freq
    ang_c = (t % GRID_W).astype(F32)[:, None] * inv_freq
    ones = jnp.ones((n_tokens, HALF), F32)
    tail = LANES - HALF - MLA_ROPE
    cos = jnp.concatenate([ones, jnp.cos(ang_r), jnp.cos(ang_r), jnp.cos(ang_c), jnp.cos(ang_c),
                           ones[:, :tail]], axis=-1)
    sin = jnp.concatenate([0 * ones, -jnp.sin(ang_r), jnp.sin(ang_r), -jnp.sin(ang_c), jnp.sin(ang_c),
                           0 * ones[:, :tail]], axis=-1)
    return cos, sin


def kernel(x_prompt, x_sample, cache_na_k, cache_na_v, cache_mla_ckv, cache_mla_krope, c, c_ctx, w_mod, b_mod,
           g_mix, w_in, na_rpb, g_cq, w_uq, g_ckv, w_ukv, g_sgu, w_sgu, b_sgu, w_out, g_ffn, w_ffn_in,
           ffn_conv_w, ffn_conv_b, w_ffn_out, g_final):
    n_ctx, n_lat = x_prompt.shape[0], x_sample.shape[0]
    t_ctx, t_lat = n_ctx * SEQ, n_lat * DEC_SEQ
    layers = _pack_weights(w_in, w_uq, w_ukv, w_sgu, b_sgu, w_out, w_ffn_in, ffn_conv_w, ffn_conv_b,
                           w_ffn_out, g_mix, g_cq, g_ckv, g_sgu, g_ffn)
    g_fin = g_final[None, :]

    cond = jnp.concatenate([c_ctx[None, :], c, jnp.zeros((8 - 1 - n_lat, D_MODEL), F32)], axis=0)
    mods = _modulation(cond, w_mod, b_mod)
    rope_tabs = _rope_tables(DEC_SEQ)
    na_bias = _na_bias(na_rpb)
    kr_pad = jnp.pad(cache_mla_krope, ((0, 0), (0, 0), (0, 0), (HALF, LANES - HALF - MLA_ROPE)))
    cache_kp, cache_vp = _cache_kv(cache_mla_ckv, kr_pad, jnp.stack([lw["w_ukv"] for lw in layers]))
    cache_k = cache_na_k.reshape(n_lat, DEPTH, PAST_LEN, NA_WIDTH)
    cache_v = cache_na_v.reshape(n_lat, DEPTH, PAST_LEN, NA_WIDTH)

    lat_tiles = DEC_SEQ // TM
    ctx_row = lambda i: 0
    lat_row = lambda i: 1 + i // lat_tiles

    xp = x_prompt.reshape(t_ctx, D_MODEL)
    xs = x_sample.reshape(t_lat, D_MODEL)
    new_k, new_v, new_ckv, new_kr = [], [], [], []
    for l, lw in enumerate(layers):
        last = l == DEPTH - 1
        m = mods[l]
        qa, ka, va, qm, kp, vp, oc, ka_f, va_f, ckv_f, kr_f = _project(xp, m, ctx_row, lw, None, True, 1)
        new_k.append(ka_f)
        new_v.append(va_f)
        new_ckv.append(ckv_f)
        new_kr.append(kr_f[:, HALF:HALF + MLA_ROPE])
        o_a = _attention(qa, [(ka, va, SEQ)], n_ctx, NA_HEADS, SEQ, SEQ, True, "ctx_na")
        o_b = _attention(qm, [(kp, vp, SEQ)], n_ctx, MLA_HEADS, SEQ, SEQ, False, "ctx_mla")
        x1, h2 = _out_project(xp, m, ctx_row, lw, (("heads", NA_HEADS), ("heads", MLA_HEADS), ("flat", SGU_WIDTH)),
                              (o_a, o_b, oc), "ctx_out")
        xp = _ffn(x1, h2, m, ctx_row, lw, SEQ, g_fin if last else None, "ctx_ffn")

        qa, ka, va, qm, kp, vp, oc = _project(xs, m, lat_row, lw, rope_tabs, False, lat_tiles)
        o_a = _na_attention(qa, ka, va, cache_k, cache_v, na_bias[l], l, n_lat)
        o_b = _attention(qm, [(cache_kp[l], cache_vp[l], PAST_LEN), (kp, vp, DEC_SEQ)], n_lat, MLA_HEADS,
                         DEC_SEQ, TM, False, "lat_mla")
        x1, h2 = _out_project(xs, m, lat_row, lw, (("flat", NA_WIDTH), ("heads", MLA_HEADS), ("flat", SGU_WIDTH)),
                              (o_a, o_b, oc), "lat_out")
        xs = _ffn(x1, h2, m, lat_row, lw, DEC_SEQ, g_fin if last else None, "lat_ffn")

    def stacked(parts, tail):
        return jnp.stack([a.reshape((n_ctx, SEQ) + tail) for a in parts], axis=1)

    return (xp.reshape(n_ctx, SEQ, D_MODEL), xs.reshape(n_lat, DEC_SEQ, D_MODEL),
            stacked(new_k, (NA_HEADS, HEAD_DIM)), stacked(new_v, (NA_HEADS, HEAD_DIM)),
            stacked(new_ckv, (MLA_KV_RANK,)), stacked(new_kr, (MLA_ROPE,)))
```

```python
import functools
import math

import jax
import jax.numpy as jnp
from jax import lax
from jax.experimental import pallas as pl
from jax.experimental.pallas import tpu as pltpu

F32 = jnp.float32
BF16 = jnp.bfloat16

D_MODEL = 1024
DEPTH = 4
SEQ = 256
DEC_SEQ = 2048
PAST_LEN = 256
GRID_W = 64
HEAD_DIM = 64
NA_WIDTH = 256
NA_HEADS = 4
NA_WIN_R = 8
NA_WIN_C = 16
MLA_HEADS = 8
MLA_NOPE = 64
MLA_ROPE = 32
MLA_V = 64
MLA_WIDTH = MLA_HEADS * MLA_V
MLA_Q_RANK = 384
MLA_KV_RANK = 256
SGU_WIDTH = 256
SGU_GROUPS = 4
SGU_CHUNK = 128
D_FF = 2816
ROPE_THETA = 10000.0
EPS = 1e-6
NEG_INF = -1e30
LOG2E = math.log2(math.e)
NA_QSCALE = HEAD_DIM ** -0.5 * LOG2E
MLA_QSCALE = (MLA_NOPE + MLA_ROPE) ** -0.5 * LOG2E

LANES = 128
BF16_ROWS = 16
HALF = LANES // 2

OFF_QA, OFF_KA, OFF_VA = 0, 256, 512
OFF_CQ = 768
OFF_CKV = OFF_CQ + MLA_Q_RANK
OFF_KR = OFF_CKV + MLA_KV_RANK
OFF_UV = OFF_KR + LANES
IN_COLS = OFF_UV + 2 * SGU_WIDTH
KV_COLS = MLA_HEADS * LANES + MLA_HEADS * MLA_V

FF_CHUNK = 256
N_FF_CHUNKS = D_FF // FF_CHUNK
FF_LB = 2 * FF_CHUNK // LANES

TM = 512
NA_QROWS = 4
NA_QBLK = NA_QROWS * GRID_W
NA_KBLKS = 3
VMEM_LIMIT = 56 * 1024 * 1024


def _cparams(n_axes):
    return pltpu.CompilerParams(dimension_semantics=("arbitrary",) * n_axes,
                                vmem_limit_bytes=VMEM_LIMIT)


def _layer_spec(l, *shape, single=False):
    mode = dict(pipeline_mode=pl.Buffered(1)) if single else {}
    return pl.BlockSpec((None,) + shape, lambda *_: (l,) + (0,) * len(shape), **mode)


def _rms(x, g):
    ms = jnp.mean(x * x, axis=-1, keepdims=True)
    return x * lax.rsqrt(ms + EPS) * g


def _dot(a, b):
    return jnp.dot(a, b, preferred_element_type=F32)


def _dot_nt(a, b):
    return lax.dot_general(a, b, (((1,), (1,)), ((), ())), preferred_element_type=F32)


def _half_mask(parity):
    lane = lax.broadcasted_iota(jnp.int32, (1, LANES), 1)
    return (lane // HALF) == parity


def _keep(mask, x):
    return jnp.where(mask, x, jnp.zeros_like(x))


def _softmax_pv(scores, values):
    m = functools.reduce(jnp.maximum, [jnp.max(s, axis=-1, keepdims=True) for s in scores])
    probs = [jnp.exp2(s - m) for s in scores]
    denom = functools.reduce(jnp.add, [jnp.sum(p, axis=-1, keepdims=True) for p in probs])
    acc = functools.reduce(jnp.add, [_dot(p.astype(BF16), v) for p, v in zip(probs, values)])
    return acc / denom


def _mod_kernel(c_ref, w_ref, b_ref, o_ref):
    c = c_ref[...]
    s = c * jax.nn.sigmoid(c)
    o_ref[...] = jnp.dot(s, w_ref[...], preferred_element_type=F32,
                         precision=lax.Precision.HIGHEST) + b_ref[...]


def _modulation(cond, w_mod, b_mod):
    n = 6
    out = pl.pallas_call(
        _mod_kernel,
        grid=(DEPTH, n),
        in_specs=[
            pl.BlockSpec((8, D_MODEL), lambda l, j: (0, 0)),
            pl.BlockSpec((None, D_MODEL, D_MODEL), lambda l, j: (l, 0, j)),
            pl.BlockSpec((None, 1, D_MODEL), lambda l, j: (l, 0, j)),
        ],
        out_specs=pl.BlockSpec((None, 8, D_MODEL), lambda l, j: (l, 0, j)),
        out_shape=jax.ShapeDtypeStruct((DEPTH, 8, n * D_MODEL), F32),
        compiler_params=_cparams(2),
        name="modulation",
    )(cond, w_mod, b_mod.reshape(DEPTH, 1, n * D_MODEL))
    return out.reshape(DEPTH, 8, n, D_MODEL)


def _rope(x, cos, sin, lane_lo):
    up = pltpu.roll(x, LANES - 8, axis=1)
    dn = pltpu.roll(x, 8, axis=1)
    return x * cos + jnp.where(lane_lo, up, dn) * sin


def _proj_kernel(rope, emit_f32, x_ref, mod_ref, gmix_ref, win_ref, gcq_ref, wuq_ref, gckv_ref,
                 wukv_ref, gsgu_ref, wsgu_ref, bsg_ref, *refs):
    if rope:
        cos_ref, sin_ref = refs[:2]
        refs = refs[2:]
    qa_ref, ka_ref, va_ref, qm_ref, kp_ref, vp_ref, oc_ref = refs[:7]
    if emit_f32:
        kaf_ref, vaf_ref, ckvf_ref, krf_ref = refs[7:]

    x = x_ref[...]
    h = _rms(x, gmix_ref[...]) * (1.0 + mod_ref[1:2, :]) + mod_ref[0:1, :]
    hb = h.astype(BF16)
    tm = x.shape[0]
    lane = lax.broadcasted_iota(jnp.int32, (1, LANES), 1)
    lane_lo = (lane % 16) < 8
    if rope:
        cos = cos_ref[...]
        sin = sin_ref[...]

    qa = _dot(hb, win_ref[:, OFF_QA:OFF_QA + NA_WIDTH]) * NA_QSCALE
    ka = _dot(hb, win_ref[:, OFF_KA:OFF_KA + NA_WIDTH])
    va = _dot(hb, win_ref[:, OFF_VA:OFF_VA + NA_WIDTH])
    for p in range(NA_HEADS // 2):
        sl = slice(p * LANES, (p + 1) * LANES)
        qa_ref[p] = qa[:, sl].astype(BF16)
        ka_ref[p] = ka[:, sl].astype(BF16)
        va_ref[p] = va[:, sl].astype(BF16)
    if emit_f32:
        kaf_ref[...] = ka
        vaf_ref[...] = va

    cq = _dot(hb, win_ref[:, OFF_CQ:OFF_CQ + MLA_Q_RANK])
    cqn = _rms(cq, gcq_ref[...]).astype(BF16)
    qm = _dot(cqn, wuq_ref[...])
    for hd in range(MLA_HEADS):
        qh = qm[:, hd * LANES:(hd + 1) * LANES]
        if rope:
            qh = _rope(qh, cos, sin, lane_lo)
        qm_ref[hd] = (qh * MLA_QSCALE).astype(BF16)

    ckv = _dot(hb, win_ref[:, OFF_CKV:OFF_CKV + MLA_KV_RANK])
    ckvn = _rms(ckv, gckv_ref[...])
    kr = _dot(hb, win_ref[:, OFF_KR:OFF_KR + LANES])
    if emit_f32:
        ckvf_ref[...] = ckvn
        krf_ref[...] = kr
    if rope:
        kr = _rope(kr, cos, sin, lane_lo)
    kv = _dot(ckvn.astype(BF16), wukv_ref[...])
    for hd in range(MLA_HEADS):
        kp_ref[hd] = (kv[:, hd * LANES:(hd + 1) * LANES] + kr).astype(BF16)
    voff = MLA_HEADS * LANES
    for p in range(MLA_HEADS // 2):
        vp_ref[p] = kv[:, voff + p * LANES:voff + (p + 1) * LANES].astype(BF16)

    uv = jax.nn.gelu(_dot(hb, win_ref[:, OFF_UV:OFF_UV + 2 * SGU_WIDTH]))
    u = uv[:, :SGU_WIDTH]
    vn = _rms(uv[:, SGU_WIDTH:], gsgu_ref[...])
    even = _half_mask(0)
    for ch in range(tm // SGU_CHUNK):
        rows = slice(ch * SGU_CHUNK, (ch + 1) * SGU_CHUNK)
        for p in range(SGU_GROUPS // 2):
            sl = slice(p * LANES, (p + 1) * LANES)
            vc = vn[rows, sl]
            mixed = (_dot(wsgu_ref[2 * p], jnp.where(even, vc, 0.0).astype(BF16))
                     + _dot(wsgu_ref[2 * p + 1], jnp.where(even, 0.0, vc).astype(BF16))
                     + bsg_ref[:, sl])
            oc_ref[rows, sl] = (u[rows, sl] * mixed).astype(BF16)


def _project(x, mods, mod_row, pw, l, rope_tabs, emit_f32, tiles_per_seq):
    t = x.shape[0]
    nt = t // TM
    rope = rope_tabs is not None
    in_specs = [
        pl.BlockSpec((TM, D_MODEL), lambda i: (i, 0)),
        pl.BlockSpec((None, 6, D_MODEL), lambda i: (mod_row(i), 0, 0)),
        _layer_spec(l, 1, D_MODEL),
        _layer_spec(l, D_MODEL, IN_COLS),
        _layer_spec(l, 1, MLA_Q_RANK),
        _layer_spec(l, MLA_Q_RANK, MLA_HEADS * LANES),
        _layer_spec(l, 1, MLA_KV_RANK),
        _layer_spec(l, MLA_KV_RANK, KV_COLS),
        _layer_spec(l, 1, SGU_WIDTH),
        _layer_spec(l, SGU_GROUPS, SGU_CHUNK, SGU_CHUNK),
        _layer_spec(l, SGU_CHUNK, SGU_WIDTH),
    ]
    args = [x, mods, pw["g_mix"], pw["w_in"], pw["g_cq"], pw["w_uq"], pw["g_ckv"], pw["w_ukv"],
            pw["g_sgu"], pw["w_sgu"], pw["b_sgu"]]
    if rope:
        tab_spec = pl.BlockSpec((TM, LANES), lambda i: (i % tiles_per_seq, 0))
        in_specs += [tab_spec, tab_spec]
        args += list(rope_tabs)

    def heads(n):
        return (pl.BlockSpec((n, TM, LANES), lambda i: (0, i, 0)),
                jax.ShapeDtypeStruct((n, t, LANES), BF16))

    def flat(w, dt):
        return (pl.BlockSpec((TM, w), lambda i: (i, 0)), jax.ShapeDtypeStruct((t, w), dt))

    outs = [heads(2), heads(2), heads(2), heads(MLA_HEADS), heads(MLA_HEADS), heads(MLA_HEADS // 2),
            flat(SGU_WIDTH, BF16)]
    if emit_f32:
        outs += [flat(NA_WIDTH, F32), flat(NA_WIDTH, F32), flat(MLA_KV_RANK, F32), flat(LANES, F32)]
    return pl.pallas_call(
        functools.partial(_proj_kernel, rope, emit_f32),
        grid=(nt,),
        in_specs=in_specs,
        out_specs=[o[0] for o in outs],
        out_shape=[o[1] for o in outs],
        compiler_params=_cparams(1),
        name="project_lat" if rope else "project_ctx",
    )(*args)


def _cache_kv_kernel(ckv_ref, kr_ref, wukv_ref, kp_ref, vp_ref):
    kv = _dot(ckv_ref[...].astype(BF16), wukv_ref[...])
    kr = kr_ref[...]
    for hd in range(MLA_HEADS):
        kp_ref[hd] = (kv[:, hd * LANES:(hd + 1) * LANES] + kr).astype(BF16)
    voff = MLA_HEADS * LANES
    for p in range(MLA_HEADS // 2):
        vp_ref[p] = kv[:, voff + p * LANES:voff + (p + 1) * LANES].astype(BF16)


def _cache_kv(cache_ckv, cache_kr_pad, w_ukv):
    b = cache_ckv.shape[0]
    return pl.pallas_call(
        _cache_kv_kernel,
        grid=(DEPTH, b),
        in_specs=[
            pl.BlockSpec((None, None, PAST_LEN, MLA_KV_RANK), lambda l, i: (i, l, 0, 0)),
            pl.BlockSpec((None, None, PAST_LEN, LANES), lambda l, i: (i, l, 0, 0)),
            pl.BlockSpec((None, MLA_KV_RANK, KV_COLS), lambda l, i: (l, 0, 0)),
        ],
        out_specs=[
            pl.BlockSpec((None, MLA_HEADS, PAST_LEN, LANES), lambda l, i: (l, 0, i, 0)),
            pl.BlockSpec((None, MLA_HEADS // 2, PAST_LEN, LANES), lambda l, i: (l, 0, i, 0)),
        ],
        out_shape=[
            jax.ShapeDtypeStruct((DEPTH, MLA_HEADS, b * PAST_LEN, LANES), BF16),
            jax.ShapeDtypeStruct((DEPTH, MLA_HEADS // 2, b * PAST_LEN, LANES), BF16),
        ],
        compiler_params=_cparams(2),
        name="cache_kv",
    )(cache_ckv, cache_kr_pad, w_ukv)


def _ctx_attn_kernel(qa_ref, ka_ref, va_ref, qm_ref, kp_ref, vp_ref, o_ref):
    for p in range(NA_HEADS // 2 + MLA_HEADS // 2):
        out = None
        for half in range(2):
            own = _half_mask(half)
            if p < NA_HEADS // 2:
                q = _keep(own, qa_ref[p])
                k = ka_ref[p]
                v = va_ref[p]
            else:
                hd = 2 * (p - NA_HEADS // 2) + half
                q = qm_ref[hd]
                k = kp_ref[hd]
                v = vp_ref[hd // 2]
            o = _softmax_pv([_dot_nt(q, k)], [_keep(own, v)])
            out = o if out is None else out + o
        o_ref[:, p * LANES:(p + 1) * LANES] = out.astype(BF16)


def _ctx_attention(qa, ka, va, qm, kp, vp, n_seq):
    heads = lambda n: pl.BlockSpec((n, SEQ, LANES), lambda b: (0, b, 0))
    width = NA_WIDTH + MLA_WIDTH
    return pl.pallas_call(
        _ctx_attn_kernel,
        grid=(n_seq,),
        in_specs=[heads(2), heads(2), heads(2), heads(MLA_HEADS), heads(MLA_HEADS), heads(MLA_HEADS // 2)],
        out_specs=pl.BlockSpec((SEQ, width), lambda b: (b, 0)),
        out_shape=jax.ShapeDtypeStruct((n_seq * SEQ, width), BF16),
        compiler_params=_cparams(1),
        name="ctx_attention",
    )(qa, ka, va, qm, kp, vp)


def _lat_mla_kernel(q_ref, kc_ref, kl_ref, vc_ref, vl_ref, o_ref, s_ref, m_ref, p_ref):
    nc = kc_ref.shape[1]
    denom = {}
    outs = {}

    def lane_block_max(s):
        return functools.reduce(jnp.maximum, [s[:, j * LANES:(j + 1) * LANES] for j in range(s.shape[1] // LANES)])

    def scores(hd):
        q = q_ref[hd]
        sc = _dot_nt(q, kc_ref[hd])
        sl = _dot_nt(q, kl_ref[hd])
        s_ref[hd % 2, :, 0:nc] = sc
        s_ref[hd % 2, :, nc:] = sl
        m_ref[hd % 2] = jnp.maximum(lane_block_max(sc), lane_block_max(sl))

    def softmax(hd):
        m = jnp.max(m_ref[hd % 2], axis=-1, keepdims=True)
        p = jnp.exp2(s_ref[hd % 2] - m)
        denom[hd] = jnp.sum(p, axis=-1, keepdims=True)
        p_ref[hd % 2] = p.astype(BF16)

    def values(hd):
        own = _half_mask(hd % 2)
        o = (_dot(p_ref[hd % 2, :, 0:nc], _keep(own, vc_ref[hd // 2]))
             + _dot(p_ref[hd % 2, :, nc:], _keep(own, vl_ref[hd // 2]))) / denom.pop(hd)
        if hd % 2 == 0:
            outs[hd // 2] = o
        else:
            o_ref[:, (hd // 2) * LANES:(hd // 2 + 1) * LANES] = (outs.pop(hd // 2) + o).astype(BF16)

    for t in range(MLA_HEADS + 2):
        if t < MLA_HEADS:
            scores(t)
        if 0 <= t - 1 < MLA_HEADS:
            softmax(t - 1)
        if 0 <= t - 2 < MLA_HEADS:
            values(t - 2)


def _lat_mla(q, kc, vc, kl, vl, n_batch):
    nq = DEC_SEQ // TM
    n_keys = PAST_LEN + DEC_SEQ
    return pl.pallas_call(
        _lat_mla_kernel,
        grid=(n_batch, nq),
        in_specs=[pl.BlockSpec((MLA_HEADS, TM, LANES), lambda b, i: (0, b * nq + i, 0)),
                  pl.BlockSpec((MLA_HEADS, PAST_LEN, LANES), lambda b, i: (0, b, 0)),
                  pl.BlockSpec((MLA_HEADS, DEC_SEQ, LANES), lambda b, i: (0, b, 0)),
                  pl.BlockSpec((MLA_HEADS // 2, PAST_LEN, LANES), lambda b, i: (0, b, 0)),
                  pl.BlockSpec((MLA_HEADS // 2, DEC_SEQ, LANES), lambda b, i: (0, b, 0))],
        out_specs=pl.BlockSpec((TM, MLA_WIDTH), lambda b, i: (b * nq + i, 0)),
        out_shape=jax.ShapeDtypeStruct((n_batch * DEC_SEQ, MLA_WIDTH), BF16),
        scratch_shapes=[pltpu.VMEM((2, TM, n_keys), F32), pltpu.VMEM((2, TM, LANES), F32),
                        pltpu.VMEM((2, TM, n_keys), BF16)],
        compiler_params=_cparams(2),
        name="lat_mla",
    )(q, kc, kl, vc, vl)


def _na_bias_kernel(rpb_ref, o_ref, tab_ref):
    l = pl.program_id(0)
    hd = pl.program_id(1)
    n_dr = 2 * NA_WIN_R - 1
    n_dc = 2 * NA_WIN_C - 1
    base = (l * NA_HEADS + hd) * n_dr * n_dc
    qc = lax.broadcasted_iota(jnp.int32, (GRID_W, LANES), 0)
    kc = lax.broadcasted_iota(jnp.int32, (GRID_W, LANES), 1) % GRID_W
    diff = kc - qc
    cs = jnp.clip(qc - NA_WIN_C // 2, 0, GRID_W - NA_WIN_C)
    in_win = (kc >= cs) & (kc < cs + NA_WIN_C)
    neg = jnp.full((GRID_W, LANES), NEG_INF * LOG2E, F32)
    for dr in range(n_dr):
        acc = neg
        for dc in range(n_dc):
            acc = jnp.where(diff == dc - (NA_WIN_C - 1), rpb_ref[base + dr * n_dc + dc] * LOG2E, acc)
        tab_ref[dr] = jnp.where(in_win, acc, neg)
    left = lax.broadcasted_iota(jnp.int32, (GRID_W, LANES), 1) < GRID_W
    n_krows = NA_KBLKS * NA_QBLK // GRID_W
    cases = ((0, lambda qr: 0), (-NA_QROWS, lambda qr: qr), (-2 * NA_QROWS, lambda qr: NA_QROWS))
    for c, (shift, first) in enumerate(cases):
        for qr in range(NA_QROWS):
            for kp in range(n_krows // 2):
                tiles = []
                for kr in (2 * kp, 2 * kp + 1):
                    ok = first(qr) <= kr < first(qr) + NA_WIN_R
                    tiles.append(tab_ref[kr - qr + shift + NA_WIN_R - 1] if ok else neg)
                o_ref[c, qr * GRID_W:(qr + 1) * GRID_W, kp * LANES:(kp + 1) * LANES] = (
                    jnp.where(left, tiles[0], tiles[1]))


def _na_bias(na_rpb):
    n_k = NA_KBLKS * NA_QBLK
    return pl.pallas_call(
        _na_bias_kernel,
        grid=(DEPTH, NA_HEADS),
        in_specs=[pl.BlockSpec(memory_space=pltpu.SMEM)],
        out_specs=pl.BlockSpec((None, 3, None, NA_QBLK, n_k), lambda l, h: (l, 0, h, 0, 0)),
        out_shape=jax.ShapeDtypeStruct((DEPTH, 3, NA_HEADS, NA_QBLK, n_k), F32),
        scratch_shapes=[pltpu.VMEM((2 * NA_WIN_R - 1, GRID_W, LANES), F32)],
        compiler_params=_cparams(2),
        name="na_bias",
    )(na_rpb.reshape(-1))


def _na_kernel(q_ref, k0_ref, k1_ref, k2_ref, v0_ref, v1_ref, v2_ref, kc_ref, vc_ref, bias_ref, o_ref):
    k_refs = (k0_ref, k1_ref, k2_ref)
    v_refs = (v0_ref, v1_ref, v2_ref)
    for p in range(NA_HEADS // 2):
        sl = slice(p * LANES, (p + 1) * LANES)
        kc = kc_ref[:, sl].astype(BF16)
        vc = vc_ref[:, sl].astype(BF16)
        out = None
        for half in range(2):
            hd = 2 * p + half
            own = _half_mask(half)
            q = _keep(own, q_ref[p])
            scores = [_dot_nt(q, k_refs[i][p]) + bias_ref[hd, :, i * NA_QBLK:(i + 1) * NA_QBLK]
                      for i in range(NA_KBLKS)]
            scores.append(_dot_nt(q, kc))
            vals = [_keep(own, v_refs[i][p]) for i in range(NA_KBLKS)] + [_keep(own, vc)]
            o = _softmax_pv(scores, vals)
            out = o if out is None else out + o
        o_ref[:, sl] = out.astype(BF16)


def _na_attention(qa, ka, va, cache_k, cache_v, bias, l, n_batch):
    nblk = DEC_SEQ // NA_QBLK
    max_start = nblk - NA_KBLKS

    def kspec(i):
        return pl.BlockSpec((NA_HEADS // 2, NA_QBLK, LANES),
                            lambda j, b: (0, b * nblk + jnp.clip(j - 1, 0, max_start) + i, 0))

    case = lambda j: jnp.where(j == 0, 0, jnp.where(j == nblk - 1, 2, 1))
    cache_spec = pl.BlockSpec((None, None, PAST_LEN, NA_WIDTH), lambda j, b: (b, l, 0, 0))
    return pl.pallas_call(
        _na_kernel,
        grid=(nblk, n_batch),
        in_specs=[pl.BlockSpec((NA_HEADS // 2, NA_QBLK, LANES), lambda j, b: (0, b * nblk + j, 0)),
                  kspec(0), kspec(1), kspec(2), kspec(0), kspec(1), kspec(2),
                  cache_spec, cache_spec,
                  pl.BlockSpec((None, None, NA_HEADS, NA_QBLK, NA_KBLKS * NA_QBLK),
                               lambda j, b: (l, case(j), 0, 0, 0))],
        out_specs=pl.BlockSpec((NA_QBLK, NA_WIDTH), lambda j, b: (b * nblk + j, 0)),
        out_shape=jax.ShapeDtypeStruct((n_batch * DEC_SEQ, NA_WIDTH), BF16),
        compiler_params=_cparams(2),
        name="na_attention",
    )(qa, ka, ka, ka, va, va, va, cache_k, cache_v, bias)


def _outproj_kernel(widths, x_ref, mod_ref, gffn_ref, wout_ref, *refs):
    part_refs = refs[:len(widths)]
    x1_ref, h2_ref = refs[len(widths):]
    acc = None
    off = 0
    for n, ref in zip(widths, part_refs):
        part = _dot(ref[...], wout_ref[off:off + n, :])
        acc = part if acc is None else acc + part
        off += n
    x1 = x_ref[...] + mod_ref[2:3, :] * acc
    x1_ref[...] = x1
    h2 = _rms(x1, gffn_ref[...]) * (1.0 + mod_ref[4:5, :]) + mod_ref[3:4, :]
    h2_ref[...] = h2.astype(BF16)


def _out_project(x, mods, mod_row, pw, l, parts, name):
    t = x.shape[0]
    widths = tuple(p.shape[1] for p in parts)
    tile = pl.BlockSpec((TM, D_MODEL), lambda i: (i, 0))
    in_specs = [tile,
                pl.BlockSpec((None, 6, D_MODEL), lambda i: (mod_row(i), 0, 0)),
                _layer_spec(l, 1, D_MODEL),
                _layer_spec(l, D_MODEL, D_MODEL)]
    in_specs += [pl.BlockSpec((TM, n), lambda i: (i, 0)) for n in widths]
    return pl.pallas_call(
        functools.partial(_outproj_kernel, widths),
        grid=(t // TM,),
        in_specs=in_specs,
        out_specs=[tile, tile],
        out_shape=[jax.ShapeDtypeStruct((t, D_MODEL), F32), jax.ShapeDtypeStruct((t, D_MODEL), BF16)],
        compiler_params=_cparams(1),
        name=name,
    )(x, mods, pw["g_ffn"], pw["w_out"], *parts)


def _ffn_kernel(seq_len, final, *refs):
    halo = seq_len > TM
    if halo:
        x1_ref, hp_ref, hm_ref, hn_ref = refs[:4]
        refs = refs[4:]
    else:
        x1_ref, hm_ref = refs[:2]
        refs = refs[2:]
    mod_ref, win_ref, cw_ref, cb_ref, wout_ref = refs[:5]
    refs = refs[5:]
    if final:
        gfin_ref = refs[0]
        refs = refs[1:]
    o_ref, hext_ref, a_ref, act_ref = refs
    tm = hm_ref.shape[0]
    pad = BF16_ROWS
    zero = jnp.zeros((pad, D_MODEL), BF16)
    if halo:
        tiles_per_seq = seq_len // tm
        pos = pl.program_id(0) % tiles_per_seq
        hext_ref[0:pad] = jnp.where(pos == 0, zero, hp_ref[...])
        hext_ref[pad:pad + tm] = hm_ref[...]
        hext_ref[pad + tm:2 * pad + tm] = jnp.where(pos == tiles_per_seq - 1, zero, hn_ref[...])
        starts = [pad]
        seg_len = tm
    else:
        nseg = tm // seq_len
        starts = [pad + s * (seq_len + pad) for s in range(nseg)]
        seg_len = seq_len
        for s in range(nseg + 1):
            hext_ref[s * (seq_len + pad):s * (seq_len + pad) + pad] = zero
        for s in range(nseg):
            hext_ref[starts[s]:starts[s] + seq_len] = hm_ref[s * seq_len:(s + 1) * seq_len]

    def col(c, lb):
        half = FF_LB // 2
        return (lb // half) * D_FF + c * FF_CHUNK + (lb % half) * LANES

    def up(c):
        hext = hext_ref[...]
        for part in range(2):
            a = _dot(hext, win_ref[:, part * D_FF + c * FF_CHUNK:part * D_FF + (c + 1) * FF_CHUNK])
            for j in range(FF_LB // 2):
                a_ref[c % 2, part * (FF_LB // 2) + j] = a[:, j * LANES:(j + 1) * LANES]

    def conv(c, lb, st):
        cs = slice(col(c, lb), col(c, lb) + LANES)
        return (a_ref[c % 2, lb, pl.ds(st - 1, seg_len, stride=1), :] * cw_ref[0:1, cs]
                + a_ref[c % 2, lb, st:st + seg_len, :] * cw_ref[1:2, cs]
                + a_ref[c % 2, lb, pl.ds(st + 1, seg_len, stride=1), :] * cw_ref[2:3, cs] + cb_ref[:, cs])

    up(0)
    for c in range(N_FF_CHUNKS):
        if c + 1 < N_FF_CHUNKS:
            up(c + 1)
        for s, st in enumerate(starts):
            for lb in range(FF_LB // 2):
                gate = conv(c, lb, st)
                val = conv(c, lb + FF_LB // 2, st)
                act_ref[s * seg_len:(s + 1) * seg_len, c * FF_CHUNK + lb * LANES:c * FF_CHUNK + (lb + 1) * LANES] = (
                    gate * jax.nn.sigmoid(gate) * val).astype(BF16)
    y = x1_ref[...] + mod_ref[5:6, :] * _dot(act_ref[...], wout_ref[...])
    if final:
        y = _rms(y, gfin_ref[...])
    o_ref[...] = y


def _ffn(x1, h2, mods, mod_row, pw, l, seq_len, g_final, name):
    t = x1.shape[0]
    nt = t // TM
    hb = TM // BF16_ROWS
    n_hblk = t // BF16_ROWS
    final = g_final is not None
    halo = seq_len > TM
    ext = TM + 2 * BF16_ROWS if halo else TM + (TM // seq_len + 1) * BF16_ROWS
    tile = pl.BlockSpec((TM, D_MODEL), lambda i: (i, 0))
    if halo:
        in_specs = [tile,
                    pl.BlockSpec((BF16_ROWS, D_MODEL), lambda i: (jnp.maximum(i * hb - 1, 0), 0)),
                    tile,
                    pl.BlockSpec((BF16_ROWS, D_MODEL), lambda i: (jnp.minimum((i + 1) * hb, n_hblk - 1), 0))]
        args = [x1, h2, h2, h2]
    else:
        in_specs = [tile, tile]
        args = [x1, h2]
    in_specs += [pl.BlockSpec((None, 6, D_MODEL), lambda i: (mod_row(i), 0, 0)),
                 _layer_spec(l, D_MODEL, 2 * D_FF, single=True),
                 _layer_spec(l, 3, 2 * D_FF, single=True),
                 _layer_spec(l, 1, 2 * D_FF, single=True),
                 _layer_spec(l, D_FF, D_MODEL, single=True)]
    args += [mods, pw["w_ffn_in"], pw["conv_w"], pw["conv_b"], pw["w_ffn_out"]]
    if final:
        in_specs.append(pl.BlockSpec((1, D_MODEL), lambda i: (0, 0)))
        args.append(g_final)
    return pl.pallas_call(
        functools.partial(_ffn_kernel, seq_len, final),
        grid=(nt,),
        in_specs=in_specs,
        out_specs=tile,
        out_shape=jax.ShapeDtypeStruct((t, D_MODEL), F32),
        scratch_shapes=[pltpu.VMEM((ext, D_MODEL), BF16), pltpu.VMEM((2, FF_LB, ext, LANES), F32),
                        pltpu.VMEM((TM, D_FF), BF16)],
        compiler_params=_cparams(1),
        name=name,
    )(*args)


def _pack_weights(w_in, w_uq, w_ukv, w_sgu, b_sgu, w_out, w_ffn_in, ffn_conv_w, ffn_conv_b, w_ffn_out,
                  g_mix, g_cq, g_ckv, g_sgu, g_ffn):
    nl = w_in.shape[0]
    z = lambda n: jnp.zeros((nl, D_MODEL, n), BF16)
    w_in = w_in.astype(BF16)
    w_in_p = jnp.concatenate([w_in[..., :OFF_KR], z(HALF), w_in[..., OFF_KR:OFF_KR + MLA_ROPE],
                              z(LANES - HALF - MLA_ROPE), w_in[..., OFF_KR + MLA_ROPE:]], axis=-1)
    dq = MLA_NOPE + MLA_ROPE
    w_uq_p = jnp.pad(w_uq.astype(BF16).reshape(nl, MLA_Q_RANK, MLA_HEADS, dq),
                     ((0, 0), (0, 0), (0, 0), (0, LANES - dq))).reshape(nl, MLA_Q_RANK, MLA_HEADS * LANES)
    kvh = w_ukv.astype(BF16).reshape(nl, MLA_KV_RANK, MLA_HEADS, MLA_NOPE + MLA_V)
    k_part = jnp.pad(kvh[..., :MLA_NOPE], ((0, 0), (0, 0), (0, 0), (0, LANES - MLA_NOPE)))
    w_ukv_p = jnp.concatenate([k_part.reshape(nl, MLA_KV_RANK, MLA_HEADS * LANES),
                               kvh[..., MLA_NOPE:].reshape(nl, MLA_KV_RANK, MLA_HEADS * MLA_V)], axis=-1)
    b_sgu_p = jnp.repeat(jnp.swapaxes(b_sgu, 1, 2), SGU_WIDTH // SGU_GROUPS, axis=-1)
    return dict(
        w_in=w_in_p, w_uq=w_uq_p, w_ukv=w_ukv_p, w_sgu=w_sgu.astype(BF16), b_sgu=b_sgu_p,
        w_out=w_out.astype(BF16), w_ffn_in=w_ffn_in.astype(BF16), conv_w=ffn_conv_w,
        conv_b=ffn_conv_b[:, None, :], w_ffn_out=w_ffn_out.astype(BF16),
        g_mix=g_mix[:, None, :], g_cq=g_cq[:, None, :], g_ckv=g_ckv[:, None, :], g_sgu=g_sgu[:, None, :],
        g_ffn=g_ffn[:, None, :])


def _rope_tables(n_tokens):
    t = jnp.arange(n_tokens)
    n_freq = MLA_ROPE // 4
    inv_freq = ROPE_THETA ** (-jnp.arange(n_freq, dtype=F32) / n_freq)
    ang_r = (t // GRID_W).astype(F32)[:, None] * inv_freq
    ang_c = (t % GRID_W).astype(F32)[:, None] * inv_freq
    ones = jnp.ones((n_tokens, HALF), F32)
    tail = LANES - HALF - MLA_ROPE
    cos = jnp.concatenate([ones, jnp.cos(ang_r), jnp.cos(ang_r), jnp.cos(ang_c), jnp.cos(ang_c),
                           ones[:, :tail]], axis=-1)
    sin = jnp.concatenate([0 * ones, -jnp.sin(ang_r), jnp.sin(ang_r), -jnp.sin(ang_c), jnp.sin(ang_c),
                           0 * ones[:, :tail]], axis=-1)
    return cos, sin


def kernel(x_prompt, x_sample, cache_na_k, cache_na_v, cache_mla_ckv, cache_mla_krope, c, c_ctx, w_mod, b_mod,
           g_mix, w_in, na_rpb, g_cq, w_uq, g_ckv, w_ukv, g_sgu, w_sgu, b_sgu, w_out, g_ffn, w_ffn_in,
           ffn_conv_w, ffn_conv_b, w_ffn_out, g_final):
    n_ctx, n_lat = x_prompt.shape[0], x_sample.shape[0]
    t_ctx, t_lat = n_ctx * SEQ, n_lat * DEC_SEQ
    pw = _pack_weights(w_in, w_uq, w_ukv, w_sgu, b_sgu, w_out, w_ffn_in, ffn_conv_w, ffn_conv_b,
                       w_ffn_out, g_mix, g_cq, g_ckv, g_sgu, g_ffn)
    g_fin = g_final[None, :]

    cond = jnp.concatenate([c_ctx[None, :], c, jnp.zeros((8 - 1 - n_lat, D_MODEL), F32)], axis=0)
    mods = _modulation(cond, w_mod, b_mod)
    rope_tabs = _rope_tables(DEC_SEQ)
    na_bias = _na_bias(na_rpb)
    kr_pad = jnp.pad(cache_mla_krope, ((0, 0), (0, 0), (0, 0), (HALF, LANES - HALF - MLA_ROPE)))
    cache_kp, cache_vp = _cache_kv(cache_mla_ckv, kr_pad, pw["w_ukv"])
    cache_k = cache_na_k.reshape(n_lat, DEPTH, PAST_LEN, NA_WIDTH)
    cache_v = cache_na_v.reshape(n_lat, DEPTH, PAST_LEN, NA_WIDTH)

    lat_tiles = DEC_SEQ // TM
    ctx_row = lambda i: 0
    lat_row = lambda i: 1 + i // lat_tiles

    xp = x_prompt.reshape(t_ctx, D_MODEL)
    xs = x_sample.reshape(t_lat, D_MODEL)
    new_k, new_v, new_ckv, new_kr = [], [], [], []
    for l in range(DEPTH):
        last = l == DEPTH - 1
        m = mods[l]
        qa, ka, va, qm, kp, vp, oc, ka_f, va_f, ckv_f, kr_f = _project(xp, m, ctx_row, pw, l, None, True, 1)
        new_k.append(ka_f)
        new_v.append(va_f)
        new_ckv.append(ckv_f)
        new_kr.append(kr_f[:, HALF:HALF + MLA_ROPE])
        o_ab = _ctx_attention(qa, ka, va, qm, kp, vp, n_ctx)
        x1, h2 = _out_project(xp, m, ctx_row, pw, l, (o_ab, oc), "ctx_out")
        xp = _ffn(x1, h2, m, ctx_row, pw, l, SEQ, g_fin if last else None, "ctx_ffn")

        qa, ka, va, qm, kp, vp, oc = _project(xs, m, lat_row, pw, l, rope_tabs, False, lat_tiles)
        o_a = _na_attention(qa, ka, va, cache_k, cache_v, na_bias, l, n_lat)
        o_b = _lat_mla(qm, cache_kp[l], cache_vp[l], kp, vp, n_lat)
        x1, h2 = _out_project(xs, m, lat_row, pw, l, (o_a, o_b, oc), "lat_out")
        xs = _ffn(x1, h2, m, lat_row, pw, l, DEC_SEQ, g_fin if last else None, "lat_ffn")

    def stacked(parts, tail):
        return jnp.stack([a.reshape((n_ctx, SEQ) + tail) for a in parts], axis=1)

    return (xp.reshape(n_ctx, SEQ, D_MODEL), xs.reshape(n_lat, DEC_SEQ, D_MODEL),
            stacked(new_k, (NA_HEADS, HEAD_DIM)), stacked(new_v, (NA_HEADS, HEAD_DIM)),
            stacked(new_ckv, (MLA_KV_RANK,)), stacked(new_kr, (MLA_ROPE,)))
```

```python
import functools
import math

import jax
import jax.numpy as jnp
from jax import lax
from jax.experimental import pallas as pl
from jax.experimental.pallas import tpu as pltpu

F32 = jnp.float32
BF16 = jnp.bfloat16

D_MODEL = 1024
DEPTH = 4
SEQ = 256
DEC_SEQ = 2048
PAST_LEN = 256
GRID_W = 64
HEAD_DIM = 64
NA_WIDTH = 256
NA_HEADS = 4
NA_WIN_R = 8
NA_WIN_C = 16
MLA_HEADS = 8
MLA_NOPE = 64
MLA_ROPE = 32
MLA_V = 64
MLA_WIDTH = MLA_HEADS * MLA_V
MLA_Q_RANK = 384
MLA_KV_RANK = 256
SGU_WIDTH = 256
SGU_GROUPS = 4
SGU_CHUNK = 128
D_FF = 2816
ROPE_THETA = 10000.0
EPS = 1e-6
NEG_INF = -1e30
LOG2E = math.log2(math.e)
NA_QSCALE = HEAD_DIM ** -0.5 * LOG2E
MLA_QSCALE = (MLA_NOPE + MLA_ROPE) ** -0.5 * LOG2E

LANES = 128
BF16_ROWS = 16
HALF = LANES // 2

OFF_QA, OFF_KA, OFF_VA = 0, 256, 512
OFF_CQ = 768
OFF_CKV = OFF_CQ + MLA_Q_RANK
OFF_KR = OFF_CKV + MLA_KV_RANK
OFF_UV = OFF_KR + LANES
IN_COLS = OFF_UV + 2 * SGU_WIDTH
KV_COLS = MLA_HEADS * LANES + MLA_HEADS * MLA_V

FF_CHUNK = 256
N_FF_CHUNKS = D_FF // FF_CHUNK
FF_LB = 2 * FF_CHUNK // LANES

TM = 512
NA_QROWS = 4
NA_QBLK = NA_QROWS * GRID_W
NA_KBLKS = 3
VMEM_LIMIT = 56 * 1024 * 1024


def _cparams(n_axes):
    return pltpu.CompilerParams(dimension_semantics=("arbitrary",) * n_axes,
                                vmem_limit_bytes=VMEM_LIMIT)


def _layer_spec(l, *shape, single=False):
    mode = dict(pipeline_mode=pl.Buffered(1)) if single else {}
    return pl.BlockSpec((None,) + shape, lambda *_: (l,) + (0,) * len(shape), **mode)


def _rms(x, g):
    ms = jnp.mean(x * x, axis=-1, keepdims=True)
    return x * lax.rsqrt(ms + EPS) * g


def _dot(a, b):
    return jnp.dot(a, b, preferred_element_type=F32)


def _dot_nt(a, b):
    return lax.dot_general(a, b, (((1,), (1,)), ((), ())), preferred_element_type=F32)


def _half_mask(parity):
    lane = lax.broadcasted_iota(jnp.int32, (1, LANES), 1)
    return (lane // HALF) == parity


def _keep(mask, x):
    return jnp.where(mask, x, jnp.zeros_like(x))


def _softmax_pv(scores, values):
    m = functools.reduce(jnp.maximum, [jnp.max(s, axis=-1, keepdims=True) for s in scores])
    probs = [jnp.exp2(s - m) for s in scores]
    denom = functools.reduce(jnp.add, [jnp.sum(p, axis=-1, keepdims=True) for p in probs])
    acc = functools.reduce(jnp.add, [_dot(p.astype(BF16), v) for p, v in zip(probs, values)])
    return acc / denom


def _mod_kernel(c_ref, w_ref, b_ref, o_ref):
    c = c_ref[...]
    s = c * jax.nn.sigmoid(c)
    o_ref[...] = jnp.dot(s, w_ref[...], preferred_element_type=F32,
                         precision=lax.Precision.HIGHEST) + b_ref[...]


def _modulation(cond, w_mod, b_mod):
    n = 6
    wide = 2 * D_MODEL
    out = pl.pallas_call(
        _mod_kernel,
        grid=(DEPTH, n * D_MODEL // wide),
        in_specs=[
            pl.BlockSpec((8, D_MODEL), lambda l, j: (0, 0)),
            pl.BlockSpec((None, D_MODEL, wide), lambda l, j: (l, 0, j)),
            pl.BlockSpec((None, 1, wide), lambda l, j: (l, 0, j)),
        ],
        out_specs=pl.BlockSpec((None, 8, wide), lambda l, j: (l, 0, j)),
        out_shape=jax.ShapeDtypeStruct((DEPTH, 8, n * D_MODEL), F32),
        compiler_params=_cparams(2),
        name="modulation",
    )(cond, w_mod, b_mod.reshape(DEPTH, 1, n * D_MODEL))
    return out.reshape(DEPTH, 8, n, D_MODEL)


def _rope(x, cos, sin, lane_lo):
    up = pltpu.roll(x, LANES - 8, axis=1)
    dn = pltpu.roll(x, 8, axis=1)
    return x * cos + jnp.where(lane_lo, up, dn) * sin


def _proj_kernel(rope, emit_f32, x_ref, mod_ref, gmix_ref, win_ref, gcq_ref, wuq_ref, gckv_ref,
                 wukv_ref, gsgu_ref, wsgu_ref, bsg_ref, *refs):
    if rope:
        cos_ref, sin_ref = refs[:2]
        refs = refs[2:]
    qa_ref, ka_ref, va_ref, qm_ref, kp_ref, vp_ref, oc_ref = refs[:7]
    refs = refs[7:]
    if emit_f32:
        kaf_ref, vaf_ref, ckvf_ref, krf_ref = refs[:4]
        refs = refs[4:]
    hb_ref, z_ref = refs

    x = x_ref[...]
    h = _rms(x, gmix_ref[...]) * (1.0 + mod_ref[1:2, :]) + mod_ref[0:1, :]
    hb_ref[...] = h.astype(BF16)
    tm = x.shape[0]
    z_ref[:, OFF_UV:] = _dot(hb_ref[...], win_ref[:, OFF_UV:])
    z_ref[:, OFF_CQ:OFF_UV] = _dot(hb_ref[...], win_ref[:, OFF_CQ:OFF_UV])
    z_ref[:, :OFF_CQ] = _dot(hb_ref[...], win_ref[:, :OFF_CQ])
    lane = lax.broadcasted_iota(jnp.int32, (1, LANES), 1)
    lane_lo = (lane % 16) < 8
    if rope:
        cos = cos_ref[...]
        sin = sin_ref[...]

    uv = jax.nn.gelu(z_ref[:, OFF_UV:OFF_UV + 2 * SGU_WIDTH])
    u = uv[:, :SGU_WIDTH]
    vn = _rms(uv[:, SGU_WIDTH:], gsgu_ref[...])
    even = _half_mask(0)
    for ch in range(tm // SGU_CHUNK):
        rows = slice(ch * SGU_CHUNK, (ch + 1) * SGU_CHUNK)
        for p in range(SGU_GROUPS // 2):
            sl = slice(p * LANES, (p + 1) * LANES)
            vc = vn[rows, sl]
            mixed = (_dot(wsgu_ref[2 * p], jnp.where(even, vc, 0.0).astype(BF16))
                     + _dot(wsgu_ref[2 * p + 1], jnp.where(even, 0.0, vc).astype(BF16))
                     + bsg_ref[:, sl])
            oc_ref[rows, sl] = (u[rows, sl] * mixed).astype(BF16)

    cq = z_ref[:, OFF_CQ:OFF_CQ + MLA_Q_RANK]
    cqn = _rms(cq, gcq_ref[...]).astype(BF16)
    qm = _dot(cqn, wuq_ref[...])
    for hd in range(MLA_HEADS):
        qh = qm[:, hd * LANES:(hd + 1) * LANES]
        if rope:
            qh = _rope(qh, cos, sin, lane_lo)
        qm_ref[hd] = (qh * MLA_QSCALE).astype(BF16)

    ckv = z_ref[:, OFF_CKV:OFF_CKV + MLA_KV_RANK]
    ckvn = _rms(ckv, gckv_ref[...])
    kr = z_ref[:, OFF_KR:OFF_KR + LANES]
    if emit_f32:
        ckvf_ref[...] = ckvn
        krf_ref[...] = kr
    if rope:
        kr = _rope(kr, cos, sin, lane_lo)
    kv = _dot(ckvn.astype(BF16), wukv_ref[...])
    for hd in range(MLA_HEADS):
        kp_ref[hd] = (kv[:, hd * LANES:(hd + 1) * LANES] + kr).astype(BF16)
    voff = MLA_HEADS * LANES
    for p in range(MLA_HEADS // 2):
        vp_ref[p] = kv[:, voff + p * LANES:voff + (p + 1) * LANES].astype(BF16)

    qa = z_ref[:, OFF_QA:OFF_QA + NA_WIDTH] * NA_QSCALE
    ka = z_ref[:, OFF_KA:OFF_KA + NA_WIDTH]
    va = z_ref[:, OFF_VA:OFF_VA + NA_WIDTH]
    for p in range(NA_HEADS // 2):
        sl = slice(p * LANES, (p + 1) * LANES)
        qa_ref[p] = qa[:, sl].astype(BF16)
        ka_ref[p] = ka[:, sl].astype(BF16)
        va_ref[p] = va[:, sl].astype(BF16)
    if emit_f32:
        kaf_ref[...] = ka
        vaf_ref[...] = va


def _project(x, mods, mod_row, pw, l, rope_tabs, emit_f32, tiles_per_seq):
    t = x.shape[0]
    nt = t // TM
    rope = rope_tabs is not None
    in_specs = [
        pl.BlockSpec((TM, D_MODEL), lambda i: (i, 0)),
        pl.BlockSpec((None, 6, D_MODEL), lambda i: (mod_row(i), 0, 0)),
        _layer_spec(l, 1, D_MODEL),
        _layer_spec(l, D_MODEL, IN_COLS),
        _layer_spec(l, 1, MLA_Q_RANK),
        _layer_spec(l, MLA_Q_RANK, MLA_HEADS * LANES),
        _layer_spec(l, 1, MLA_KV_RANK),
        _layer_spec(l, MLA_KV_RANK, KV_COLS),
        _layer_spec(l, 1, SGU_WIDTH),
        _layer_spec(l, SGU_GROUPS, SGU_CHUNK, SGU_CHUNK),
        _layer_spec(l, SGU_CHUNK, SGU_WIDTH),
    ]
    args = [x, mods, pw["g_mix"], pw["w_in"], pw["g_cq"], pw["w_uq"], pw["g_ckv"], pw["w_ukv"],
            pw["g_sgu"], pw["w_sgu"], pw["b_sgu"]]
    if rope:
        tab_spec = pl.BlockSpec((TM, LANES), lambda i: (i % tiles_per_seq, 0))
        in_specs += [tab_spec, tab_spec]
        args += list(rope_tabs)

    def heads(n):
        return (pl.BlockSpec((n, TM, LANES), lambda i: (0, i, 0)),
                jax.ShapeDtypeStruct((n, t, LANES), BF16))

    def flat(w, dt):
        return (pl.BlockSpec((TM, w), lambda i: (i, 0)), jax.ShapeDtypeStruct((t, w), dt))

    outs = [heads(2), heads(2), heads(2), heads(MLA_HEADS), heads(MLA_HEADS), heads(MLA_HEADS // 2),
            flat(SGU_WIDTH, BF16)]
    if emit_f32:
        outs += [flat(NA_WIDTH, F32), flat(NA_WIDTH, F32), flat(MLA_KV_RANK, F32), flat(LANES, F32)]
    return pl.pallas_call(
        functools.partial(_proj_kernel, rope, emit_f32),
        grid=(nt,),
        in_specs=in_specs,
        out_specs=[o[0] for o in outs],
        out_shape=[o[1] for o in outs],
        scratch_shapes=[pltpu.VMEM((TM, D_MODEL), BF16), pltpu.VMEM((TM, IN_COLS), F32)],
        compiler_params=_cparams(1),
        name="project_lat" if rope else "project_ctx",
    )(*args)


def _cache_kv_kernel(ckv_ref, kr_ref, wukv_ref, kp_ref, vp_ref):
    kv = _dot(ckv_ref[...].astype(BF16), wukv_ref[...])
    kr = kr_ref[...]
    for hd in range(MLA_HEADS):
        kp_ref[hd] = (kv[:, hd * LANES:(hd + 1) * LANES] + kr).astype(BF16)
    voff = MLA_HEADS * LANES
    for p in range(MLA_HEADS // 2):
        vp_ref[p] = kv[:, voff + p * LANES:voff + (p + 1) * LANES].astype(BF16)


def _cache_kv(cache_ckv, cache_kr_pad, w_ukv):
    b = cache_ckv.shape[0]
    return pl.pallas_call(
        _cache_kv_kernel,
        grid=(DEPTH, b),
        in_specs=[
            pl.BlockSpec((None, None, PAST_LEN, MLA_KV_RANK), lambda l, i: (i, l, 0, 0)),
            pl.BlockSpec((None, None, PAST_LEN, LANES), lambda l, i: (i, l, 0, 0)),
            pl.BlockSpec((None, MLA_KV_RANK, KV_COLS), lambda l, i: (l, 0, 0)),
        ],
        out_specs=[
            pl.BlockSpec((None, MLA_HEADS, PAST_LEN, LANES), lambda l, i: (l, 0, i, 0)),
            pl.BlockSpec((None, MLA_HEADS // 2, PAST_LEN, LANES), lambda l, i: (l, 0, i, 0)),
        ],
        out_shape=[
            jax.ShapeDtypeStruct((DEPTH, MLA_HEADS, b * PAST_LEN, LANES), BF16),
            jax.ShapeDtypeStruct((DEPTH, MLA_HEADS // 2, b * PAST_LEN, LANES), BF16),
        ],
        compiler_params=_cparams(2),
        name="cache_kv",
    )(cache_ckv, cache_kr_pad, w_ukv)


def _ctx_attn_kernel(qa_ref, ka_ref, va_ref, qm_ref, kp_ref, vp_ref, o_ref):
    for p in range(NA_HEADS // 2 + MLA_HEADS // 2):
        out = None
        for half in range(2):
            own = _half_mask(half)
            if p < NA_HEADS // 2:
                q = _keep(own, qa_ref[p])
                k = ka_ref[p]
                v = va_ref[p]
            else:
                hd = 2 * (p - NA_HEADS // 2) + half
                q = qm_ref[hd]
                k = kp_ref[hd]
                v = vp_ref[hd // 2]
            o = _softmax_pv([_dot_nt(q, k)], [_keep(own, v)])
            out = o if out is None else out + o
        o_ref[:, p * LANES:(p + 1) * LANES] = out.astype(BF16)


def _ctx_attention(qa, ka, va, qm, kp, vp, n_seq):
    heads = lambda n: pl.BlockSpec((n, SEQ, LANES), lambda b: (0, b, 0))
    width = NA_WIDTH + MLA_WIDTH
    return pl.pallas_call(
        _ctx_attn_kernel,
        grid=(n_seq,),
        in_specs=[heads(2), heads(2), heads(2), heads(MLA_HEADS), heads(MLA_HEADS), heads(MLA_HEADS // 2)],
        out_specs=pl.BlockSpec((SEQ, width), lambda b: (b, 0)),
        out_shape=jax.ShapeDtypeStruct((n_seq * SEQ, width), BF16),
        compiler_params=_cparams(1),
        name="ctx_attention",
    )(qa, ka, va, qm, kp, vp)


def _lat_mla_kernel(q_ref, kc_ref, kl_ref, vc_ref, vl_ref, o_ref, s_ref, m_ref, p_ref):
    nc = kc_ref.shape[1]
    denom = {}
    outs = {}

    def scores(hd):
        q = q_ref[hd]
        sc = _dot_nt(q, kc_ref[hd])
        sl = _dot_nt(q, kl_ref[hd])
        s_ref[hd % 2, :, 0:nc] = sc
        s_ref[hd % 2, :, nc:] = sl
        m_ref[hd % 2] = jnp.maximum(_lane_block_max(sc), _lane_block_max(sl))

    def softmax(hd):
        m = jnp.max(m_ref[hd % 2], axis=-1, keepdims=True)
        p = jnp.exp2(s_ref[hd % 2] - m)
        denom[hd] = jnp.sum(p, axis=-1, keepdims=True)
        p_ref[hd % 2] = p.astype(BF16)

    def values(hd):
        own = _half_mask(hd % 2)
        o = (_dot(p_ref[hd % 2, :, 0:nc], _keep(own, vc_ref[hd // 2]))
             + _dot(p_ref[hd % 2, :, nc:], _keep(own, vl_ref[hd // 2]))) / denom.pop(hd)
        if hd % 2 == 0:
            outs[hd // 2] = o
        else:
            o_ref[:, (hd // 2) * LANES:(hd // 2 + 1) * LANES] = (outs.pop(hd // 2) + o).astype(BF16)

    for t in range(MLA_HEADS + 2):
        if t < MLA_HEADS:
            scores(t)
        if 0 <= t - 1 < MLA_HEADS:
            softmax(t - 1)
        if 0 <= t - 2 < MLA_HEADS:
            values(t - 2)


def _lat_mla(q, kc, vc, kl, vl, l, n_batch):
    nq = DEC_SEQ // TM
    n_keys = PAST_LEN + DEC_SEQ
    return pl.pallas_call(
        _lat_mla_kernel,
        grid=(n_batch, nq),
        in_specs=[pl.BlockSpec((MLA_HEADS, TM, LANES), lambda b, i: (0, b * nq + i, 0)),
                  pl.BlockSpec((None, MLA_HEADS, PAST_LEN, LANES), lambda b, i: (l, 0, b, 0)),
                  pl.BlockSpec((MLA_HEADS, DEC_SEQ, LANES), lambda b, i: (0, b, 0)),
                  pl.BlockSpec((None, MLA_HEADS // 2, PAST_LEN, LANES), lambda b, i: (l, 0, b, 0)),
                  pl.BlockSpec((MLA_HEADS // 2, DEC_SEQ, LANES), lambda b, i: (0, b, 0))],
        out_specs=pl.BlockSpec((TM, MLA_WIDTH), lambda b, i: (b * nq + i, 0)),
        out_shape=jax.ShapeDtypeStruct((n_batch * DEC_SEQ, MLA_WIDTH), BF16),
        scratch_shapes=[pltpu.VMEM((2, TM, n_keys), F32), pltpu.VMEM((2, TM, LANES), F32),
                        pltpu.VMEM((2, TM, n_keys), BF16)],
        compiler_params=_cparams(2),
        name="lat_mla",
    )(q, kc, kl, vc, vl)


def _na_bias_kernel(rpb_ref, o_ref, tab_ref):
    l = pl.program_id(0)
    hd = pl.program_id(1)
    n_dr = 2 * NA_WIN_R - 1
    n_dc = 2 * NA_WIN_C - 1
    base = (l * NA_HEADS + hd) * n_dr * n_dc
    qc = lax.broadcasted_iota(jnp.int32, (GRID_W, LANES), 0)
    kc = lax.broadcasted_iota(jnp.int32, (GRID_W, LANES), 1) % GRID_W
    diff = kc - qc
    cs = jnp.clip(qc - NA_WIN_C // 2, 0, GRID_W - NA_WIN_C)
    in_win = (kc >= cs) & (kc < cs + NA_WIN_C)
    neg = jnp.full((GRID_W, LANES), NEG_INF * LOG2E, F32)
    for dr in range(n_dr):
        acc = neg
        for dc in range(n_dc):
            acc = jnp.where(diff == dc - (NA_WIN_C - 1), rpb_ref[base + dr * n_dc + dc] * LOG2E, acc)
        tab_ref[dr] = jnp.where(in_win, acc, neg)
    left = lax.broadcasted_iota(jnp.int32, (GRID_W, LANES), 1) < GRID_W
    n_krows = NA_KBLKS * NA_QBLK // GRID_W
    cases = ((0, lambda qr: 0), (-NA_QROWS, lambda qr: qr), (-2 * NA_QROWS, lambda qr: NA_QROWS))
    for c, (shift, first) in enumerate(cases):
        for qr in range(NA_QROWS):
            for kp in range(n_krows // 2):
                tiles = []
                for kr in (2 * kp, 2 * kp + 1):
                    ok = first(qr) <= kr < first(qr) + NA_WIN_R
                    tiles.append(tab_ref[kr - qr + shift + NA_WIN_R - 1] if ok else neg)
                o_ref[c, qr * GRID_W:(qr + 1) * GRID_W, kp * LANES:(kp + 1) * LANES] = (
                    jnp.where(left, tiles[0], tiles[1]))


def _na_bias(na_rpb):
    n_k = NA_KBLKS * NA_QBLK
    return pl.pallas_call(
        _na_bias_kernel,
        grid=(DEPTH, NA_HEADS),
        in_specs=[pl.BlockSpec(memory_space=pltpu.SMEM)],
        out_specs=pl.BlockSpec((None, 3, None, NA_QBLK, n_k), lambda l, h: (l, 0, h, 0, 0)),
        out_shape=jax.ShapeDtypeStruct((DEPTH, 3, NA_HEADS, NA_QBLK, n_k), F32),
        scratch_shapes=[pltpu.VMEM((2 * NA_WIN_R - 1, GRID_W, LANES), F32)],
        compiler_params=_cparams(2),
        name="na_bias",
    )(na_rpb.reshape(-1))


def _lane_block_max(s):
    return functools.reduce(jnp.maximum, [s[:, j * LANES:(j + 1) * LANES] for j in range(s.shape[1] // LANES)])


def _na_kernel(q_ref, k0_ref, k1_ref, k2_ref, v0_ref, v1_ref, v2_ref, kc_ref, vc_ref, bias_ref, o_ref,
               s_ref, m_ref, p_ref):
    k_refs = (k0_ref, k1_ref, k2_ref)
    v_refs = (v0_ref, v1_ref, v2_ref)
    denom = {}
    outs = {}
    n_loc = NA_KBLKS * NA_QBLK

    def scores(hd):
        p = hd // 2
        q = _keep(_half_mask(hd % 2), q_ref[p])
        m = None
        for i in range(NA_KBLKS):
            s = _dot_nt(q, k_refs[i][p]) + bias_ref[hd, :, i * NA_QBLK:(i + 1) * NA_QBLK]
            s_ref[hd % 2, :, i * NA_QBLK:(i + 1) * NA_QBLK] = s
            m = _lane_block_max(s) if m is None else jnp.maximum(m, _lane_block_max(s))
        s = _dot_nt(q, kc_ref[:, p * LANES:(p + 1) * LANES].astype(BF16))
        s_ref[hd % 2, :, n_loc:] = s
        m_ref[hd % 2] = jnp.maximum(m, _lane_block_max(s))

    def softmax(hd):
        m = jnp.max(m_ref[hd % 2], axis=-1, keepdims=True)
        pr = jnp.exp2(s_ref[hd % 2] - m)
        denom[hd] = jnp.sum(pr, axis=-1, keepdims=True)
        p_ref[hd % 2] = pr.astype(BF16)

    def values(hd):
        p = hd // 2
        own = _half_mask(hd % 2)
        acc = _dot(p_ref[hd % 2, :, n_loc:], _keep(own, vc_ref[:, p * LANES:(p + 1) * LANES].astype(BF16)))
        for i in range(NA_KBLKS):
            acc = acc + _dot(p_ref[hd % 2, :, i * NA_QBLK:(i + 1) * NA_QBLK], _keep(own, v_refs[i][p]))
        o = acc / denom.pop(hd)
        if hd % 2 == 0:
            outs[p] = o
        else:
            o_ref[:, p * LANES:(p + 1) * LANES] = (outs.pop(p) + o).astype(BF16)

    for t in range(NA_HEADS + 2):
        if t < NA_HEADS:
            scores(t)
        if 0 <= t - 1 < NA_HEADS:
            softmax(t - 1)
        if 0 <= t - 2 < NA_HEADS:
            values(t - 2)


def _na_attention(qa, ka, va, cache_k, cache_v, bias, l, n_batch):
    nblk = DEC_SEQ // NA_QBLK
    max_start = nblk - NA_KBLKS

    def kspec(i):
        return pl.BlockSpec((NA_HEADS // 2, NA_QBLK, LANES),
                            lambda j, b: (0, b * nblk + jnp.clip(j - 1, 0, max_start) + i, 0))

    case = lambda j: jnp.where(j == 0, 0, jnp.where(j == nblk - 1, 2, 1))
    cache_spec = pl.BlockSpec((None, None, PAST_LEN, NA_WIDTH), lambda j, b: (b, l, 0, 0))
    return pl.pallas_call(
        _na_kernel,
        grid=(nblk, n_batch),
        in_specs=[pl.BlockSpec((NA_HEADS // 2, NA_QBLK, LANES), lambda j, b: (0, b * nblk + j, 0)),
                  kspec(0), kspec(1), kspec(2), kspec(0), kspec(1), kspec(2),
                  cache_spec, cache_spec,
                  pl.BlockSpec((None, None, NA_HEADS, NA_QBLK, NA_KBLKS * NA_QBLK),
                               lambda j, b: (l, case(j), 0, 0, 0))],
        out_specs=pl.BlockSpec((NA_QBLK, NA_WIDTH), lambda j, b: (b * nblk + j, 0)),
        out_shape=jax.ShapeDtypeStruct((n_batch * DEC_SEQ, NA_WIDTH), BF16),
        scratch_shapes=[pltpu.VMEM((2, NA_QBLK, (NA_KBLKS + 1) * NA_QBLK), F32),
                        pltpu.VMEM((2, NA_QBLK, LANES), F32),
                        pltpu.VMEM((2, NA_QBLK, (NA_KBLKS + 1) * NA_QBLK), BF16)],
        compiler_params=_cparams(2),
        name="na_attention",
    )(qa, ka, ka, ka, va, va, va, cache_k, cache_v, bias)


def _ffn_kernel(seq_len, final, widths, *refs):
    halo = seq_len > TM
    n_in = 3 if halo else 1
    x_refs = refs[:n_in]
    refs = refs[n_in:]
    part_refs = [refs[i * n_in:(i + 1) * n_in] for i in range(len(widths))]
    refs = refs[n_in * len(widths):]
    mod_ref, gffn_ref, wo_ref, win_ref, cw_ref, cb_ref, wout_ref = refs[:7]
    refs = refs[7:]
    if final:
        gfin_ref = refs[0]
        refs = refs[1:]
    o_ref, mix_ref, hext_ref, a_ref, act_ref = refs
    tm = o_ref.shape[0]
    pad = BF16_ROWS
    if halo:
        tiles_per_seq = seq_len // tm
        pos = pl.program_id(0) % tiles_per_seq
        pieces = [(0, 0, slice(0, pad), pos != 0), (pad, 1, slice(0, tm), None),
                  (pad + tm, 2, slice(0, pad), pos != tiles_per_seq - 1)]
        frames = []
        starts = [pad]
        seg_len = tm
    else:
        nseg = tm // seq_len
        starts = [pad + s * (seq_len + pad) for s in range(nseg)]
        seg_len = seq_len
        pieces = [(starts[s], 0, slice(s * seq_len, (s + 1) * seq_len), None) for s in range(nseg)]
        frames = [s * (seq_len + pad) for s in range(nseg + 1)]
    for f in frames:
        mix_ref[f:f + pad] = jnp.zeros((pad, D_MODEL), BF16)
        hext_ref[f:f + pad] = jnp.zeros((pad, D_MODEL), BF16)
    off = 0
    for n, prefs in zip(widths, part_refs):
        for row, src, rows, _ in pieces:
            mix_ref[row:row + rows.stop - rows.start, off:off + n] = prefs[src][rows, :]
        off += n
    r = _dot(mix_ref[...], wo_ref[...])
    g1 = mod_ref[2:3, :]
    scale = gffn_ref[...] * (1.0 + mod_ref[4:5, :])
    shift = mod_ref[3:4, :]
    for row, src, rows, keep in pieces:
        n_rows = rows.stop - rows.start
        x1 = x_refs[src][rows, :] + g1 * r[row:row + n_rows]
        if src == n_in // 2:
            o_ref[rows, :] = x1
        h2 = (x1 * lax.rsqrt(jnp.mean(x1 * x1, axis=-1, keepdims=True) + EPS) * scale + shift).astype(BF16)
        if keep is not None:
            h2 = jnp.where(keep, h2, jnp.zeros_like(h2))
        hext_ref[row:row + n_rows] = h2

    def col(c, lb):
        half = FF_LB // 2
        return (lb // half) * D_FF + c * FF_CHUNK + (lb % half) * LANES

    def up(c):
        hext = hext_ref[...]
        for part in range(2):
            a = _dot(hext, win_ref[:, part * D_FF + c * FF_CHUNK:part * D_FF + (c + 1) * FF_CHUNK])
            for j in range(FF_LB // 2):
                a_ref[c % 2, part * (FF_LB // 2) + j] = a[:, j * LANES:(j + 1) * LANES]

    def conv(c, lb, st):
        cs = slice(col(c, lb), col(c, lb) + LANES)
        return (a_ref[c % 2, lb, pl.ds(st - 1, seg_len, stride=1), :] * cw_ref[0:1, cs]
                + a_ref[c % 2, lb, st:st + seg_len, :] * cw_ref[1:2, cs]
                + a_ref[c % 2, lb, pl.ds(st + 1, seg_len, stride=1), :] * cw_ref[2:3, cs] + cb_ref[:, cs])

    up(0)
    for c in range(N_FF_CHUNKS):
        if c + 1 < N_FF_CHUNKS:
            up(c + 1)
        for s, st in enumerate(starts):
            for lb in range(FF_LB // 2):
                gate = conv(c, lb, st)
                val = conv(c, lb + FF_LB // 2, st)
                act_ref[s * seg_len:(s + 1) * seg_len, c * FF_CHUNK + lb * LANES:c * FF_CHUNK + (lb + 1) * LANES] = (
                    gate * jax.nn.sigmoid(gate) * val).astype(BF16)
    y = o_ref[...] + mod_ref[5:6, :] * _dot(act_ref[...], wout_ref[...])
    if final:
        y = _rms(y, gfin_ref[...])
    o_ref[...] = y


def _ffn(x, parts, mods, mod_row, pw, l, seq_len, g_final, name):
    t = x.shape[0]
    nt = t // TM
    hb = TM // BF16_ROWS
    n_hblk = t // BF16_ROWS
    final = g_final is not None
    halo = seq_len > TM
    ext = TM + 2 * BF16_ROWS if halo else TM + (TM // seq_len + 1) * BF16_ROWS
    widths = tuple(p.shape[1] for p in parts)

    def tiled(a):
        w = a.shape[1]
        main = pl.BlockSpec((TM, w), lambda i: (i, 0))
        if not halo:
            return [main], [a]
        prev = pl.BlockSpec((BF16_ROWS, w), lambda i: (jnp.maximum(i * hb - 1, 0), 0))
        nxt = pl.BlockSpec((BF16_ROWS, w), lambda i: (jnp.minimum((i + 1) * hb, n_hblk - 1), 0))
        return [prev, main, nxt], [a, a, a]

    in_specs, args = [], []
    for a in (x,) + tuple(parts):
        sp, ar = tiled(a)
        in_specs += sp
        args += ar
    in_specs += [pl.BlockSpec((None, 6, D_MODEL), lambda i: (mod_row(i), 0, 0)),
                 _layer_spec(l, 1, D_MODEL),
                 _layer_spec(l, D_MODEL, D_MODEL, single=True),
                 _layer_spec(l, D_MODEL, 2 * D_FF, single=True),
                 _layer_spec(l, 3, 2 * D_FF, single=True),
                 _layer_spec(l, 1, 2 * D_FF, single=True),
                 _layer_spec(l, D_FF, D_MODEL, single=True)]
    args += [mods, pw["g_ffn"], pw["w_out"], pw["w_ffn_in"], pw["conv_w"], pw["conv_b"], pw["w_ffn_out"]]
    if final:
        in_specs.append(pl.BlockSpec((1, D_MODEL), lambda i: (0, 0)))
        args.append(g_final)
    return pl.pallas_call(
        functools.partial(_ffn_kernel, seq_len, final, widths),
        grid=(nt,),
        in_specs=in_specs,
        out_specs=pl.BlockSpec((TM, D_MODEL), lambda i: (i, 0)),
        out_shape=jax.ShapeDtypeStruct((t, D_MODEL), F32),
        scratch_shapes=[pltpu.VMEM((ext, D_MODEL), BF16), pltpu.VMEM((ext, D_MODEL), BF16),
                        pltpu.VMEM((2, FF_LB, ext, LANES), F32), pltpu.VMEM((TM, D_FF), BF16)],
        compiler_params=_cparams(1),
        name=name,
    )(*args)


def _pack_weights(w_in, w_uq, w_ukv, w_sgu, b_sgu, w_out, w_ffn_in, ffn_conv_w, ffn_conv_b, w_ffn_out,
                  g_mix, g_cq, g_ckv, g_sgu, g_ffn):
    nl = w_in.shape[0]
    z = lambda n: jnp.zeros((nl, D_MODEL, n), BF16)
    w_in = w_in.astype(BF16)
    w_in_p = jnp.concatenate([w_in[..., :OFF_KR], z(HALF), w_in[..., OFF_KR:OFF_KR + MLA_ROPE],
                              z(LANES - HALF - MLA_ROPE), w_in[..., OFF_KR + MLA_ROPE:]], axis=-1)
    dq = MLA_NOPE + MLA_ROPE
    w_uq_p = jnp.pad(w_uq.astype(BF16).reshape(nl, MLA_Q_RANK, MLA_HEADS, dq),
                     ((0, 0), (0, 0), (0, 0), (0, LANES - dq))).reshape(nl, MLA_Q_RANK, MLA_HEADS * LANES)
    kvh = w_ukv.astype(BF16).reshape(nl, MLA_KV_RANK, MLA_HEADS, MLA_NOPE + MLA_V)
    k_part = jnp.pad(kvh[..., :MLA_NOPE], ((0, 0), (0, 0), (0, 0), (0, LANES - MLA_NOPE)))
    w_ukv_p = jnp.concatenate([k_part.reshape(nl, MLA_KV_RANK, MLA_HEADS * LANES),
                               kvh[..., MLA_NOPE:].reshape(nl, MLA_KV_RANK, MLA_HEADS * MLA_V)], axis=-1)
    b_sgu_p = jnp.repeat(jnp.swapaxes(b_sgu, 1, 2), SGU_WIDTH // SGU_GROUPS, axis=-1)
    return dict(
        w_in=w_in_p, w_uq=w_uq_p, w_ukv=w_ukv_p, w_sgu=w_sgu.astype(BF16), b_sgu=b_sgu_p,
        w_out=w_out.astype(BF16), w_ffn_in=w_ffn_in.astype(BF16), conv_w=ffn_conv_w,
        conv_b=ffn_conv_b[:, None, :], w_ffn_out=w_ffn_out.astype(BF16),
        g_mix=g_mix[:, None, :], g_cq=g_cq[:, None, :], g_ckv=g_ckv[:, None, :], g_sgu=g_sgu[:, None, :],
        g_ffn=g_ffn[:, None, :])


def _rope_tables(n_tokens):
    t = jnp.arange(n_tokens)
    n_freq = MLA_ROPE // 4
    inv_freq = ROPE_THETA ** (-jnp.arange(n_freq, dtype=F32) / n_freq)
    ang_r = (t // GRID_W).astype(F32)[:, None] * inv_freq
    ang_c = (t % GRID_W).astype(F32)[:, None] * inv_freq
    ones = jnp.ones((n_tokens, HALF), F32)
    tail = LANES - HALF - MLA_ROPE
    cos = jnp.concatenate([ones, jnp.cos(ang_r), jnp.cos(ang_r), jnp.cos(ang_c), jnp.cos(ang_c),
                           ones[:, :tail]], axis=-1)
    sin = jnp.concatenate([0 * ones, -jnp.sin(ang_r), jnp.sin(ang_r), -jnp.sin(ang_c), jnp.sin(ang_c),
                           0 * ones[:, :tail]], axis=-1)
    return cos, sin


def kernel(x_prompt, x_sample, cache_na_k, cache_na_v, cache_mla_ckv, cache_mla_krope, c, c_ctx, w_mod, b_mod,
           g_mix, w_in, na_rpb, g_cq, w_uq, g_ckv, w_ukv, g_sgu, w_sgu, b_sgu, w_out, g_ffn, w_ffn_in,
           ffn_conv_w, ffn_conv_b, w_ffn_out, g_final):
    n_ctx, n_lat = x_prompt.shape[0], x_sample.shape[0]
    t_ctx, t_lat = n_ctx * SEQ, n_lat * DEC_SEQ
    pw = _pack_weights(w_in, w_uq, w_ukv, w_sgu, b_sgu, w_out, w_ffn_in, ffn_conv_w, ffn_conv_b,
                       w_ffn_out, g_mix, g_cq, g_ckv, g_sgu, g_ffn)
    g_fin = g_final[None, :]

    cond = jnp.concatenate([c_ctx[None, :], c, jnp.zeros((8 - 1 - n_lat, D_MODEL), F32)], axis=0)
    mods = _modulation(cond, w_mod, b_mod)
    rope_tabs = _rope_tables(DEC_SEQ)
    na_bias = _na_bias(na_rpb)
    kr_pad = jnp.pad(cache_mla_krope, ((0, 0), (0, 0), (0, 0), (HALF, LANES - HALF - MLA_ROPE)))
    cache_kp, cache_vp = _cache_kv(cache_mla_ckv, kr_pad, pw["w_ukv"])
    cache_k = cache_na_k.reshape(n_lat, DEPTH, PAST_LEN, NA_WIDTH)
    cache_v = cache_na_v.reshape(n_lat, DEPTH, PAST_LEN, NA_WIDTH)

    lat_tiles = DEC_SEQ // TM
    ctx_row = lambda i: 0
    lat_row = lambda i: 1 + i // lat_tiles

    xp = x_prompt.reshape(t_ctx, D_MODEL)
    xs = x_sample.reshape(t_lat, D_MODEL)
    new_k, new_v, new_ckv, new_kr = [], [], [], []
    for l in range(DEPTH):
        last = l == DEPTH - 1
        m = mods[l]
        qa, ka, va, qm, kp, vp, oc, ka_f, va_f, ckv_f, kr_f = _project(xp, m, ctx_row, pw, l, None, True, 1)
        new_k.append(ka_f)
        new_v.append(va_f)
        new_ckv.append(ckv_f)
        new_kr.append(kr_f[:, HALF:HALF + MLA_ROPE])
        o_ab = _ctx_attention(qa, ka, va, qm, kp, vp, n_ctx)
        xp = _ffn(xp, (o_ab, oc), m, ctx_row, pw, l, SEQ, g_fin if last else None, "ctx_ffn")

        qa, ka, va, qm, kp, vp, oc = _project(xs, m, lat_row, pw, l, rope_tabs, False, lat_tiles)
        o_a = _na_attention(qa, ka, va, cache_k, cache_v, na_bias, l, n_lat)
        o_b = _lat_mla(qm, cache_kp, cache_vp, kp, vp, l, n_lat)
        xs = _ffn(xs, (o_a, o_b, oc), m, lat_row, pw, l, DEC_SEQ, g_fin if last else None, "lat_ffn")

    def stacked(parts, tail):
        return jnp.stack([a.reshape((n_ctx, SEQ) + tail) for a in parts], axis=1)

    return (xp.reshape(n_ctx, SEQ, D_MODEL), xs.reshape(n_lat, DEC_SEQ, D_MODEL),
            stacked(new_k, (NA_HEADS, HEAD_DIM)), stacked(new_v, (NA_HEADS, HEAD_DIM)),
            stacked(new_ckv, (MLA_KV_RANK,)), stacked(new_kr, (MLA_ROPE,)))
```

```python
import functools
import math

import jax
import jax.numpy as jnp
from jax import lax
from jax.experimental import pallas as pl
from jax.experimental.pallas import tpu as pltpu

F32 = jnp.float32
BF16 = jnp.bfloat16

D_MODEL = 1024
DEPTH = 4
SEQ = 256
DEC_SEQ = 2048
PAST_LEN = 256
GRID_W = 64
HEAD_DIM = 64
NA_WIDTH = 256
NA_HEADS = 4
NA_WIN_R = 8
NA_WIN_C = 16
MLA_HEADS = 8
MLA_NOPE = 64
MLA_ROPE = 32
MLA_V = 64
MLA_WIDTH = MLA_HEADS * MLA_V
MLA_Q_RANK = 384
MLA_KV_RANK = 256
SGU_WIDTH = 256
SGU_GROUPS = 4
SGU_CHUNK = 128
D_FF = 2816
ROPE_THETA = 10000.0
EPS = 1e-6
NEG_INF = -1e30
LOG2E = math.log2(math.e)
NA_QSCALE = HEAD_DIM ** -0.5 * LOG2E
MLA_QSCALE = (MLA_NOPE + MLA_ROPE) ** -0.5 * LOG2E

LANES = 128
BF16_ROWS = 16
HALF = LANES // 2

OFF_QA, OFF_KA, OFF_VA = 0, 256, 512
OFF_CQ = 768
OFF_CKV = OFF_CQ + MLA_Q_RANK
OFF_KR = OFF_CKV + MLA_KV_RANK
OFF_UV = OFF_KR + LANES
IN_COLS = OFF_UV + 2 * SGU_WIDTH
KV_COLS = MLA_HEADS * LANES + MLA_HEADS * MLA_V

FF_CHUNK = 256
N_FF_CHUNKS = D_FF // FF_CHUNK
FF_LB = 2 * FF_CHUNK // LANES

TM = 512
NA_QROWS = 4
NA_QBLK = NA_QROWS * GRID_W
NA_KBLKS = 3
VMEM_LIMIT = 56 * 1024 * 1024


def _cparams(n_axes):
    return pltpu.CompilerParams(dimension_semantics=("arbitrary",) * n_axes,
                                vmem_limit_bytes=VMEM_LIMIT)


def _layer_spec(l, *shape, single=False):
    mode = dict(pipeline_mode=pl.Buffered(1)) if single else {}
    return pl.BlockSpec((None,) + shape, lambda *_: (l,) + (0,) * len(shape), **mode)


def _rms(x, g):
    ms = jnp.mean(x * x, axis=-1, keepdims=True)
    return x * lax.rsqrt(ms + EPS) * g


def _dot(a, b):
    return jnp.dot(a, b, preferred_element_type=F32)


def _dot_nt(a, b):
    return lax.dot_general(a, b, (((1,), (1,)), ((), ())), preferred_element_type=F32)


def _half_mask(parity):
    lane = lax.broadcasted_iota(jnp.int32, (1, LANES), 1)
    return (lane // HALF) == parity


def _keep(mask, x):
    return jnp.where(mask, x, jnp.zeros_like(x))


def _with_ones(own, parity, v):
    lane = lax.broadcasted_iota(jnp.int32, (1, LANES), 1)
    ones = (lane == HALF * (1 - parity)).astype(v.dtype)
    return jnp.where(own, v, ones)


def _normalised(own, parity, parts):
    o = functools.reduce(jnp.add, parts)
    spare = HALF * (1 - parity)
    return jnp.where(own, o, 0.0) / o[:, spare:spare + 1]


def _mod_kernel(c_ref, w_ref, b_ref, o_ref):
    c = c_ref[...]
    s = c * jax.nn.sigmoid(c)
    o_ref[...] = _dot(s.astype(BF16), w_ref[...].astype(BF16)) + b_ref[...]


def _modulation(cond, w_mod, b_mod):
    n = 6
    wide = 2 * D_MODEL
    out = pl.pallas_call(
        _mod_kernel,
        grid=(DEPTH, n * D_MODEL // wide),
        in_specs=[
            pl.BlockSpec((8, D_MODEL), lambda l, j: (0, 0)),
            pl.BlockSpec((None, D_MODEL, wide), lambda l, j: (l, 0, j)),
            pl.BlockSpec((None, 1, wide), lambda l, j: (l, 0, j)),
        ],
        out_specs=pl.BlockSpec((None, 8, wide), lambda l, j: (l, 0, j)),
        out_shape=jax.ShapeDtypeStruct((DEPTH, 8, n * D_MODEL), F32),
        compiler_params=_cparams(2),
        name="modulation",
    )(cond, w_mod, b_mod.reshape(DEPTH, 1, n * D_MODEL))
    return out.reshape(DEPTH, 8, n, D_MODEL)


def _rope(x, cos, sin, lane_lo):
    up = pltpu.roll(x, LANES - 8, axis=1)
    dn = pltpu.roll(x, 8, axis=1)
    return x * cos + jnp.where(lane_lo, up, dn) * sin


def _proj_kernel(rope, emit_f32, x_ref, mod_ref, gmix_ref, win_ref, gcq_ref, wuq_ref, gckv_ref,
                 wukv_ref, gsgu_ref, wsgu_ref, bsg_ref, *refs):
    if rope:
        cos_ref, sin_ref = refs[:2]
        refs = refs[2:]
    qa_ref, ka_ref, va_ref, qm_ref, kp_ref, vp_ref, oc_ref = refs[:7]
    refs = refs[7:]
    if emit_f32:
        kaf_ref, vaf_ref, ckvf_ref, krf_ref = refs[:4]
        refs = refs[4:]
    hb_ref, z_ref = refs

    x = x_ref[...]
    h = _rms(x, gmix_ref[...]) * (1.0 + mod_ref[1:2, :]) + mod_ref[0:1, :]
    hb_ref[...] = h.astype(BF16)
    tm = x.shape[0]
    z_ref[:, OFF_UV:] = _dot(hb_ref[...], win_ref[:, OFF_UV:])
    z_ref[:, OFF_CQ:OFF_UV] = _dot(hb_ref[...], win_ref[:, OFF_CQ:OFF_UV])
    z_ref[:, :OFF_CQ] = _dot(hb_ref[...], win_ref[:, :OFF_CQ])
    lane = lax.broadcasted_iota(jnp.int32, (1, LANES), 1)
    lane_lo = (lane % 16) < 8
    if rope:
        cos = cos_ref[...]
        sin = sin_ref[...]

    uv = jax.nn.gelu(z_ref[:, OFF_UV:OFF_UV + 2 * SGU_WIDTH])
    u = uv[:, :SGU_WIDTH]
    vn = _rms(uv[:, SGU_WIDTH:], gsgu_ref[...])
    even = _half_mask(0)
    for ch in range(tm // SGU_CHUNK):
        rows = slice(ch * SGU_CHUNK, (ch + 1) * SGU_CHUNK)
        for p in range(SGU_GROUPS // 2):
            sl = slice(p * LANES, (p + 1) * LANES)
            vc = vn[rows, sl]
            mixed = (_dot(wsgu_ref[2 * p], jnp.where(even, vc, 0.0).astype(BF16))
                     + _dot(wsgu_ref[2 * p + 1], jnp.where(even, 0.0, vc).astype(BF16))
                     + bsg_ref[:, sl])
            oc_ref[rows, sl] = (u[rows, sl] * mixed).astype(BF16)

    cq = z_ref[:, OFF_CQ:OFF_CQ + MLA_Q_RANK]
    cqn = _rms(cq, gcq_ref[...]).astype(BF16)
    qm = _dot(cqn, wuq_ref[...])
    for hd in range(MLA_HEADS):
        qh = qm[:, hd * LANES:(hd + 1) * LANES]
        if rope:
            qh = _rope(qh, cos, sin, lane_lo)
        qm_ref[hd] = (qh * MLA_QSCALE).astype(BF16)

    ckv = z_ref[:, OFF_CKV:OFF_CKV + MLA_KV_RANK]
    ckvn = _rms(ckv, gckv_ref[...])
    kr = z_ref[:, OFF_KR:OFF_KR + LANES]
    if emit_f32:
        ckvf_ref[...] = ckvn
        krf_ref[...] = kr
    if rope:
        kr = _rope(kr, cos, sin, lane_lo)
    kv = _dot(ckvn.astype(BF16), wukv_ref[...])
    for hd in range(MLA_HEADS):
        kp_ref[hd] = (kv[:, hd * LANES:(hd + 1) * LANES] + kr).astype(BF16)
    voff = MLA_HEADS * LANES
    for p in range(MLA_HEADS // 2):
        vp_ref[p] = kv[:, voff + p * LANES:voff + (p + 1) * LANES].astype(BF16)

    qa = z_ref[:, OFF_QA:OFF_QA + NA_WIDTH] * NA_QSCALE
    ka = z_ref[:, OFF_KA:OFF_KA + NA_WIDTH]
    va = z_ref[:, OFF_VA:OFF_VA + NA_WIDTH]
    for p in range(NA_HEADS // 2):
        sl = slice(p * LANES, (p + 1) * LANES)
        qa_ref[p] = qa[:, sl].astype(BF16)
        ka_ref[p] = ka[:, sl].astype(BF16)
        va_ref[p] = va[:, sl].astype(BF16)
    if emit_f32:
        kaf_ref[...] = ka
        vaf_ref[...] = va


def _project(x, mods, mod_row, pw, l, rope_tabs, emit_f32, tiles_per_seq):
    t = x.shape[0]
    nt = t // TM
    rope = rope_tabs is not None
    in_specs = [
        pl.BlockSpec((TM, D_MODEL), lambda i: (i, 0)),
        pl.BlockSpec((None, 6, D_MODEL), lambda i: (mod_row(i), 0, 0)),
        _layer_spec(l, 1, D_MODEL),
        _layer_spec(l, D_MODEL, IN_COLS),
        _layer_spec(l, 1, MLA_Q_RANK),
        _layer_spec(l, MLA_Q_RANK, MLA_HEADS * LANES),
        _layer_spec(l, 1, MLA_KV_RANK),
        _layer_spec(l, MLA_KV_RANK, KV_COLS),
        _layer_spec(l, 1, SGU_WIDTH),
        _layer_spec(l, SGU_GROUPS, SGU_CHUNK, SGU_CHUNK),
        _layer_spec(l, SGU_CHUNK, SGU_WIDTH),
    ]
    args = [x, mods, pw["g_mix"], pw["w_in"], pw["g_cq"], pw["w_uq"], pw["g_ckv"], pw["w_ukv"],
            pw["g_sgu"], pw["w_sgu"], pw["b_sgu"]]
    if rope:
        tab_spec = pl.BlockSpec((TM, LANES), lambda i: (i % tiles_per_seq, 0))
        in_specs += [tab_spec, tab_spec]
        args += list(rope_tabs)

    def heads(n):
        return (pl.BlockSpec((n, TM, LANES), lambda i: (0, i, 0)),
                jax.ShapeDtypeStruct((n, t, LANES), BF16))

    def flat(w, dt):
        return (pl.BlockSpec((TM, w), lambda i: (i, 0)), jax.ShapeDtypeStruct((t, w), dt))

    outs = [heads(2), heads(2), heads(2), heads(MLA_HEADS), heads(MLA_HEADS), heads(MLA_HEADS // 2),
            flat(SGU_WIDTH, BF16)]
    if emit_f32:
        outs += [flat(NA_WIDTH, F32), flat(NA_WIDTH, F32), flat(MLA_KV_RANK, F32), flat(LANES, F32)]
    return pl.pallas_call(
        functools.partial(_proj_kernel, rope, emit_f32),
        grid=(nt,),
        in_specs=in_specs,
        out_specs=[o[0] for o in outs],
        out_shape=[o[1] for o in outs],
        scratch_shapes=[pltpu.VMEM((TM, D_MODEL), BF16), pltpu.VMEM((TM, IN_COLS), F32)],
        compiler_params=_cparams(1),
        name="project_lat" if rope else "project_ctx",
    )(*args)


def _cache_kv_kernel(ckv_ref, kr_ref, wukv_ref, kp_ref, vp_ref):
    kv = _dot(ckv_ref[...].astype(BF16), wukv_ref[...])
    kr = kr_ref[...]
    for hd in range(MLA_HEADS):
        kp_ref[hd] = (kv[:, hd * LANES:(hd + 1) * LANES] + kr).astype(BF16)
    voff = MLA_HEADS * LANES
    for p in range(MLA_HEADS // 2):
        vp_ref[p] = kv[:, voff + p * LANES:voff + (p + 1) * LANES].astype(BF16)


def _cache_kv(cache_ckv, cache_kr_pad, w_ukv):
    b = cache_ckv.shape[0]
    return pl.pallas_call(
        _cache_kv_kernel,
        grid=(DEPTH, b),
        in_specs=[
            pl.BlockSpec((None, None, PAST_LEN, MLA_KV_RANK), lambda l, i: (i, l, 0, 0)),
            pl.BlockSpec((None, None, PAST_LEN, LANES), lambda l, i: (i, l, 0, 0)),
            pl.BlockSpec((None, MLA_KV_RANK, KV_COLS), lambda l, i: (l, 0, 0)),
        ],
        out_specs=[
            pl.BlockSpec((None, MLA_HEADS, PAST_LEN, LANES), lambda l, i: (l, 0, i, 0)),
            pl.BlockSpec((None, MLA_HEADS // 2, PAST_LEN, LANES), lambda l, i: (l, 0, i, 0)),
        ],
        out_shape=[
            jax.ShapeDtypeStruct((DEPTH, MLA_HEADS, b * PAST_LEN, LANES), BF16),
            jax.ShapeDtypeStruct((DEPTH, MLA_HEADS // 2, b * PAST_LEN, LANES), BF16),
        ],
        compiler_params=_cparams(2),
        name="cache_kv",
    )(cache_ckv, cache_kr_pad, w_ukv)


def _ctx_attn_kernel(qa_ref, ka_ref, va_ref, qm_ref, kp_ref, vp_ref, o_ref):
    for p in range(NA_HEADS // 2 + MLA_HEADS // 2):
        out = None
        for half in range(2):
            own = _half_mask(half)
            if p < NA_HEADS // 2:
                q = _keep(own, qa_ref[p])
                k = ka_ref[p]
                v = va_ref[p]
            else:
                hd = 2 * (p - NA_HEADS // 2) + half
                q = qm_ref[hd]
                k = kp_ref[hd]
                v = vp_ref[hd // 2]
            s = _dot_nt(q, k)
            p_ = jnp.exp2(s - jnp.max(s, axis=-1, keepdims=True)).astype(BF16)
            o = _normalised(own, half, [_dot(p_, _with_ones(own, half, v))])
            out = o if out is None else out + o
        o_ref[:, p * LANES:(p + 1) * LANES] = out.astype(BF16)


def _ctx_attention(qa, ka, va, qm, kp, vp, n_seq):
    heads = lambda n: pl.BlockSpec((n, SEQ, LANES), lambda b: (0, b, 0))
    width = NA_WIDTH + MLA_WIDTH
    return pl.pallas_call(
        _ctx_attn_kernel,
        grid=(n_seq,),
        in_specs=[heads(2), heads(2), heads(2), heads(MLA_HEADS), heads(MLA_HEADS), heads(MLA_HEADS // 2)],
        out_specs=pl.BlockSpec((SEQ, width), lambda b: (b, 0)),
        out_shape=jax.ShapeDtypeStruct((n_seq * SEQ, width), BF16),
        compiler_params=_cparams(1),
        name="ctx_attention",
    )(qa, ka, va, qm, kp, vp)


def _lat_mla_kernel(q_ref, kc_ref, kl_ref, vc_ref, vl_ref, o_ref, s_ref, m_ref, p_ref):
    nc = kc_ref.shape[1]
    outs = {}

    def scores(hd):
        q = q_ref[hd]
        sc = _dot_nt(q, kc_ref[hd])
        sl = _dot_nt(q, kl_ref[hd])
        s_ref[hd % 2, :, 0:nc] = sc
        s_ref[hd % 2, :, nc:] = sl
        m_ref[hd % 2] = jnp.maximum(_lane_block_max(sc), _lane_block_max(sl))

    def softmax(hd):
        m = jnp.max(m_ref[hd % 2], axis=-1, keepdims=True)
        p_ref[hd % 2] = jnp.exp2(s_ref[hd % 2] - m).astype(BF16)

    def values(hd):
        own = _half_mask(hd % 2)
        o = _normalised(own, hd % 2, [_dot(p_ref[hd % 2, :, 0:nc], _with_ones(own, hd % 2, vc_ref[hd // 2])),
                                      _dot(p_ref[hd % 2, :, nc:], _with_ones(own, hd % 2, vl_ref[hd // 2]))])
        if hd % 2 == 0:
            outs[hd // 2] = o
        else:
            o_ref[:, (hd // 2) * LANES:(hd // 2 + 1) * LANES] = (outs.pop(hd // 2) + o).astype(BF16)

    for t in range(MLA_HEADS + 2):
        if t < MLA_HEADS:
            scores(t)
        if 0 <= t - 1 < MLA_HEADS:
            softmax(t - 1)
        if 0 <= t - 2 < MLA_HEADS:
            values(t - 2)


def _lat_mla(q, kc, vc, kl, vl, l, n_batch):
    nq = DEC_SEQ // TM
    n_keys = PAST_LEN + DEC_SEQ
    return pl.pallas_call(
        _lat_mla_kernel,
        grid=(n_batch, nq),
        in_specs=[pl.BlockSpec((MLA_HEADS, TM, LANES), lambda b, i: (0, b * nq + i, 0)),
                  pl.BlockSpec((None, MLA_HEADS, PAST_LEN, LANES), lambda b, i: (l, 0, b, 0)),
                  pl.BlockSpec((MLA_HEADS, DEC_SEQ, LANES), lambda b, i: (0, b, 0)),
                  pl.BlockSpec((None, MLA_HEADS // 2, PAST_LEN, LANES), lambda b, i: (l, 0, b, 0)),
                  pl.BlockSpec((MLA_HEADS // 2, DEC_SEQ, LANES), lambda b, i: (0, b, 0))],
        out_specs=pl.BlockSpec((TM, MLA_WIDTH), lambda b, i: (b * nq + i, 0)),
        out_shape=jax.ShapeDtypeStruct((n_batch * DEC_SEQ, MLA_WIDTH), BF16),
        scratch_shapes=[pltpu.VMEM((2, TM, n_keys), F32), pltpu.VMEM((2, TM, LANES), F32),
                        pltpu.VMEM((2, TM, n_keys), BF16)],
        compiler_params=_cparams(2),
        name="lat_mla",
    )(q, kc, kl, vc, vl)


def _na_bias_kernel(rpb_ref, o_ref, tab_ref):
    l = pl.program_id(0)
    hd = pl.program_id(1)
    n_dr = 2 * NA_WIN_R - 1
    n_dc = 2 * NA_WIN_C - 1
    base = (l * NA_HEADS + hd) * n_dr * n_dc
    qc = lax.broadcasted_iota(jnp.int32, (GRID_W, LANES), 0)
    kc = lax.broadcasted_iota(jnp.int32, (GRID_W, LANES), 1) % GRID_W
    diff = kc - qc
    cs = jnp.clip(qc - NA_WIN_C // 2, 0, GRID_W - NA_WIN_C)
    in_win = (kc >= cs) & (kc < cs + NA_WIN_C)
    neg = jnp.full((GRID_W, LANES), NEG_INF * LOG2E, F32)
    for dr in range(n_dr):
        acc = neg
        for dc in range(n_dc):
            acc = jnp.where(diff == dc - (NA_WIN_C - 1), rpb_ref[base + dr * n_dc + dc] * LOG2E, acc)
        tab_ref[dr] = jnp.where(in_win, acc, neg)
    left = lax.broadcasted_iota(jnp.int32, (GRID_W, LANES), 1) < GRID_W
    n_krows = NA_KBLKS * NA_QBLK // GRID_W
    cases = ((0, lambda qr: 0), (-NA_QROWS, lambda qr: qr), (-2 * NA_QROWS, lambda qr: NA_QROWS))
    for c, (shift, first) in enumerate(cases):
        for qr in range(NA_QROWS):
            for kp in range(n_krows // 2):
                tiles = []
                for kr in (2 * kp, 2 * kp + 1):
                    ok = first(qr) <= kr < first(qr) + NA_WIN_R
                    tiles.append(tab_ref[kr - qr + shift + NA_WIN_R - 1] if ok else neg)
                o_ref[c, qr * GRID_W:(qr + 1) * GRID_W, kp * LANES:(kp + 1) * LANES] = (
                    jnp.where(left, tiles[0], tiles[1]))


def _na_bias(na_rpb):
    n_k = NA_KBLKS * NA_QBLK
    return pl.pallas_call(
        _na_bias_kernel,
        grid=(DEPTH, NA_HEADS),
        in_specs=[pl.BlockSpec(memory_space=pltpu.SMEM)],
        out_specs=pl.BlockSpec((None, 3, None, NA_QBLK, n_k), lambda l, h: (l, 0, h, 0, 0)),
        out_shape=jax.ShapeDtypeStruct((DEPTH, 3, NA_HEADS, NA_QBLK, n_k), F32),
        scratch_shapes=[pltpu.VMEM((2 * NA_WIN_R - 1, GRID_W, LANES), F32)],
        compiler_params=_cparams(2),
        name="na_bias",
    )(na_rpb.reshape(-1))


def _lane_block_max(s):
    return functools.reduce(jnp.maximum, [s[:, j * LANES:(j + 1) * LANES] for j in range(s.shape[1] // LANES)])


def _na_kernel(q_ref, k0_ref, k1_ref, k2_ref, v0_ref, v1_ref, v2_ref, kc_ref, vc_ref, bias_ref, o_ref,
               s_ref, m_ref, p_ref):
    k_refs = (k0_ref, k1_ref, k2_ref)
    v_refs = (v0_ref, v1_ref, v2_ref)
    outs = {}
    n_loc = NA_KBLKS * NA_QBLK

    def scores(hd):
        p = hd // 2
        q = _keep(_half_mask(hd % 2), q_ref[p])
        m = None
        for i in range(NA_KBLKS):
            s = _dot_nt(q, k_refs[i][p]) + bias_ref[hd, :, i * NA_QBLK:(i + 1) * NA_QBLK]
            s_ref[hd % 2, :, i * NA_QBLK:(i + 1) * NA_QBLK] = s
            m = _lane_block_max(s) if m is None else jnp.maximum(m, _lane_block_max(s))
        s = _dot_nt(q, kc_ref[:, p * LANES:(p + 1) * LANES].astype(BF16))
        s_ref[hd % 2, :, n_loc:] = s
        m_ref[hd % 2] = jnp.maximum(m, _lane_block_max(s))

    def softmax(hd):
        m = jnp.max(m_ref[hd % 2], axis=-1, keepdims=True)
        p_ref[hd % 2] = jnp.exp2(s_ref[hd % 2] - m).astype(BF16)

    def values(hd):
        p = hd // 2
        own = _half_mask(hd % 2)
        vc = vc_ref[:, p * LANES:(p + 1) * LANES].astype(BF16)
        parts = [_dot(p_ref[hd % 2, :, n_loc:], _with_ones(own, hd % 2, vc))]
        for i in range(NA_KBLKS):
            parts.append(_dot(p_ref[hd % 2, :, i * NA_QBLK:(i + 1) * NA_QBLK],
                              _with_ones(own, hd % 2, v_refs[i][p])))
        o = _normalised(own, hd % 2, parts)
        if hd % 2 == 0:
            outs[p] = o
        else:
            o_ref[:, p * LANES:(p + 1) * LANES] = (outs.pop(p) + o).astype(BF16)

    for t in range(NA_HEADS + 2):
        if t < NA_HEADS:
            scores(t)
        if 0 <= t - 1 < NA_HEADS:
            softmax(t - 1)
        if 0 <= t - 2 < NA_HEADS:
            values(t - 2)


def _na_attention(qa, ka, va, cache_k, cache_v, bias, l, n_batch):
    nblk = DEC_SEQ // NA_QBLK
    max_start = nblk - NA_KBLKS

    def kspec(i):
        return pl.BlockSpec((NA_HEADS // 2, NA_QBLK, LANES),
                            lambda j, b: (0, b * nblk + jnp.clip(j - 1, 0, max_start) + i, 0))

    case = lambda j: jnp.where(j == 0, 0, jnp.where(j == nblk - 1, 2, 1))
    cache_spec = pl.BlockSpec((None, None, PAST_LEN, NA_WIDTH), lambda j, b: (b, l, 0, 0))
    return pl.pallas_call(
        _na_kernel,
        grid=(nblk, n_batch),
        in_specs=[pl.BlockSpec((NA_HEADS // 2, NA_QBLK, LANES), lambda j, b: (0, b * nblk + j, 0)),
                  kspec(0), kspec(1), kspec(2), kspec(0), kspec(1), kspec(2),
                  cache_spec, cache_spec,
                  pl.BlockSpec((None, None, NA_HEADS, NA_QBLK, NA_KBLKS * NA_QBLK),
                               lambda j, b: (l, case(j), 0, 0, 0))],
        out_specs=pl.BlockSpec((NA_QBLK, NA_WIDTH), lambda j, b: (b * nblk + j, 0)),
        out_shape=jax.ShapeDtypeStruct((n_batch * DEC_SEQ, NA_WIDTH), BF16),
        scratch_shapes=[pltpu.VMEM((2, NA_QBLK, (NA_KBLKS + 1) * NA_QBLK), F32),
                        pltpu.VMEM((2, NA_QBLK, LANES), F32),
                        pltpu.VMEM((2, NA_QBLK, (NA_KBLKS + 1) * NA_QBLK), BF16)],
        compiler_params=_cparams(2),
        name="na_attention",
    )(qa, ka, ka, ka, va, va, va, cache_k, cache_v, bias)


def _ffn_kernel(seq_len, final, widths, *refs):
    halo = seq_len > TM
    n_in = 3 if halo else 1
    x_refs = refs[:n_in]
    refs = refs[n_in:]
    part_refs = [refs[i * n_in:(i + 1) * n_in] for i in range(len(widths))]
    refs = refs[n_in * len(widths):]
    mod_ref, gffn_ref, wo_ref, win_ref, cw_ref, cb_ref, wout_ref = refs[:7]
    refs = refs[7:]
    if final:
        gfin_ref = refs[0]
        refs = refs[1:]
    o_ref, mix_ref, hext_ref, a_ref, act_ref = refs
    tm = o_ref.shape[0]
    pad = BF16_ROWS
    if halo:
        tiles_per_seq = seq_len // tm
        pos = pl.program_id(0) % tiles_per_seq
        pieces = [(0, 0, slice(0, pad), pos != 0), (pad, 1, slice(0, tm), None),
                  (pad + tm, 2, slice(0, pad), pos != tiles_per_seq - 1)]
        frames = []
        starts = [pad]
        seg_len = tm
    else:
        nseg = tm // seq_len
        starts = [pad + s * (seq_len + pad) for s in range(nseg)]
        seg_len = seq_len
        pieces = [(starts[s], 0, slice(s * seq_len, (s + 1) * seq_len), None) for s in range(nseg)]
        frames = [s * (seq_len + pad) for s in range(nseg + 1)]
    for f in frames:
        mix_ref[f:f + pad] = jnp.zeros((pad, D_MODEL), BF16)
        hext_ref[f:f + pad] = jnp.zeros((pad, D_MODEL), BF16)
    off = 0
    for n, prefs in zip(widths, part_refs):
        for row, src, rows, _ in pieces:
            mix_ref[row:row + rows.stop - rows.start, off:off + n] = prefs[src][rows, :]
        off += n
    r = _dot(mix_ref[...], wo_ref[...])
    g1 = mod_ref[2:3, :]
    scale = gffn_ref[...] * (1.0 + mod_ref[4:5, :])
    shift = mod_ref[3:4, :]
    for row, src, rows, keep in pieces:
        n_rows = rows.stop - rows.start
        x1 = x_refs[src][rows, :] + g1 * r[row:row + n_rows]
        if src == n_in // 2:
            o_ref[rows, :] = x1
        h2 = (x1 * lax.rsqrt(jnp.mean(x1 * x1, axis=-1, keepdims=True) + EPS) * scale + shift).astype(BF16)
        if keep is not None:
            h2 = jnp.where(keep, h2, jnp.zeros_like(h2))
        hext_ref[row:row + n_rows] = h2

    def col(c, lb):
        half = FF_LB // 2
        return (lb // half) * D_FF + c * FF_CHUNK + (lb % half) * LANES

    def up(c):
        hext = hext_ref[...]
        for part in range(2):
            a = _dot(hext, win_ref[:, part * D_FF + c * FF_CHUNK:part * D_FF + (c + 1) * FF_CHUNK])
            for j in range(FF_LB // 2):
                a_ref[c % 2, part * (FF_LB // 2) + j] = a[:, j * LANES:(j + 1) * LANES]

    def conv(c, lb, st):
        cs = slice(col(c, lb), col(c, lb) + LANES)
        return (a_ref[c % 2, lb, pl.ds(st - 1, seg_len, stride=1), :] * cw_ref[0:1, cs]
                + a_ref[c % 2, lb, st:st + seg_len, :] * cw_ref[1:2, cs]
                + a_ref[c % 2, lb, pl.ds(st + 1, seg_len, stride=1), :] * cw_ref[2:3, cs] + cb_ref[:, cs])

    up(0)
    for c in range(N_FF_CHUNKS):
        if c + 1 < N_FF_CHUNKS:
            up(c + 1)
        for s, st in enumerate(starts):
            for lb in range(FF_LB // 2):
                gate = conv(c, lb, st)
                val = conv(c, lb + FF_LB // 2, st)
                act_ref[s * seg_len:(s + 1) * seg_len, c * FF_CHUNK + lb * LANES:c * FF_CHUNK + (lb + 1) * LANES] = (
                    gate * jax.nn.sigmoid(gate) * val).astype(BF16)
    y = o_ref[...] + mod_ref[5:6, :] * _dot(act_ref[...], wout_ref[...])
    if final:
        y = _rms(y, gfin_ref[...])
    o_ref[...] = y


def _ffn(x, parts, mods, mod_row, pw, l, seq_len, g_final, name):
    t = x.shape[0]
    nt = t // TM
    hb = TM // BF16_ROWS
    n_hblk = t // BF16_ROWS
    final = g_final is not None
    halo = seq_len > TM
    ext = TM + 2 * BF16_ROWS if halo else TM + (TM // seq_len + 1) * BF16_ROWS
    widths = tuple(p.shape[1] for p in parts)

    def tiled(a):
        w = a.shape[1]
        main = pl.BlockSpec((TM, w), lambda i: (i, 0))
        if not halo:
            return [main], [a]
        prev = pl.BlockSpec((BF16_ROWS, w), lambda i: (jnp.maximum(i * hb - 1, 0), 0))
        nxt = pl.BlockSpec((BF16_ROWS, w), lambda i: (jnp.minimum((i + 1) * hb, n_hblk - 1), 0))
        return [prev, main, nxt], [a, a, a]

    in_specs, args = [], []
    for a in (x,) + tuple(parts):
        sp, ar = tiled(a)
        in_specs += sp
        args += ar
    in_specs += [pl.BlockSpec((None, 6, D_MODEL), lambda i: (mod_row(i), 0, 0)),
                 _layer_spec(l, 1, D_MODEL),
                 _layer_spec(l, D_MODEL, D_MODEL, single=True),
                 _layer_spec(l, D_MODEL, 2 * D_FF, single=True),
                 _layer_spec(l, 3, 2 * D_FF, single=True),
                 _layer_spec(l, 1, 2 * D_FF, single=True),
                 _layer_spec(l, D_FF, D_MODEL, single=True)]
    args += [mods, pw["g_ffn"], pw["w_out"], pw["w_ffn_in"], pw["conv_w"], pw["conv_b"], pw["w_ffn_out"]]
    if final:
        in_specs.append(pl.BlockSpec((1, D_MODEL), lambda i: (0, 0)))
        args.append(g_final)
    return pl.pallas_call(
        functools.partial(_ffn_kernel, seq_len, final, widths),
        grid=(nt,),
        in_specs=in_specs,
        out_specs=pl.BlockSpec((TM, D_MODEL), lambda i: (i, 0)),
        out_shape=jax.ShapeDtypeStruct((t, D_MODEL), F32),
        scratch_shapes=[pltpu.VMEM((ext, D_MODEL), BF16), pltpu.VMEM((ext, D_MODEL), BF16),
                        pltpu.VMEM((2, FF_LB, ext, LANES), F32), pltpu.VMEM((TM, D_FF), BF16)],
        compiler_params=_cparams(1),
        name=name,
    )(*args)


def _pack_in_kernel(w_ref, o_ref):
    rows = w_ref.shape[0]
    o_ref[:, :OFF_KR] = w_ref[:, :OFF_KR].astype(BF16)
    kr = w_ref[:, OFF_KR:OFF_KR + MLA_ROPE]
    z = lambda n: jnp.zeros((rows, n), F32)
    o_ref[:, OFF_KR:OFF_UV] = jnp.concatenate([z(HALF), kr, z(LANES - HALF - MLA_ROPE)], axis=-1).astype(BF16)
    o_ref[:, OFF_UV:] = w_ref[:, OFF_KR + MLA_ROPE:].astype(BF16)


def _pack_mla_kernel(wuq_ref, wukv_ref, ouq_ref, oukv_ref):
    dq = MLA_NOPE + MLA_ROPE
    zq = jnp.zeros((MLA_Q_RANK, LANES - dq), F32)
    zk = jnp.zeros((MLA_KV_RANK, LANES - MLA_NOPE), F32)
    dkv = MLA_NOPE + MLA_V
    for hd in range(MLA_HEADS):
        ouq_ref[:, hd * LANES:(hd + 1) * LANES] = jnp.concatenate(
            [wuq_ref[:, hd * dq:(hd + 1) * dq], zq], axis=-1).astype(BF16)
        oukv_ref[:, hd * LANES:(hd + 1) * LANES] = jnp.concatenate(
            [wukv_ref[:, hd * dkv:hd * dkv + MLA_NOPE], zk], axis=-1).astype(BF16)
    voff = MLA_HEADS * LANES
    for p in range(MLA_HEADS // 2):
        oukv_ref[:, voff + p * LANES:voff + (p + 1) * LANES] = jnp.concatenate(
            [wukv_ref[:, (2 * p + j) * dkv + MLA_NOPE:(2 * p + j + 1) * dkv] for j in range(2)], axis=-1).astype(BF16)


def _pack_weights(w_in, w_uq, w_ukv, w_sgu, b_sgu, w_out, w_ffn_in, ffn_conv_w, ffn_conv_b, w_ffn_out,
                  g_mix, g_cq, g_ckv, g_sgu, g_ffn):
    nl = w_in.shape[0]
    rows = 256
    w_in_p = pl.pallas_call(
        _pack_in_kernel,
        grid=(nl, D_MODEL // rows),
        in_specs=[pl.BlockSpec((None, rows, w_in.shape[2]), lambda l, i: (l, i, 0))],
        out_specs=pl.BlockSpec((None, rows, IN_COLS), lambda l, i: (l, i, 0)),
        out_shape=jax.ShapeDtypeStruct((nl, D_MODEL, IN_COLS), BF16),
        compiler_params=_cparams(2),
        name="pack_w_in",
    )(w_in)
    w_uq_p, w_ukv_p = pl.pallas_call(
        _pack_mla_kernel,
        grid=(nl,),
        in_specs=[pl.BlockSpec((None,) + w_uq.shape[1:], lambda l: (l, 0, 0)),
                  pl.BlockSpec((None,) + w_ukv.shape[1:], lambda l: (l, 0, 0))],
        out_specs=[pl.BlockSpec((None, MLA_Q_RANK, MLA_HEADS * LANES), lambda l: (l, 0, 0)),
                   pl.BlockSpec((None, MLA_KV_RANK, KV_COLS), lambda l: (l, 0, 0))],
        out_shape=[jax.ShapeDtypeStruct((nl, MLA_Q_RANK, MLA_HEADS * LANES), BF16),
                   jax.ShapeDtypeStruct((nl, MLA_KV_RANK, KV_COLS), BF16)],
        compiler_params=_cparams(1),
        name="pack_w_mla",
    )(w_uq, w_ukv)
    b_sgu_p = jnp.repeat(jnp.swapaxes(b_sgu, 1, 2), SGU_WIDTH // SGU_GROUPS, axis=-1)
    return dict(
        w_in=w_in_p, w_uq=w_uq_p, w_ukv=w_ukv_p, w_sgu=w_sgu.astype(BF16), b_sgu=b_sgu_p,
        w_out=w_out.astype(BF16), w_ffn_in=w_ffn_in.astype(BF16), conv_w=ffn_conv_w,
        conv_b=ffn_conv_b[:, None, :], w_ffn_out=w_ffn_out.astype(BF16),
        g_mix=g_mix[:, None, :], g_cq=g_cq[:, None, :], g_ckv=g_ckv[:, None, :], g_sgu=g_sgu[:, None, :],
        g_ffn=g_ffn[:, None, :])


def _rope_tables(n_tokens):
    t = jnp.arange(n_tokens)
    n_freq = MLA_ROPE // 4
    inv_freq = ROPE_THETA ** (-jnp.arange(n_freq, dtype=F32) / n_freq)
    ang_r = (t // GRID_W).astype(F32)[:, None] * inv_freq
    ang_c = (t % GRID_W).astype(F32)[:, None] * inv_freq
    ones = jnp.ones((n_tokens, HALF), F32)
    tail = LANES - HALF - MLA_ROPE
    cos = jnp.concatenate([ones, jnp.cos(ang_r), jnp.cos(ang_r), jnp.cos(ang_c), jnp.cos(ang_c),
                           ones[:, :tail]], axis=-1)
    sin = jnp.concatenate([0 * ones, -jnp.sin(ang_r), jnp.sin(ang_r), -jnp.sin(ang_c), jnp.sin(ang_c),
                           0 * ones[:, :tail]], axis=-1)
    return cos, sin


def kernel(x_prompt, x_sample, cache_na_k, cache_na_v, cache_mla_ckv, cache_mla_krope, c, c_ctx, w_mod, b_mod,
           g_mix, w_in, na_rpb, g_cq, w_uq, g_ckv, w_ukv, g_sgu, w_sgu, b_sgu, w_out, g_ffn, w_ffn_in,
           ffn_conv_w, ffn_conv_b, w_ffn_out, g_final):
    n_ctx, n_lat = x_prompt.shape[0], x_sample.shape[0]
    t_ctx, t_lat = n_ctx * SEQ, n_lat * DEC_SEQ
    pw = _pack_weights(w_in, w_uq, w_ukv, w_sgu, b_sgu, w_out, w_ffn_in, ffn_conv_w, ffn_conv_b,
                       w_ffn_out, g_mix, g_cq, g_ckv, g_sgu, g_ffn)
    g_fin = g_final[None, :]

    cond = jnp.concatenate([c_ctx[None, :], c, jnp.zeros((8 - 1 - n_lat, D_MODEL), F32)], axis=0)
    mods = _modulation(cond, w_mod, b_mod)
    rope_tabs = _rope_tables(DEC_SEQ)
    na_bias = _na_bias(na_rpb)
    kr_pad = jnp.pad(cache_mla_krope, ((0, 0), (0, 0), (0, 0), (HALF, LANES - HALF - MLA_ROPE)))
    cache_kp, cache_vp = _cache_kv(cache_mla_ckv, kr_pad, pw["w_ukv"])
    cache_k = cache_na_k.reshape(n_lat, DEPTH, PAST_LEN, NA_WIDTH)
    cache_v = cache_na_v.reshape(n_lat, DEPTH, PAST_LEN, NA_WIDTH)

    lat_tiles = DEC_SEQ // TM
    ctx_row = lambda i: 0
    lat_row = lambda i: 1 + i // lat_tiles

    xp = x_prompt.reshape(t_ctx, D_MODEL)
    xs = x_sample.reshape(t_lat, D_MODEL)
    new_k, new_v, new_ckv, new_kr = [], [], [], []
    for l in range(DEPTH):
        last = l == DEPTH - 1
        m = mods[l]
        qa, ka, va, qm, kp, vp, oc, ka_f, va_f, ckv_f, kr_f = _project(xp, m, ctx_row, pw, l, None, True, 1)
        new_k.append(ka_f)
        new_v.append(va_f)
        new_ckv.append(ckv_f)
        new_kr.append(kr_f[:, HALF:HALF + MLA_ROPE])
        o_ab = _ctx_attention(qa, ka, va, qm, kp, vp, n_ctx)
        xp = _ffn(xp, (o_ab, oc), m, ctx_row, pw, l, SEQ, g_fin if last else None, "ctx_ffn")

        qa, ka, va, qm, kp, vp, oc = _project(xs, m, lat_row, pw, l, rope_tabs, False, lat_tiles)
        o_a = _na_attention(qa, ka, va, cache_k, cache_v, na_bias, l, n_lat)
        o_b = _lat_mla(qm, cache_kp, cache_vp, kp, vp, l, n_lat)
        xs = _ffn(xs, (o_a, o_b, oc), m, lat_row, pw, l, DEC_SEQ, g_fin if last else None, "lat_ffn")

    def stacked(parts, tail):
        return jnp.stack([a.reshape((n_ctx, SEQ) + tail) for a in parts], axis=1)

    return (xp.reshape(n_ctx, SEQ, D_MODEL), xs.reshape(n_lat, DEC_SEQ, D_MODEL),
            stacked(new_k, (NA_HEADS, HEAD_DIM)), stacked(new_v, (NA_HEADS, HEAD_DIM)),
            stacked(new_ckv, (MLA_KV_RANK,)), stacked(new_kr, (MLA_ROPE,)))
```

```python
import functools
import math

import jax
import jax.numpy as jnp
from jax import lax
from jax.experimental import pallas as pl
from jax.experimental.pallas import tpu as pltpu

F32 = jnp.float32
BF16 = jnp.bfloat16

D_MODEL = 1024
DEPTH = 4
SEQ = 256
DEC_SEQ = 2048
PAST_LEN = 256
GRID_W = 64
HEAD_DIM = 64
NA_WIDTH = 256
NA_HEADS = 4
NA_WIN_R = 8
NA_WIN_C = 16
MLA_HEADS = 8
MLA_NOPE = 64
MLA_ROPE = 32
MLA_V = 64
MLA_WIDTH = MLA_HEADS * MLA_V
MLA_Q_RANK = 384
MLA_KV_RANK = 256
SGU_WIDTH = 256
SGU_GROUPS = 4
SGU_CHUNK = 128
D_FF = 2816
ROPE_THETA = 10000.0
EPS = 1e-6
NEG_INF = -1e30
LOG2E = math.log2(math.e)
NA_QSCALE = HEAD_DIM ** -0.5 * LOG2E
MLA_QSCALE = (MLA_NOPE + MLA_ROPE) ** -0.5 * LOG2E

LANES = 128
BF16_ROWS = 16
HALF = LANES // 2

OFF_QA, OFF_KA, OFF_VA = 0, 256, 512
OFF_CQ = 768
OFF_CKV = OFF_CQ + MLA_Q_RANK
OFF_KR = OFF_CKV + MLA_KV_RANK
OFF_UV = OFF_KR + LANES
IN_COLS = OFF_UV + 2 * SGU_WIDTH
KV_COLS = MLA_HEADS * LANES + MLA_HEADS * MLA_V

FF_CHUNK = 256
N_FF_CHUNKS = D_FF // FF_CHUNK
FF_LB = 2 * FF_CHUNK // LANES

TM = 512
NA_QROWS = 4
NA_QBLK = NA_QROWS * GRID_W
NA_KBLKS = 3
VMEM_LIMIT = 56 * 1024 * 1024


def _cparams(n_axes):
    return pltpu.CompilerParams(dimension_semantics=("arbitrary",) * n_axes,
                                vmem_limit_bytes=VMEM_LIMIT)


def _layer_spec(l, *shape, single=False):
    mode = dict(pipeline_mode=pl.Buffered(1)) if single else {}
    return pl.BlockSpec((None,) + shape, lambda *_: (l,) + (0,) * len(shape), **mode)


def _rms(x, g):
    ms = jnp.mean(x * x, axis=-1, keepdims=True)
    return x * lax.rsqrt(ms + EPS) * g


def _dot(a, b):
    return jnp.dot(a, b, preferred_element_type=F32)


def _dot_nt(a, b):
    return lax.dot_general(a, b, (((1,), (1,)), ((), ())), preferred_element_type=F32)


def _half_mask(parity):
    lane = lax.broadcasted_iota(jnp.int32, (1, LANES), 1)
    return (lane // HALF) == parity


def _keep(mask, x):
    return jnp.where(mask, x, jnp.zeros_like(x))


def _with_ones(own, parity, v):
    lane = lax.broadcasted_iota(jnp.int32, (1, LANES), 1)
    ones = (lane == HALF * (1 - parity)).astype(v.dtype)
    return jnp.where(own, v, ones)


def _normalised(own, parity, parts):
    o = functools.reduce(jnp.add, parts)
    spare = HALF * (1 - parity)
    return jnp.where(own, o, 0.0) / o[:, spare:spare + 1]


def _mod_kernel(c_ref, w_ref, b_ref, o_ref):
    c = c_ref[...]
    s = c * jax.nn.sigmoid(c)
    o_ref[...] = _dot(s.astype(BF16), w_ref[...].astype(BF16)) + b_ref[...]


def _modulation(cond, w_mod, b_mod):
    n = 6
    wide = 2 * D_MODEL
    out = pl.pallas_call(
        _mod_kernel,
        grid=(DEPTH, n * D_MODEL // wide),
        in_specs=[
            pl.BlockSpec((8, D_MODEL), lambda l, j: (0, 0)),
            pl.BlockSpec((None, D_MODEL, wide), lambda l, j: (l, 0, j)),
            pl.BlockSpec((None, 1, wide), lambda l, j: (l, 0, j)),
        ],
        out_specs=pl.BlockSpec((None, 8, wide), lambda l, j: (l, 0, j)),
        out_shape=jax.ShapeDtypeStruct((DEPTH, 8, n * D_MODEL), F32),
        compiler_params=_cparams(2),
        name="modulation",
    )(cond, w_mod, b_mod.reshape(DEPTH, 1, n * D_MODEL))
    return out.reshape(DEPTH, 8, n, D_MODEL)


def _rope(x, cos, sin, lane_lo):
    up = pltpu.roll(x, LANES - 8, axis=1)
    dn = pltpu.roll(x, 8, axis=1)
    return x * cos + jnp.where(lane_lo, up, dn) * sin


def _proj_kernel(rope, emit_f32, x_ref, mod_ref, gmix_ref, win_ref, gcq_ref, wuq_ref, gckv_ref,
                 wukv_ref, gsgu_ref, wsgu_ref, bsg_ref, *refs):
    if rope:
        cos_ref, sin_ref = refs[:2]
        refs = refs[2:]
    qa_ref, ka_ref, va_ref, qm_ref, kp_ref, vp_ref, oc_ref = refs[:7]
    refs = refs[7:]
    if emit_f32:
        kaf_ref, vaf_ref, ckvf_ref, krf_ref = refs[:4]
        refs = refs[4:]
    hb_ref, z_ref = refs

    x = x_ref[...]
    h = _rms(x, gmix_ref[...]) * (1.0 + mod_ref[1:2, :]) + mod_ref[0:1, :]
    hb_ref[...] = h.astype(BF16)
    tm = x.shape[0]
    z_ref[:, OFF_UV:] = _dot(hb_ref[...], win_ref[:, OFF_UV:])
    z_ref[:, OFF_CQ:OFF_UV] = _dot(hb_ref[...], win_ref[:, OFF_CQ:OFF_UV])
    z_ref[:, :OFF_CQ] = _dot(hb_ref[...], win_ref[:, :OFF_CQ])
    lane = lax.broadcasted_iota(jnp.int32, (1, LANES), 1)
    lane_lo = (lane % 16) < 8
    if rope:
        cos = cos_ref[...]
        sin = sin_ref[...]

    uv = jax.nn.gelu(z_ref[:, OFF_UV:OFF_UV + 2 * SGU_WIDTH])
    u = uv[:, :SGU_WIDTH]
    vn = _rms(uv[:, SGU_WIDTH:], gsgu_ref[...])
    even = _half_mask(0)
    for ch in range(tm // SGU_CHUNK):
        rows = slice(ch * SGU_CHUNK, (ch + 1) * SGU_CHUNK)
        for p in range(SGU_GROUPS // 2):
            sl = slice(p * LANES, (p + 1) * LANES)
            vc = vn[rows, sl]
            mixed = (_dot(wsgu_ref[2 * p], jnp.where(even, vc, 0.0).astype(BF16))
                     + _dot(wsgu_ref[2 * p + 1], jnp.where(even, 0.0, vc).astype(BF16))
                     + bsg_ref[:, sl])
            oc_ref[rows, sl] = (u[rows, sl] * mixed).astype(BF16)

    cq = z_ref[:, OFF_CQ:OFF_CQ + MLA_Q_RANK]
    cqn = _rms(cq, gcq_ref[...]).astype(BF16)
    qm = _dot(cqn, wuq_ref[...])
    for hd in range(MLA_HEADS):
        qh = qm[:, hd * LANES:(hd + 1) * LANES]
        if rope:
            qh = _rope(qh, cos, sin, lane_lo)
        qm_ref[hd] = (qh * MLA_QSCALE).astype(BF16)

    ckv = z_ref[:, OFF_CKV:OFF_CKV + MLA_KV_RANK]
    ckvn = _rms(ckv, gckv_ref[...])
    kr = z_ref[:, OFF_KR:OFF_KR + LANES]
    if emit_f32:
        ckvf_ref[...] = ckvn
        krf_ref[...] = kr
    if rope:
        kr = _rope(kr, cos, sin, lane_lo)
    kv = _dot(ckvn.astype(BF16), wukv_ref[...])
    for hd in range(MLA_HEADS):
        kp_ref[hd] = (kv[:, hd * LANES:(hd + 1) * LANES] + kr).astype(BF16)
    voff = MLA_HEADS * LANES
    for p in range(MLA_HEADS // 2):
        vp_ref[p] = kv[:, voff + p * LANES:voff + (p + 1) * LANES].astype(BF16)

    qa = z_ref[:, OFF_QA:OFF_QA + NA_WIDTH] * NA_QSCALE
    ka = z_ref[:, OFF_KA:OFF_KA + NA_WIDTH]
    va = z_ref[:, OFF_VA:OFF_VA + NA_WIDTH]
    for p in range(NA_HEADS // 2):
        sl = slice(p * LANES, (p + 1) * LANES)
        qa_ref[p] = qa[:, sl].astype(BF16)
        ka_ref[p] = ka[:, sl].astype(BF16)
        va_ref[p] = va[:, sl].astype(BF16)
    if emit_f32:
        kaf_ref[...] = ka
        vaf_ref[...] = va


def _project(x, mods, mod_row, pw, l, rope_tabs, emit_f32, tiles_per_seq):
    t = x.shape[0]
    nt = t // TM
    rope = rope_tabs is not None
    in_specs = [
        pl.BlockSpec((TM, D_MODEL), lambda i: (i, 0)),
        pl.BlockSpec((None, 6, D_MODEL), lambda i: (mod_row(i), 0, 0)),
        _layer_spec(l, 1, D_MODEL),
        _layer_spec(l, D_MODEL, IN_COLS),
        _layer_spec(l, 1, MLA_Q_RANK),
        _layer_spec(l, MLA_Q_RANK, MLA_HEADS * LANES),
        _layer_spec(l, 1, MLA_KV_RANK),
        _layer_spec(l, MLA_KV_RANK, KV_COLS),
        _layer_spec(l, 1, SGU_WIDTH),
        _layer_spec(l, SGU_GROUPS, SGU_CHUNK, SGU_CHUNK),
        _layer_spec(l, SGU_CHUNK, SGU_WIDTH),
    ]
    args = [x, mods, pw["g_mix"], pw["w_in"], pw["g_cq"], pw["w_uq"], pw["g_ckv"], pw["w_ukv"],
            pw["g_sgu"], pw["w_sgu"], pw["b_sgu"]]
    if rope:
        tab_spec = pl.BlockSpec((TM, LANES), lambda i: (i % tiles_per_seq, 0))
        in_specs += [tab_spec, tab_spec]
        args += list(rope_tabs)

    def heads(n):
        return (pl.BlockSpec((n, TM, LANES), lambda i: (0, i, 0)),
                jax.ShapeDtypeStruct((n, t, LANES), BF16))

    def flat(w, dt):
        return (pl.BlockSpec((TM, w), lambda i: (i, 0)), jax.ShapeDtypeStruct((t, w), dt))

    outs = [heads(2), heads(2), heads(2), heads(MLA_HEADS), heads(MLA_HEADS), heads(MLA_HEADS // 2),
            flat(SGU_WIDTH, BF16)]
    if emit_f32:
        outs += [flat(NA_WIDTH, F32), flat(NA_WIDTH, F32), flat(MLA_KV_RANK, F32), flat(LANES, F32)]
    return pl.pallas_call(
        functools.partial(_proj_kernel, rope, emit_f32),
        grid=(nt,),
        in_specs=in_specs,
        out_specs=[o[0] for o in outs],
        out_shape=[o[1] for o in outs],
        scratch_shapes=[pltpu.VMEM((TM, D_MODEL), BF16), pltpu.VMEM((TM, IN_COLS), F32)],
        compiler_params=_cparams(1),
        name="project_lat" if rope else "project_ctx",
    )(*args)


def _cache_kv_kernel(ckv_ref, kr_ref, wukv_ref, kp_ref, vp_ref):
    kv = _dot(ckv_ref[...].astype(BF16), wukv_ref[...])
    kr = kr_ref[...]
    for hd in range(MLA_HEADS):
        kp_ref[hd] = (kv[:, hd * LANES:(hd + 1) * LANES] + kr).astype(BF16)
    voff = MLA_HEADS * LANES
    for p in range(MLA_HEADS // 2):
        vp_ref[p] = kv[:, voff + p * LANES:voff + (p + 1) * LANES].astype(BF16)


def _cache_kv(cache_ckv, cache_kr_pad, w_ukv):
    b = cache_ckv.shape[0]
    return pl.pallas_call(
        _cache_kv_kernel,
        grid=(DEPTH, b),
        in_specs=[
            pl.BlockSpec((None, None, PAST_LEN, MLA_KV_RANK), lambda l, i: (i, l, 0, 0)),
            pl.BlockSpec((None, None, PAST_LEN, LANES), lambda l, i: (i, l, 0, 0)),
            pl.BlockSpec((None, MLA_KV_RANK, KV_COLS), lambda l, i: (l, 0, 0)),
        ],
        out_specs=[
            pl.BlockSpec((None, MLA_HEADS, PAST_LEN, LANES), lambda l, i: (l, 0, i, 0)),
            pl.BlockSpec((None, MLA_HEADS // 2, PAST_LEN, LANES), lambda l, i: (l, 0, i, 0)),
        ],
        out_shape=[
            jax.ShapeDtypeStruct((DEPTH, MLA_HEADS, b * PAST_LEN, LANES), BF16),
            jax.ShapeDtypeStruct((DEPTH, MLA_HEADS // 2, b * PAST_LEN, LANES), BF16),
        ],
        compiler_params=_cparams(2),
        name="cache_kv",
    )(cache_ckv, cache_kr_pad, w_ukv)


def _ctx_attn_kernel(qa_ref, ka_ref, va_ref, qm_ref, kp_ref, vp_ref, o_ref):
    for p in range(NA_HEADS // 2 + MLA_HEADS // 2):
        out = None
        for half in range(2):
            own = _half_mask(half)
            if p < NA_HEADS // 2:
                q = _keep(own, qa_ref[p])
                k = ka_ref[p]
                v = va_ref[p]
            else:
                hd = 2 * (p - NA_HEADS // 2) + half
                q = qm_ref[hd]
                k = kp_ref[hd]
                v = vp_ref[hd // 2]
            s = _dot_nt(q, k)
            p_ = jnp.exp2(s - jnp.max(s, axis=-1, keepdims=True))
            o = _dot(p_.astype(BF16), _keep(own, v)) / jnp.sum(p_, axis=-1, keepdims=True)
            out = o if out is None else out + o
        o_ref[:, p * LANES:(p + 1) * LANES] = out.astype(BF16)


def _ctx_attention(qa, ka, va, qm, kp, vp, n_seq):
    heads = lambda n: pl.BlockSpec((n, SEQ, LANES), lambda b: (0, b, 0))
    width = NA_WIDTH + MLA_WIDTH
    return pl.pallas_call(
        _ctx_attn_kernel,
        grid=(n_seq,),
        in_specs=[heads(2), heads(2), heads(2), heads(MLA_HEADS), heads(MLA_HEADS), heads(MLA_HEADS // 2)],
        out_specs=pl.BlockSpec((SEQ, width), lambda b: (b, 0)),
        out_shape=jax.ShapeDtypeStruct((n_seq * SEQ, width), BF16),
        compiler_params=_cparams(1),
        name="ctx_attention",
    )(qa, ka, va, qm, kp, vp)


def _lat_mla_kernel(q_ref, kc_ref, kl_ref, vc_ref, vl_ref, o_ref, s_ref, m_ref, p_ref):
    nc = kc_ref.shape[1]
    outs = {}

    def scores(hd):
        q = q_ref[hd]
        sc = _dot_nt(q, kc_ref[hd])
        sl = _dot_nt(q, kl_ref[hd])
        s_ref[hd % 2, :, 0:nc] = sc
        s_ref[hd % 2, :, nc:] = sl
        m_ref[hd % 2] = jnp.maximum(_lane_block_max(sc), _lane_block_max(sl))

    def softmax(hd):
        m = jnp.max(m_ref[hd % 2], axis=-1, keepdims=True)
        p_ref[hd % 2] = jnp.exp2(s_ref[hd % 2] - m).astype(BF16)

    def values(hd):
        own = _half_mask(hd % 2)
        o = _normalised(own, hd % 2, [_dot(p_ref[hd % 2, :, 0:nc], _with_ones(own, hd % 2, vc_ref[hd // 2])),
                                      _dot(p_ref[hd % 2, :, nc:], _with_ones(own, hd % 2, vl_ref[hd // 2]))])
        if hd % 2 == 0:
            outs[hd // 2] = o
        else:
            o_ref[:, (hd // 2) * LANES:(hd // 2 + 1) * LANES] = (outs.pop(hd // 2) + o).astype(BF16)

    for t in range(MLA_HEADS + 2):
        if t < MLA_HEADS:
            scores(t)
        if 0 <= t - 1 < MLA_HEADS:
            softmax(t - 1)
        if 0 <= t - 2 < MLA_HEADS:
            values(t - 2)


def _lat_mla(q, kc, vc, kl, vl, l, n_batch):
    nq = DEC_SEQ // TM
    n_keys = PAST_LEN + DEC_SEQ
    return pl.pallas_call(
        _lat_mla_kernel,
        grid=(n_batch, nq),
        in_specs=[pl.BlockSpec((MLA_HEADS, TM, LANES), lambda b, i: (0, b * nq + i, 0)),
                  pl.BlockSpec((None, MLA_HEADS, PAST_LEN, LANES), lambda b, i: (l, 0, b, 0)),
                  pl.BlockSpec((MLA_HEADS, DEC_SEQ, LANES), lambda b, i: (0, b, 0)),
                  pl.BlockSpec((None, MLA_HEADS // 2, PAST_LEN, LANES), lambda b, i: (l, 0, b, 0)),
                  pl.BlockSpec((MLA_HEADS // 2, DEC_SEQ, LANES), lambda b, i: (0, b, 0))],
        out_specs=pl.BlockSpec((TM, MLA_WIDTH), lambda b, i: (b * nq + i, 0)),
        out_shape=jax.ShapeDtypeStruct((n_batch * DEC_SEQ, MLA_WIDTH), BF16),
        scratch_shapes=[pltpu.VMEM((2, TM, n_keys), F32), pltpu.VMEM((2, TM, LANES), F32),
                        pltpu.VMEM((2, TM, n_keys), BF16)],
        compiler_params=_cparams(2),
        name="lat_mla",
    )(q, kc, kl, vc, vl)


def _na_bias_kernel(rpb_ref, o_ref, tab_ref):
    l = pl.program_id(0)
    hd = pl.program_id(1)
    n_dr = 2 * NA_WIN_R - 1
    n_dc = 2 * NA_WIN_C - 1
    base = (l * NA_HEADS + hd) * n_dr * n_dc
    qc = lax.broadcasted_iota(jnp.int32, (GRID_W, LANES), 0)
    kc = lax.broadcasted_iota(jnp.int32, (GRID_W, LANES), 1) % GRID_W
    diff = kc - qc
    cs = jnp.clip(qc - NA_WIN_C // 2, 0, GRID_W - NA_WIN_C)
    in_win = (kc >= cs) & (kc < cs + NA_WIN_C)
    neg = jnp.full((GRID_W, LANES), NEG_INF * LOG2E, F32)
    for dr in range(n_dr):
        acc = neg
        for dc in range(n_dc):
            acc = jnp.where(diff == dc - (NA_WIN_C - 1), rpb_ref[base + dr * n_dc + dc] * LOG2E, acc)
        tab_ref[dr] = jnp.where(in_win, acc, neg)
    left = lax.broadcasted_iota(jnp.int32, (GRID_W, LANES), 1) < GRID_W
    n_krows = NA_KBLKS * NA_QBLK // GRID_W
    cases = ((0, lambda qr: 0), (-NA_QROWS, lambda qr: qr), (-2 * NA_QROWS, lambda qr: NA_QROWS))
    for c, (shift, first) in enumerate(cases):
        for qr in range(NA_QROWS):
            for kp in range(n_krows // 2):
                tiles = []
                for kr in (2 * kp, 2 * kp + 1):
                    ok = first(qr) <= kr < first(qr) + NA_WIN_R
                    tiles.append(tab_ref[kr - qr + shift + NA_WIN_R - 1] if ok else neg)
                o_ref[c, qr * GRID_W:(qr + 1) * GRID_W, kp * LANES:(kp + 1) * LANES] = (
                    jnp.where(left, tiles[0], tiles[1]))


def _na_bias(na_rpb):
    n_k = NA_KBLKS * NA_QBLK
    return pl.pallas_call(
        _na_bias_kernel,
        grid=(DEPTH, NA_HEADS),
        in_specs=[pl.BlockSpec(memory_space=pltpu.SMEM)],
        out_specs=pl.BlockSpec((None, 3, None, NA_QBLK, n_k), lambda l, h: (l, 0, h, 0, 0)),
        out_shape=jax.ShapeDtypeStruct((DEPTH, 3, NA_HEADS, NA_QBLK, n_k), F32),
        scratch_shapes=[pltpu.VMEM((2 * NA_WIN_R - 1, GRID_W, LANES), F32)],
        compiler_params=_cparams(2),
        name="na_bias",
    )(na_rpb.reshape(-1))


def _lane_block_max(s):
    return functools.reduce(jnp.maximum, [s[:, j * LANES:(j + 1) * LANES] for j in range(s.shape[1] // LANES)])


def _na_kernel(q_ref, k0_ref, k1_ref, k2_ref, v0_ref, v1_ref, v2_ref, kc_ref, vc_ref, bias_ref, o_ref,
               s_ref, m_ref, p_ref):
    k_refs = (k0_ref, k1_ref, k2_ref)
    v_refs = (v0_ref, v1_ref, v2_ref)
    outs = {}
    n_loc = NA_KBLKS * NA_QBLK

    def scores(hd):
        p = hd // 2
        q = _keep(_half_mask(hd % 2), q_ref[p])
        m = None
        for i in range(NA_KBLKS):
            s = _dot_nt(q, k_refs[i][p]) + bias_ref[hd, :, i * NA_QBLK:(i + 1) * NA_QBLK]
            s_ref[hd % 2, :, i * NA_QBLK:(i + 1) * NA_QBLK] = s
            m = _lane_block_max(s) if m is None else jnp.maximum(m, _lane_block_max(s))
        s = _dot_nt(q, kc_ref[:, p * LANES:(p + 1) * LANES].astype(BF16))
        s_ref[hd % 2, :, n_loc:] = s
        m_ref[hd % 2] = jnp.maximum(m, _lane_block_max(s))

    def softmax(hd):
        m = jnp.max(m_ref[hd % 2], axis=-1, keepdims=True)
        p_ref[hd % 2] = jnp.exp2(s_ref[hd % 2] - m).astype(BF16)

    def values(hd):
        p = hd // 2
        own = _half_mask(hd % 2)
        vc = vc_ref[:, p * LANES:(p + 1) * LANES].astype(BF16)
        parts = [_dot(p_ref[hd % 2, :, n_loc:], _with_ones(own, hd % 2, vc))]
        for i in range(NA_KBLKS):
            parts.append(_dot(p_ref[hd % 2, :, i * NA_QBLK:(i + 1) * NA_QBLK],
                              _with_ones(own, hd % 2, v_refs[i][p])))
        o = _normalised(own, hd % 2, parts)
        if hd % 2 == 0:
            outs[p] = o
        else:
            o_ref[:, p * LANES:(p + 1) * LANES] = (outs.pop(p) + o).astype(BF16)

    for t in range(NA_HEADS + 2):
        if t < NA_HEADS:
            scores(t)
        if 0 <= t - 1 < NA_HEADS:
            softmax(t - 1)
        if 0 <= t - 2 < NA_HEADS:
            values(t - 2)


def _na_attention(qa, ka, va, cache_k, cache_v, bias, l, n_batch):
    nblk = DEC_SEQ // NA_QBLK
    max_start = nblk - NA_KBLKS

    def kspec(i):
        return pl.BlockSpec((NA_HEADS // 2, NA_QBLK, LANES),
                            lambda j, b: (0, b * nblk + jnp.clip(j - 1, 0, max_start) + i, 0))

    case = lambda j: jnp.where(j == 0, 0, jnp.where(j == nblk - 1, 2, 1))
    cache_spec = pl.BlockSpec((None, None, PAST_LEN, NA_WIDTH), lambda j, b: (b, l, 0, 0))
    return pl.pallas_call(
        _na_kernel,
        grid=(nblk, n_batch),
        in_specs=[pl.BlockSpec((NA_HEADS // 2, NA_QBLK, LANES), lambda j, b: (0, b * nblk + j, 0)),
                  kspec(0), kspec(1), kspec(2), kspec(0), kspec(1), kspec(2),
                  cache_spec, cache_spec,
                  pl.BlockSpec((None, None, NA_HEADS, NA_QBLK, NA_KBLKS * NA_QBLK),
                               lambda j, b: (l, case(j), 0, 0, 0))],
        out_specs=pl.BlockSpec((NA_QBLK, NA_WIDTH), lambda j, b: (b * nblk + j, 0)),
        out_shape=jax.ShapeDtypeStruct((n_batch * DEC_SEQ, NA_WIDTH), BF16),
        scratch_shapes=[pltpu.VMEM((2, NA_QBLK, (NA_KBLKS + 1) * NA_QBLK), F32),
                        pltpu.VMEM((2, NA_QBLK, LANES), F32),
                        pltpu.VMEM((2, NA_QBLK, (NA_KBLKS + 1) * NA_QBLK), BF16)],
        compiler_params=_cparams(2),
        name="na_attention",
    )(qa, ka, ka, ka, va, va, va, cache_k, cache_v, bias)


def _ffn_kernel(seq_len, final, widths, *refs):
    halo = seq_len > TM
    n_in = 3 if halo else 1
    x_refs = refs[:n_in]
    refs = refs[n_in:]
    part_refs = [refs[i * n_in:(i + 1) * n_in] for i in range(len(widths))]
    refs = refs[n_in * len(widths):]
    mod_ref, gffn_ref, wo_ref, win_ref, cw_ref, cb_ref, wout_ref = refs[:7]
    refs = refs[7:]
    if final:
        gfin_ref = refs[0]
        refs = refs[1:]
    o_ref, mix_ref, hext_ref, a_ref, act_ref = refs
    tm = o_ref.shape[0]
    pad = BF16_ROWS
    if halo:
        tiles_per_seq = seq_len // tm
        pos = pl.program_id(0) % tiles_per_seq
        pieces = [(0, 0, slice(0, pad), pos != 0), (pad, 1, slice(0, tm), None),
                  (pad + tm, 2, slice(0, pad), pos != tiles_per_seq - 1)]
        frames = []
        starts = [pad]
        seg_len = tm
    else:
        nseg = tm // seq_len
        starts = [pad + s * (seq_len + pad) for s in range(nseg)]
        seg_len = seq_len
        pieces = [(starts[s], 0, slice(s * seq_len, (s + 1) * seq_len), None) for s in range(nseg)]
        frames = [s * (seq_len + pad) for s in range(nseg + 1)]
    for f in frames:
        mix_ref[f:f + pad] = jnp.zeros((pad, D_MODEL), BF16)
        hext_ref[f:f + pad] = jnp.zeros((pad, D_MODEL), BF16)
    off = 0
    for n, prefs in zip(widths, part_refs):
        for row, src, rows, _ in pieces:
            mix_ref[row:row + rows.stop - rows.start, off:off + n] = prefs[src][rows, :]
        off += n
    r = _dot(mix_ref[...], wo_ref[...])
    g1 = mod_ref[2:3, :]
    scale = gffn_ref[...] * (1.0 + mod_ref[4:5, :])
    shift = mod_ref[3:4, :]
    for row, src, rows, keep in pieces:
        n_rows = rows.stop - rows.start
        x1 = x_refs[src][rows, :] + g1 * r[row:row + n_rows]
        if src == n_in // 2:
            o_ref[rows, :] = x1
        h2 = (x1 * lax.rsqrt(jnp.mean(x1 * x1, axis=-1, keepdims=True) + EPS) * scale + shift).astype(BF16)
        if keep is not None:
            h2 = jnp.where(keep, h2, jnp.zeros_like(h2))
        hext_ref[row:row + n_rows] = h2

    def col(c, lb):
        half = FF_LB // 2
        return (lb // half) * D_FF + c * FF_CHUNK + (lb % half) * LANES

    def up(c):
        hext = hext_ref[...]
        for part in range(2):
            a = _dot(hext, win_ref[:, part * D_FF + c * FF_CHUNK:part * D_FF + (c + 1) * FF_CHUNK])
            for j in range(FF_LB // 2):
                a_ref[c % 2, part * (FF_LB // 2) + j] = a[:, j * LANES:(j + 1) * LANES]

    def conv(c, lb, st):
        cs = slice(col(c, lb), col(c, lb) + LANES)
        return (a_ref[c % 2, lb, pl.ds(st - 1, seg_len, stride=1), :] * cw_ref[0:1, cs]
                + a_ref[c % 2, lb, st:st + seg_len, :] * cw_ref[1:2, cs]
                + a_ref[c % 2, lb, pl.ds(st + 1, seg_len, stride=1), :] * cw_ref[2:3, cs] + cb_ref[:, cs])

    up(0)
    for c in range(N_FF_CHUNKS):
        if c + 1 < N_FF_CHUNKS:
            up(c + 1)
        for s, st in enumerate(starts):
            for lb in range(FF_LB // 2):
                gate = conv(c, lb, st)
                val = conv(c, lb + FF_LB // 2, st)
                act_ref[s * seg_len:(s + 1) * seg_len, c * FF_CHUNK + lb * LANES:c * FF_CHUNK + (lb + 1) * LANES] = (
                    gate * jax.nn.sigmoid(gate) * val).astype(BF16)
    y = o_ref[...] + mod_ref[5:6, :] * _dot(act_ref[...], wout_ref[...])
    if final:
        y = _rms(y, gfin_ref[...])
    o_ref[...] = y


def _ffn(x, parts, mods, mod_row, pw, l, seq_len, g_final, name):
    t = x.shape[0]
    nt = t // TM
    hb = TM // BF16_ROWS
    n_hblk = t // BF16_ROWS
    final = g_final is not None
    halo = seq_len > TM
    ext = TM + 2 * BF16_ROWS if halo else TM + (TM // seq_len + 1) * BF16_ROWS
    widths = tuple(p.shape[1] for p in parts)

    def tiled(a):
        w = a.shape[1]
        main = pl.BlockSpec((TM, w), lambda i: (i, 0))
        if not halo:
            return [main], [a]
        prev = pl.BlockSpec((BF16_ROWS, w), lambda i: (jnp.maximum(i * hb - 1, 0), 0))
        nxt = pl.BlockSpec((BF16_ROWS, w), lambda i: (jnp.minimum((i + 1) * hb, n_hblk - 1), 0))
        return [prev, main, nxt], [a, a, a]

    in_specs, args = [], []
    for a in (x,) + tuple(parts):
        sp, ar = tiled(a)
        in_specs += sp
        args += ar
    in_specs += [pl.BlockSpec((None, 6, D_MODEL), lambda i: (mod_row(i), 0, 0)),
                 _layer_spec(l, 1, D_MODEL),
                 _layer_spec(l, D_MODEL, D_MODEL, single=True),
                 _layer_spec(l, D_MODEL, 2 * D_FF, single=True),
                 _layer_spec(l, 3, 2 * D_FF, single=True),
                 _layer_spec(l, 1, 2 * D_FF, single=True),
                 _layer_spec(l, D_FF, D_MODEL, single=True)]
    args += [mods, pw["g_ffn"], pw["w_out"], pw["w_ffn_in"], pw["conv_w"], pw["conv_b"], pw["w_ffn_out"]]
    if final:
        in_specs.append(pl.BlockSpec((1, D_MODEL), lambda i: (0, 0)))
        args.append(g_final)
    return pl.pallas_call(
        functools.partial(_ffn_kernel, seq_len, final, widths),
        grid=(nt,),
        in_specs=in_specs,
        out_specs=pl.BlockSpec((TM, D_MODEL), lambda i: (i, 0)),
        out_shape=jax.ShapeDtypeStruct((t, D_MODEL), F32),
        scratch_shapes=[pltpu.VMEM((ext, D_MODEL), BF16), pltpu.VMEM((ext, D_MODEL), BF16),
                        pltpu.VMEM((2, FF_LB, ext, LANES), F32), pltpu.VMEM((TM, D_FF), BF16)],
        compiler_params=_cparams(1),
        name=name,
    )(*args)


def _pack_in_kernel(w_ref, o_ref):
    rows = w_ref.shape[0]
    o_ref[:, :OFF_KR] = w_ref[:, :OFF_KR].astype(BF16)
    kr = w_ref[:, OFF_KR:OFF_KR + MLA_ROPE]
    z = lambda n: jnp.zeros((rows, n), F32)
    o_ref[:, OFF_KR:OFF_UV] = jnp.concatenate([z(HALF), kr, z(LANES - HALF - MLA_ROPE)], axis=-1).astype(BF16)
    o_ref[:, OFF_UV:] = w_ref[:, OFF_KR + MLA_ROPE:].astype(BF16)


def _pack_mla_kernel(wuq_ref, wukv_ref, ouq_ref, oukv_ref):
    dq = MLA_NOPE + MLA_ROPE
    zq = jnp.zeros((MLA_Q_RANK, LANES - dq), F32)
    zk = jnp.zeros((MLA_KV_RANK, LANES - MLA_NOPE), F32)
    dkv = MLA_NOPE + MLA_V
    for hd in range(MLA_HEADS):
        ouq_ref[:, hd * LANES:(hd + 1) * LANES] = jnp.concatenate(
            [wuq_ref[:, hd * dq:(hd + 1) * dq], zq], axis=-1).astype(BF16)
        oukv_ref[:, hd * LANES:(hd + 1) * LANES] = jnp.concatenate(
            [wukv_ref[:, hd * dkv:hd * dkv + MLA_NOPE], zk], axis=-1).astype(BF16)
    voff = MLA_HEADS * LANES
    for p in range(MLA_HEADS // 2):
        oukv_ref[:, voff + p * LANES:voff + (p + 1) * LANES] = jnp.concatenate(
            [wukv_ref[:, (2 * p + j) * dkv + MLA_NOPE:(2 * p + j + 1) * dkv] for j in range(2)], axis=-1).astype(BF16)


def _pack_weights(w_in, w_uq, w_ukv, w_sgu, b_sgu, w_out, w_ffn_in, ffn_conv_w, ffn_conv_b, w_ffn_out,
                  g_mix, g_cq, g_ckv, g_sgu, g_ffn):
    nl = w_in.shape[0]
    rows = 256
    w_in_p = pl.pallas_call(
        _pack_in_kernel,
        grid=(nl, D_MODEL // rows),
        in_specs=[pl.BlockSpec((None, rows, w_in.shape[2]), lambda l, i: (l, i, 0))],
        out_specs=pl.BlockSpec((None, rows, IN_COLS), lambda l, i: (l, i, 0)),
        out_shape=jax.ShapeDtypeStruct((nl, D_MODEL, IN_COLS), BF16),
        compiler_params=_cparams(2),
        name="pack_w_in",
    )(w_in)
    w_uq_p, w_ukv_p = pl.pallas_call(
        _pack_mla_kernel,
        grid=(nl,),
        in_specs=[pl.BlockSpec((None,) + w_uq.shape[1:], lambda l: (l, 0, 0)),
                  pl.BlockSpec((None,) + w_ukv.shape[1:], lambda l: (l, 0, 0))],
        out_specs=[pl.BlockSpec((None, MLA_Q_RANK, MLA_HEADS * LANES), lambda l: (l, 0, 0)),
                   pl.BlockSpec((None, MLA_KV_RANK, KV_COLS), lambda l: (l, 0, 0))],
        out_shape=[jax.ShapeDtypeStruct((nl, MLA_Q_RANK, MLA_HEADS * LANES), BF16),
                   jax.ShapeDtypeStruct((nl, MLA_KV_RANK, KV_COLS), BF16)],
        compiler_params=_cparams(1),
        name="pack_w_mla",
    )(w_uq, w_ukv)
    b_sgu_p = jnp.repeat(jnp.swapaxes(b_sgu, 1, 2), SGU_WIDTH // SGU_GROUPS, axis=-1)
    return dict(
        w_in=w_in_p, w_uq=w_uq_p, w_ukv=w_ukv_p, w_sgu=w_sgu.astype(BF16), b_sgu=b_sgu_p,
        w_out=w_out.astype(BF16), w_ffn_in=w_ffn_in.astype(BF16), conv_w=ffn_conv_w,
        conv_b=ffn_conv_b[:, None, :], w_ffn_out=w_ffn_out.astype(BF16),
        g_mix=g_mix[:, None, :], g_cq=g_cq[:, None, :], g_ckv=g_ckv[:, None, :], g_sgu=g_sgu[:, None, :],
        g_ffn=g_ffn[:, None, :])


def _rope_tables(n_tokens):
    t = jnp.arange(n_tokens)
    n_freq = MLA_ROPE // 4
    inv_freq = ROPE_THETA ** (-jnp.arange(n_freq, dtype=F32) / n_freq)
    ang_r = (t // GRID_W).astype(F32)[:, None] * inv_freq
    ang_c = (t % GRID_W).astype(F32)[:, None] * inv_freq
    ones = jnp.ones((n_tokens, HALF), F32)
    tail = LANES - HALF - MLA_ROPE
    cos = jnp.concatenate([ones, jnp.cos(ang_r), jnp.cos(ang_r), jnp.cos(ang_c), jnp.cos(ang_c),
                           ones[:, :tail]], axis=-1)
    sin = jnp.concatenate([0 * ones, -jnp.sin(ang_r), jnp.sin(ang_r), -jnp.sin(ang_c), jnp.sin(ang_c),
                           0 * ones[:, :tail]], axis=-1)
    return cos, sin


def kernel(x_prompt, x_sample, cache_na_k, cache_na_v, cache_mla_ckv, cache_mla_krope, c, c_ctx, w_mod, b_mod,
           g_mix, w_in, na_rpb, g_cq, w_uq, g_ckv, w_ukv, g_sgu, w_sgu, b_sgu, w_out, g_ffn, w_ffn_in,
           ffn_conv_w, ffn_conv_b, w_ffn_out, g_final):
    n_ctx, n_lat = x_prompt.shape[0], x_sample.shape[0]
    t_ctx, t_lat = n_ctx * SEQ, n_lat * DEC_SEQ
    pw = _pack_weights(w_in, w_uq, w_ukv, w_sgu, b_sgu, w_out, w_ffn_in, ffn_conv_w, ffn_conv_b,
                       w_ffn_out, g_mix, g_cq, g_ckv, g_sgu, g_ffn)
    g_fin = g_final[None, :]

    cond = jnp.concatenate([c_ctx[None, :], c, jnp.zeros((8 - 1 - n_lat, D_MODEL), F32)], axis=0)
    mods = _modulation(cond, w_mod, b_mod)
    rope_tabs = _rope_tables(DEC_SEQ)
    na_bias = _na_bias(na_rpb)
    kr_pad = jnp.pad(cache_mla_krope, ((0, 0), (0, 0), (0, 0), (HALF, LANES - HALF - MLA_ROPE)))
    cache_kp, cache_vp = _cache_kv(cache_mla_ckv, kr_pad, pw["w_ukv"])
    cache_k = cache_na_k.reshape(n_lat, DEPTH, PAST_LEN, NA_WIDTH)
    cache_v = cache_na_v.reshape(n_lat, DEPTH, PAST_LEN, NA_WIDTH)

    lat_tiles = DEC_SEQ // TM
    ctx_row = lambda i: 0
    lat_row = lambda i: 1 + i // lat_tiles

    xp = x_prompt.reshape(t_ctx, D_MODEL)
    xs = x_sample.reshape(t_lat, D_MODEL)
    new_k, new_v, new_ckv, new_kr = [], [], [], []
    for l in range(DEPTH):
        last = l == DEPTH - 1
        m = mods[l]
        qa, ka, va, qm, kp, vp, oc, ka_f, va_f, ckv_f, kr_f = _project(xp, m, ctx_row, pw, l, None, True, 1)
        new_k.append(ka_f)
        new_v.append(va_f)
        new_ckv.append(ckv_f)
        new_kr.append(kr_f[:, HALF:HALF + MLA_ROPE])
        o_ab = _ctx_attention(qa, ka, va, qm, kp, vp, n_ctx)
        xp = _ffn(xp, (o_ab, oc), m, ctx_row, pw, l, SEQ, g_fin if last else None, "ctx_ffn")

        qa, ka, va, qm, kp, vp, oc = _project(xs, m, lat_row, pw, l, rope_tabs, False, lat_tiles)
        o_a = _na_attention(qa, ka, va, cache_k, cache_v, na_bias, l, n_lat)
        o_b = _lat_mla(qm, cache_kp, cache_vp, kp, vp, l, n_lat)
        xs = _ffn(xs, (o_a, o_b, oc), m, lat_row, pw, l, DEC_SEQ, g_fin if last else None, "lat_ffn")

    def stacked(parts, tail):
        return jnp.stack([a.reshape((n_ctx, SEQ) + tail) for a in parts], axis=1)

    return (xp.reshape(n_ctx, SEQ, D_MODEL), xs.reshape(n_lat, DEC_SEQ, D_MODEL),
            stacked(new_k, (NA_HEADS, HEAD_DIM)), stacked(new_v, (NA_HEADS, HEAD_DIM)),
            stacked(new_ckv, (MLA_KV_RANK,)), stacked(new_kr, (MLA_ROPE,)))
```

```python
import functools
import math

import jax
import jax.numpy as jnp
from jax import lax
from jax.experimental import pallas as pl
from jax.experimental.pallas import tpu as pltpu

F32 = jnp.float32
BF16 = jnp.bfloat16

D_MODEL = 1024
DEPTH = 4
SEQ = 256
DEC_SEQ = 2048
PAST_LEN = 256
GRID_W = 64
HEAD_DIM = 64
NA_WIDTH = 256
NA_HEADS = 4
NA_WIN_R = 8
NA_WIN_C = 16
MLA_HEADS = 8
MLA_NOPE = 64
MLA_ROPE = 32
MLA_V = 64
MLA_WIDTH = MLA_HEADS * MLA_V
MLA_Q_RANK = 384
MLA_KV_RANK = 256
SGU_WIDTH = 256
SGU_GROUPS = 4
SGU_CHUNK = 128
D_FF = 2816
ROPE_THETA = 10000.0
EPS = 1e-6
NEG_INF = -1e30
LOG2E = math.log2(math.e)
NA_QSCALE = HEAD_DIM ** -0.5 * LOG2E
MLA_QSCALE = (MLA_NOPE + MLA_ROPE) ** -0.5 * LOG2E

LANES = 128
BF16_ROWS = 16
HALF = LANES // 2

OFF_QA, OFF_KA, OFF_VA = 0, 256, 512
OFF_CQ = 768
OFF_CKV = OFF_CQ + MLA_Q_RANK
OFF_KR = OFF_CKV + MLA_KV_RANK
OFF_UV = OFF_KR + LANES
IN_COLS = OFF_UV + 2 * SGU_WIDTH
KV_COLS = MLA_HEADS * LANES + MLA_HEADS * MLA_V

FF_CHUNK = 256
N_FF_CHUNKS = D_FF // FF_CHUNK
FF_LB = 2 * FF_CHUNK // LANES

TM = 512
NA_QROWS = 4
NA_QBLK = NA_QROWS * GRID_W
NA_KBLKS = 3
VMEM_LIMIT = 56 * 1024 * 1024


def _cparams(n_axes):
    return pltpu.CompilerParams(dimension_semantics=("arbitrary",) * n_axes,
                                vmem_limit_bytes=VMEM_LIMIT)


def _layer_spec(l, *shape, single=False):
    mode = dict(pipeline_mode=pl.Buffered(1)) if single else {}
    return pl.BlockSpec((None,) + shape, lambda *_: (l,) + (0,) * len(shape), **mode)


def _rms(x, g):
    ms = jnp.mean(x * x, axis=-1, keepdims=True)
    return x * lax.rsqrt(ms + EPS) * g


def _dot(a, b):
    return jnp.dot(a, b, preferred_element_type=F32)


def _dot_nt(a, b):
    return lax.dot_general(a, b, (((1,), (1,)), ((), ())), preferred_element_type=F32)


def _half_mask(parity):
    lane = lax.broadcasted_iota(jnp.int32, (1, LANES), 1)
    return (lane // HALF) == parity


def _keep(mask, x):
    return jnp.where(mask, x, jnp.zeros_like(x))


def _with_ones(own, parity, v):
    lane = lax.broadcasted_iota(jnp.int32, (1, LANES), 1)
    ones = (lane == HALF * (1 - parity)).astype(v.dtype)
    return jnp.where(own, v, ones)


def _normalised(own, parity, parts):
    o = functools.reduce(jnp.add, parts)
    spare = HALF * (1 - parity)
    return jnp.where(own, o, 0.0) / o[:, spare:spare + 1]


def _mod_kernel(c_ref, w_ref, b_ref, o_ref):
    c = c_ref[...]
    s = c * jax.nn.sigmoid(c)
    o_ref[...] = _dot(s.astype(BF16), w_ref[...].astype(BF16)) + b_ref[...]


def _modulation(cond, w_mod, b_mod):
    n = 6
    wide = 2 * D_MODEL
    out = pl.pallas_call(
        _mod_kernel,
        grid=(DEPTH, n * D_MODEL // wide),
        in_specs=[
            pl.BlockSpec((8, D_MODEL), lambda l, j: (0, 0)),
            pl.BlockSpec((None, D_MODEL, wide), lambda l, j: (l, 0, j)),
            pl.BlockSpec((None, 1, wide), lambda l, j: (l, 0, j)),
        ],
        out_specs=pl.BlockSpec((None, 8, wide), lambda l, j: (l, 0, j)),
        out_shape=jax.ShapeDtypeStruct((DEPTH, 8, n * D_MODEL), F32),
        compiler_params=_cparams(2),
        name="modulation",
    )(cond, w_mod, b_mod.reshape(DEPTH, 1, n * D_MODEL))
    return out.reshape(DEPTH, 8, n, D_MODEL)


def _rope(x, cos, sin, lane_lo):
    up = pltpu.roll(x, LANES - 8, axis=1)
    dn = pltpu.roll(x, 8, axis=1)
    return x * cos + jnp.where(lane_lo, up, dn) * sin


def _proj_kernel(rope, emit_f32, x_ref, mod_ref, gmix_ref, win_ref, gcq_ref, wuq_ref, gckv_ref,
                 wukv_ref, gsgu_ref, wsgu_ref, bsg_ref, *refs):
    if rope:
        cos_ref, sin_ref = refs[:2]
        refs = refs[2:]
    qa_ref, ka_ref, va_ref, qm_ref, kp_ref, vp_ref, oc_ref = refs[:7]
    refs = refs[7:]
    if emit_f32:
        kaf_ref, vaf_ref, ckvf_ref, krf_ref = refs[:4]
        refs = refs[4:]
    hb_ref, z_ref = refs

    x = x_ref[...]
    h = _rms(x, gmix_ref[...]) * (1.0 + mod_ref[1:2, :]) + mod_ref[0:1, :]
    hb_ref[...] = h.astype(BF16)
    tm = x.shape[0]
    z_ref[:, OFF_UV:] = _dot_nt(hb_ref[...], win_ref[OFF_UV:, :])
    z_ref[:, OFF_CQ:OFF_UV] = _dot_nt(hb_ref[...], win_ref[OFF_CQ:OFF_UV, :])
    z_ref[:, :OFF_CQ] = _dot_nt(hb_ref[...], win_ref[:OFF_CQ, :])
    lane = lax.broadcasted_iota(jnp.int32, (1, LANES), 1)
    lane_lo = (lane % 16) < 8
    if rope:
        cos = cos_ref[...]
        sin = sin_ref[...]

    uv = jax.nn.gelu(z_ref[:, OFF_UV:OFF_UV + 2 * SGU_WIDTH])
    u = uv[:, :SGU_WIDTH]
    vn = _rms(uv[:, SGU_WIDTH:], gsgu_ref[...])
    even = _half_mask(0)
    for ch in range(tm // SGU_CHUNK):
        rows = slice(ch * SGU_CHUNK, (ch + 1) * SGU_CHUNK)
        for p in range(SGU_GROUPS // 2):
            sl = slice(p * LANES, (p + 1) * LANES)
            vc = vn[rows, sl]
            mixed = (_dot(wsgu_ref[2 * p], jnp.where(even, vc, 0.0).astype(BF16))
                     + _dot(wsgu_ref[2 * p + 1], jnp.where(even, 0.0, vc).astype(BF16))
                     + bsg_ref[:, sl])
            oc_ref[rows, sl] = (u[rows, sl] * mixed).astype(BF16)

    cq = z_ref[:, OFF_CQ:OFF_CQ + MLA_Q_RANK]
    cqn = _rms(cq, gcq_ref[...]).astype(BF16)
    qm = _dot(cqn, wuq_ref[...])
    for hd in range(MLA_HEADS):
        qh = qm[:, hd * LANES:(hd + 1) * LANES]
        if rope:
            qh = _rope(qh, cos, sin, lane_lo)
        qm_ref[hd] = (qh * MLA_QSCALE).astype(BF16)

    ckv = z_ref[:, OFF_CKV:OFF_CKV + MLA_KV_RANK]
    ckvn = _rms(ckv, gckv_ref[...])
    kr = z_ref[:, OFF_KR:OFF_KR + LANES]
    if emit_f32:
        ckvf_ref[...] = ckvn
        krf_ref[...] = kr
    if rope:
        kr = _rope(kr, cos, sin, lane_lo)
    kv = _dot(ckvn.astype(BF16), wukv_ref[...])
    for hd in range(MLA_HEADS):
        kp_ref[hd] = (kv[:, hd * LANES:(hd + 1) * LANES] + kr).astype(BF16)
    voff = MLA_HEADS * LANES
    for p in range(MLA_HEADS // 2):
        vp_ref[p] = kv[:, voff + p * LANES:voff + (p + 1) * LANES].astype(BF16)

    qa = z_ref[:, OFF_QA:OFF_QA + NA_WIDTH] * NA_QSCALE
    ka = z_ref[:, OFF_KA:OFF_KA + NA_WIDTH]
    va = z_ref[:, OFF_VA:OFF_VA + NA_WIDTH]
    for p in range(NA_HEADS // 2):
        sl = slice(p * LANES, (p + 1) * LANES)
        qa_ref[p] = qa[:, sl].astype(BF16)
        ka_ref[p] = ka[:, sl].astype(BF16)
        va_ref[p] = va[:, sl].astype(BF16)
    if emit_f32:
        for src, dst in ((ka, kaf_ref), (va, vaf_ref)):
            src_t = src.T
            for sq in range(tm // SEQ):
                for hd in range(NA_HEADS):
                    dst[sq, hd] = src_t[hd * HEAD_DIM:(hd + 1) * HEAD_DIM, sq * SEQ:(sq + 1) * SEQ]


def _project(x, mods, mod_row, pw, l, rope_tabs, emit_f32, tiles_per_seq):
    t = x.shape[0]
    nt = t // TM
    rope = rope_tabs is not None
    in_specs = [
        pl.BlockSpec((TM, D_MODEL), lambda i: (i, 0)),
        pl.BlockSpec((None, 6, D_MODEL), lambda i: (mod_row(i), 0, 0)),
        _layer_spec(l, 1, D_MODEL),
        _layer_spec(l, IN_COLS, D_MODEL),
        _layer_spec(l, 1, MLA_Q_RANK),
        _layer_spec(l, MLA_Q_RANK, MLA_HEADS * LANES),
        _layer_spec(l, 1, MLA_KV_RANK),
        _layer_spec(l, MLA_KV_RANK, KV_COLS),
        _layer_spec(l, 1, SGU_WIDTH),
        _layer_spec(l, SGU_GROUPS, SGU_CHUNK, SGU_CHUNK),
        _layer_spec(l, SGU_CHUNK, SGU_WIDTH),
    ]
    args = [x, mods, pw["g_mix"], pw["w_in"], pw["g_cq"], pw["w_uq"], pw["g_ckv"], pw["w_ukv"],
            pw["g_sgu"], pw["w_sgu"], pw["b_sgu"]]
    if rope:
        tab_spec = pl.BlockSpec((TM, LANES), lambda i: (i % tiles_per_seq, 0))
        in_specs += [tab_spec, tab_spec]
        args += list(rope_tabs)

    def heads(n):
        return (pl.BlockSpec((n, TM, LANES), lambda i: (0, i, 0)),
                jax.ShapeDtypeStruct((n, t, LANES), BF16))

    def flat(w, dt):
        return (pl.BlockSpec((TM, w), lambda i: (i, 0)), jax.ShapeDtypeStruct((t, w), dt))

    outs = [heads(2), heads(2), heads(2), heads(MLA_HEADS), heads(MLA_HEADS), heads(MLA_HEADS // 2),
            flat(SGU_WIDTH, BF16)]
    if emit_f32:
        per_head = (pl.BlockSpec((TM // SEQ, NA_HEADS, HEAD_DIM, SEQ), lambda i: (i, 0, 0, 0)),
                    jax.ShapeDtypeStruct((t // SEQ, NA_HEADS, HEAD_DIM, SEQ), F32))
        outs += [per_head, per_head, flat(MLA_KV_RANK, F32), flat(LANES, F32)]
    return pl.pallas_call(
        functools.partial(_proj_kernel, rope, emit_f32),
        grid=(nt,),
        in_specs=in_specs,
        out_specs=[o[0] for o in outs],
        out_shape=[o[1] for o in outs],
        scratch_shapes=[pltpu.VMEM((TM, D_MODEL), BF16), pltpu.VMEM((TM, IN_COLS), F32)],
        compiler_params=_cparams(1),
        name="project_lat" if rope else "project_ctx",
    )(*args)


def _cache_kv_kernel(ckv_ref, kr_ref, wukv_ref, kp_ref, vp_ref):
    kv = _dot(ckv_ref[...].astype(BF16), wukv_ref[...])
    kr = kr_ref[...]
    for hd in range(MLA_HEADS):
        kp_ref[hd] = (kv[:, hd * LANES:(hd + 1) * LANES] + kr).astype(BF16)
    voff = MLA_HEADS * LANES
    for p in range(MLA_HEADS // 2):
        vp_ref[p] = kv[:, voff + p * LANES:voff + (p + 1) * LANES].astype(BF16)


def _cache_kv(cache_ckv, cache_kr_pad, w_ukv):
    b = cache_ckv.shape[0]
    return pl.pallas_call(
        _cache_kv_kernel,
        grid=(DEPTH, b),
        in_specs=[
            pl.BlockSpec((None, None, PAST_LEN, MLA_KV_RANK), lambda l, i: (i, l, 0, 0)),
            pl.BlockSpec((None, None, PAST_LEN, LANES), lambda l, i: (i, l, 0, 0)),
            pl.BlockSpec((None, MLA_KV_RANK, KV_COLS), lambda l, i: (l, 0, 0)),
        ],
        out_specs=[
            pl.BlockSpec((None, MLA_HEADS, PAST_LEN, LANES), lambda l, i: (l, 0, i, 0)),
            pl.BlockSpec((None, MLA_HEADS // 2, PAST_LEN, LANES), lambda l, i: (l, 0, i, 0)),
        ],
        out_shape=[
            jax.ShapeDtypeStruct((DEPTH, MLA_HEADS, b * PAST_LEN, LANES), BF16),
            jax.ShapeDtypeStruct((DEPTH, MLA_HEADS // 2, b * PAST_LEN, LANES), BF16),
        ],
        compiler_params=_cparams(2),
        name="cache_kv",
    )(cache_ckv, cache_kr_pad, w_ukv)


def _ctx_attn_kernel(qa_ref, ka_ref, va_ref, qm_ref, kp_ref, vp_ref, o_ref):
    for p in range(NA_HEADS // 2 + MLA_HEADS // 2):
        out = None
        for half in range(2):
            own = _half_mask(half)
            if p < NA_HEADS // 2:
                q = _keep(own, qa_ref[p])
                k = ka_ref[p]
                v = va_ref[p]
            else:
                hd = 2 * (p - NA_HEADS // 2) + half
                q = qm_ref[hd]
                k = kp_ref[hd]
                v = vp_ref[hd // 2]
            s = _dot_nt(q, k)
            p_ = jnp.exp2(s - jnp.max(s, axis=-1, keepdims=True))
            o = _dot(p_.astype(BF16), _keep(own, v)) / jnp.sum(p_, axis=-1, keepdims=True)
            out = o if out is None else out + o
        o_ref[:, p * LANES:(p + 1) * LANES] = out.astype(BF16)


def _ctx_attention(qa, ka, va, qm, kp, vp, n_seq):
    heads = lambda n: pl.BlockSpec((n, SEQ, LANES), lambda b: (0, b, 0))
    width = NA_WIDTH + MLA_WIDTH
    return pl.pallas_call(
        _ctx_attn_kernel,
        grid=(n_seq,),
        in_specs=[heads(2), heads(2), heads(2), heads(MLA_HEADS), heads(MLA_HEADS), heads(MLA_HEADS // 2)],
        out_specs=pl.BlockSpec((SEQ, width), lambda b: (b, 0)),
        out_shape=jax.ShapeDtypeStruct((n_seq * SEQ, width), BF16),
        compiler_params=_cparams(1),
        name="ctx_attention",
    )(qa, ka, va, qm, kp, vp)


def _lat_mla_kernel(q_ref, kc_ref, kl_ref, vc_ref, vl_ref, o_ref, s_ref, m_ref, p_ref):
    nc = kc_ref.shape[1]
    outs = {}

    def scores(hd):
        q = q_ref[hd]
        sc = _dot_nt(q, kc_ref[hd])
        sl = _dot_nt(q, kl_ref[hd])
        s_ref[hd % 2, :, 0:nc] = sc
        s_ref[hd % 2, :, nc:] = sl
        m_ref[hd % 2] = jnp.maximum(_lane_block_max(sc), _lane_block_max(sl))

    def softmax(hd):
        m = jnp.max(m_ref[hd % 2], axis=-1, keepdims=True)
        p_ref[hd % 2] = jnp.exp2(s_ref[hd % 2] - m).astype(BF16)

    def values(hd):
        own = _half_mask(hd % 2)
        o = _normalised(own, hd % 2, [_dot(p_ref[hd % 2, :, 0:nc], _with_ones(own, hd % 2, vc_ref[hd // 2])),
                                      _dot(p_ref[hd % 2, :, nc:], _with_ones(own, hd % 2, vl_ref[hd // 2]))])
        if hd % 2 == 0:
            outs[hd // 2] = o
        else:
            o_ref[:, (hd // 2) * LANES:(hd // 2 + 1) * LANES] = (outs.pop(hd // 2) + o).astype(BF16)

    for t in range(MLA_HEADS + 2):
        if t < MLA_HEADS:
            scores(t)
        if 0 <= t - 1 < MLA_HEADS:
            softmax(t - 1)
        if 0 <= t - 2 < MLA_HEADS:
            values(t - 2)


def _lat_mla(q, kc, vc, kl, vl, l, n_batch):
    nq = DEC_SEQ // TM
    n_keys = PAST_LEN + DEC_SEQ
    return pl.pallas_call(
        _lat_mla_kernel,
        grid=(n_batch, nq),
        in_specs=[pl.BlockSpec((MLA_HEADS, TM, LANES), lambda b, i: (0, b * nq + i, 0)),
                  pl.BlockSpec((None, MLA_HEADS, PAST_LEN, LANES), lambda b, i: (l, 0, b, 0)),
                  pl.BlockSpec((MLA_HEADS, DEC_SEQ, LANES), lambda b, i: (0, b, 0)),
                  pl.BlockSpec((None, MLA_HEADS // 2, PAST_LEN, LANES), lambda b, i: (l, 0, b, 0)),
                  pl.BlockSpec((MLA_HEADS // 2, DEC_SEQ, LANES), lambda b, i: (0, b, 0))],
        out_specs=pl.BlockSpec((TM, MLA_WIDTH), lambda b, i: (b * nq + i, 0)),
        out_shape=jax.ShapeDtypeStruct((n_batch * DEC_SEQ, MLA_WIDTH), BF16),
        scratch_shapes=[pltpu.VMEM((2, TM, n_keys), F32), pltpu.VMEM((2, TM, LANES), F32),
                        pltpu.VMEM((2, TM, n_keys), BF16)],
        compiler_params=_cparams(2),
        name="lat_mla",
    )(q, kc, kl, vc, vl)


def _na_bias_kernel(rpb_ref, o_ref, tab_ref):
    l = pl.program_id(0)
    hd = pl.program_id(1)
    n_dr = 2 * NA_WIN_R - 1
    n_dc = 2 * NA_WIN_C - 1
    base = (l * NA_HEADS + hd) * n_dr * n_dc
    qc = lax.broadcasted_iota(jnp.int32, (GRID_W, LANES), 0)
    kc = lax.broadcasted_iota(jnp.int32, (GRID_W, LANES), 1) % GRID_W
    diff = kc - qc
    cs = jnp.clip(qc - NA_WIN_C // 2, 0, GRID_W - NA_WIN_C)
    in_win = (kc >= cs) & (kc < cs + NA_WIN_C)
    neg = jnp.full((GRID_W, LANES), NEG_INF * LOG2E, F32)
    for dr in range(n_dr):
        acc = neg
        for dc in range(n_dc):
            acc = jnp.where(diff == dc - (NA_WIN_C - 1), rpb_ref[base + dr * n_dc + dc] * LOG2E, acc)
        tab_ref[dr] = jnp.where(in_win, acc, neg)
    left = lax.broadcasted_iota(jnp.int32, (GRID_W, LANES), 1) < GRID_W
    n_krows = NA_KBLKS * NA_QBLK // GRID_W
    cases = ((0, lambda qr: 0), (-NA_QROWS, lambda qr: qr), (-2 * NA_QROWS, lambda qr: NA_QROWS))
    for c, (shift, first) in enumerate(cases):
        for qr in range(NA_QROWS):
            for kp in range(n_krows // 2):
                tiles = []
                for kr in (2 * kp, 2 * kp + 1):
                    ok = first(qr) <= kr < first(qr) + NA_WIN_R
                    tiles.append(tab_ref[kr - qr + shift + NA_WIN_R - 1] if ok else neg)
                o_ref[c, qr * GRID_W:(qr + 1) * GRID_W, kp * LANES:(kp + 1) * LANES] = (
                    jnp.where(left, tiles[0], tiles[1]))


def _na_bias(na_rpb):
    n_k = NA_KBLKS * NA_QBLK
    return pl.pallas_call(
        _na_bias_kernel,
        grid=(DEPTH, NA_HEADS),
        in_specs=[pl.BlockSpec(memory_space=pltpu.SMEM)],
        out_specs=pl.BlockSpec((None, 3, None, NA_QBLK, n_k), lambda l, h: (l, 0, h, 0, 0)),
        out_shape=jax.ShapeDtypeStruct((DEPTH, 3, NA_HEADS, NA_QBLK, n_k), F32),
        scratch_shapes=[pltpu.VMEM((2 * NA_WIN_R - 1, GRID_W, LANES), F32)],
        compiler_params=_cparams(2),
        name="na_bias",
    )(na_rpb.reshape(-1))


def _lane_block_max(s):
    return functools.reduce(jnp.maximum, [s[:, j * LANES:(j + 1) * LANES] for j in range(s.shape[1] // LANES)])


def _na_kernel(q_ref, k0_ref, k1_ref, k2_ref, v0_ref, v1_ref, v2_ref, kc_ref, vc_ref, bias_ref, o_ref,
               s_ref, m_ref, p_ref):
    k_refs = (k0_ref, k1_ref, k2_ref)
    v_refs = (v0_ref, v1_ref, v2_ref)
    outs = {}
    n_loc = NA_KBLKS * NA_QBLK

    def scores(hd):
        p = hd // 2
        q = _keep(_half_mask(hd % 2), q_ref[p])
        m = None
        for i in range(NA_KBLKS):
            s = _dot_nt(q, k_refs[i][p]) + bias_ref[hd, :, i * NA_QBLK:(i + 1) * NA_QBLK]
            s_ref[hd % 2, :, i * NA_QBLK:(i + 1) * NA_QBLK] = s
            m = _lane_block_max(s) if m is None else jnp.maximum(m, _lane_block_max(s))
        s = _dot(q, kc_ref[p * LANES:(p + 1) * LANES, :].astype(BF16))
        s_ref[hd % 2, :, n_loc:] = s
        m_ref[hd % 2] = jnp.maximum(m, _lane_block_max(s))

    def softmax(hd):
        m = jnp.max(m_ref[hd % 2], axis=-1, keepdims=True)
        p_ref[hd % 2] = jnp.exp2(s_ref[hd % 2] - m).astype(BF16)

    def values(hd):
        p = hd // 2
        own = _half_mask(hd % 2)
        row = lax.broadcasted_iota(jnp.int32, (LANES, 1), 0)
        vc = vc_ref[p * LANES:(p + 1) * LANES, :].astype(BF16)
        vc = jnp.where((row // HALF) == hd % 2, vc, (row == HALF * (1 - hd % 2)).astype(BF16))
        parts = [_dot_nt(p_ref[hd % 2, :, n_loc:], vc)]
        for i in range(NA_KBLKS):
            parts.append(_dot(p_ref[hd % 2, :, i * NA_QBLK:(i + 1) * NA_QBLK],
                              _with_ones(own, hd % 2, v_refs[i][p])))
        o = _normalised(own, hd % 2, parts)
        if hd % 2 == 0:
            outs[p] = o
        else:
            o_ref[:, p * LANES:(p + 1) * LANES] = (outs.pop(p) + o).astype(BF16)

    for t in range(NA_HEADS + 2):
        if t < NA_HEADS:
            scores(t)
        if 0 <= t - 1 < NA_HEADS:
            softmax(t - 1)
        if 0 <= t - 2 < NA_HEADS:
            values(t - 2)


def _na_attention(qa, ka, va, cache_k, cache_v, bias, l, n_batch):
    nblk = DEC_SEQ // NA_QBLK
    max_start = nblk - NA_KBLKS

    def kspec(i):
        return pl.BlockSpec((NA_HEADS // 2, NA_QBLK, LANES),
                            lambda j, b: (0, b * nblk + jnp.clip(j - 1, 0, max_start) + i, 0))

    case = lambda j: jnp.where(j == 0, 0, jnp.where(j == nblk - 1, 2, 1))
    cache_spec = pl.BlockSpec((None, None, NA_WIDTH, PAST_LEN), lambda j, b: (b, l, 0, 0))
    return pl.pallas_call(
        _na_kernel,
        grid=(nblk, n_batch),
        in_specs=[pl.BlockSpec((NA_HEADS // 2, NA_QBLK, LANES), lambda j, b: (0, b * nblk + j, 0)),
                  kspec(0), kspec(1), kspec(2), kspec(0), kspec(1), kspec(2),
                  cache_spec, cache_spec,
                  pl.BlockSpec((None, None, NA_HEADS, NA_QBLK, NA_KBLKS * NA_QBLK),
                               lambda j, b: (l, case(j), 0, 0, 0))],
        out_specs=pl.BlockSpec((NA_QBLK, NA_WIDTH), lambda j, b: (b * nblk + j, 0)),
        out_shape=jax.ShapeDtypeStruct((n_batch * DEC_SEQ, NA_WIDTH), BF16),
        scratch_shapes=[pltpu.VMEM((2, NA_QBLK, (NA_KBLKS + 1) * NA_QBLK), F32),
                        pltpu.VMEM((2, NA_QBLK, LANES), F32),
                        pltpu.VMEM((2, NA_QBLK, (NA_KBLKS + 1) * NA_QBLK), BF16)],
        compiler_params=_cparams(2),
        name="na_attention",
    )(qa, ka, ka, ka, va, va, va, cache_k, cache_v, bias)


def _ffn_kernel(seq_len, final, widths, *refs):
    halo = seq_len > TM
    n_in = 3 if halo else 1
    x_refs = refs[:n_in]
    refs = refs[n_in:]
    part_refs = [refs[i * n_in:(i + 1) * n_in] for i in range(len(widths))]
    refs = refs[n_in * len(widths):]
    mod_ref, gffn_ref, wo_ref, win_ref, cw_ref, cb_ref, wout_ref = refs[:7]
    refs = refs[7:]
    if final:
        gfin_ref = refs[0]
        refs = refs[1:]
    o_ref, mix_ref, hext_ref, a_ref, act_ref = refs
    tm = o_ref.shape[0]
    pad = BF16_ROWS
    if halo:
        tiles_per_seq = seq_len // tm
        pos = pl.program_id(0) % tiles_per_seq
        pieces = [(0, 0, slice(0, pad), pos != 0), (pad, 1, slice(0, tm), None),
                  (pad + tm, 2, slice(0, pad), pos != tiles_per_seq - 1)]
        frames = []
        starts = [pad]
        seg_len = tm
    else:
        nseg = tm // seq_len
        starts = [pad + s * (seq_len + pad) for s in range(nseg)]
        seg_len = seq_len
        pieces = [(starts[s], 0, slice(s * seq_len, (s + 1) * seq_len), None) for s in range(nseg)]
        frames = [s * (seq_len + pad) for s in range(nseg + 1)]
    for f in frames:
        mix_ref[f:f + pad] = jnp.zeros((pad, D_MODEL), BF16)
        hext_ref[f:f + pad] = jnp.zeros((pad, D_MODEL), BF16)
    off = 0
    for n, prefs in zip(widths, part_refs):
        for row, src, rows, _ in pieces:
            mix_ref[row:row + rows.stop - rows.start, off:off + n] = prefs[src][rows, :]
        off += n
    r = _dot(mix_ref[...], wo_ref[...])
    g1 = mod_ref[2:3, :]
    scale = gffn_ref[...] * (1.0 + mod_ref[4:5, :])
    shift = mod_ref[3:4, :]
    for row, src, rows, keep in pieces:
        n_rows = rows.stop - rows.start
        x1 = x_refs[src][rows, :] + g1 * r[row:row + n_rows]
        if src == n_in // 2:
            o_ref[rows, :] = x1
        h2 = (x1 * lax.rsqrt(jnp.mean(x1 * x1, axis=-1, keepdims=True) + EPS) * scale + shift).astype(BF16)
        if keep is not None:
            h2 = jnp.where(keep, h2, jnp.zeros_like(h2))
        hext_ref[row:row + n_rows] = h2

    def col(c, lb):
        half = FF_LB // 2
        return (lb // half) * D_FF + c * FF_CHUNK + (lb % half) * LANES

    def up(c):
        hext = hext_ref[...]
        for part in range(2):
            a = _dot(hext, win_ref[:, part * D_FF + c * FF_CHUNK:part * D_FF + (c + 1) * FF_CHUNK])
            for j in range(FF_LB // 2):
                a_ref[c % 2, part * (FF_LB // 2) + j] = a[:, j * LANES:(j + 1) * LANES]

    def conv(c, lb, st):
        cs = slice(col(c, lb), col(c, lb) + LANES)
        return (a_ref[c % 2, lb, pl.ds(st - 1, seg_len, stride=1), :] * cw_ref[0:1, cs]
                + a_ref[c % 2, lb, st:st + seg_len, :] * cw_ref[1:2, cs]
                + a_ref[c % 2, lb, pl.ds(st + 1, seg_len, stride=1), :] * cw_ref[2:3, cs] + cb_ref[:, cs])

    up(0)
    for c in range(N_FF_CHUNKS):
        if c + 1 < N_FF_CHUNKS:
            up(c + 1)
        for s, st in enumerate(starts):
            for lb in range(FF_LB // 2):
                gate = conv(c, lb, st)
                val = conv(c, lb + FF_LB // 2, st)
                act_ref[s * seg_len:(s + 1) * seg_len, c * FF_CHUNK + lb * LANES:c * FF_CHUNK + (lb + 1) * LANES] = (
                    gate * jax.nn.sigmoid(gate) * val).astype(BF16)
    y = o_ref[...] + mod_ref[5:6, :] * _dot(act_ref[...], wout_ref[...])
    if final:
        y = _rms(y, gfin_ref[...])
    o_ref[...] = y


def _ffn(x, parts, mods, mod_row, pw, l, seq_len, g_final, name):
    t = x.shape[0]
    nt = t // TM
    hb = TM // BF16_ROWS
    n_hblk = t // BF16_ROWS
    final = g_final is not None
    halo = seq_len > TM
    ext = TM + 2 * BF16_ROWS if halo else TM + (TM // seq_len + 1) * BF16_ROWS
    widths = tuple(p.shape[1] for p in parts)

    def tiled(a):
        w = a.shape[1]
        main = pl.BlockSpec((TM, w), lambda i: (i, 0))
        if not halo:
            return [main], [a]
        prev = pl.BlockSpec((BF16_ROWS, w), lambda i: (jnp.maximum(i * hb - 1, 0), 0))
        nxt = pl.BlockSpec((BF16_ROWS, w), lambda i: (jnp.minimum((i + 1) * hb, n_hblk - 1), 0))
        return [prev, main, nxt], [a, a, a]

    in_specs, args = [], []
    for a in (x,) + tuple(parts):
        sp, ar = tiled(a)
        in_specs += sp
        args += ar
    in_specs += [pl.BlockSpec((None, 6, D_MODEL), lambda i: (mod_row(i), 0, 0)),
                 _layer_spec(l, 1, D_MODEL),
                 _layer_spec(l, D_MODEL, D_MODEL, single=True),
                 _layer_spec(l, D_MODEL, 2 * D_FF, single=True),
                 _layer_spec(l, 3, 2 * D_FF, single=True),
                 _layer_spec(l, 1, 2 * D_FF, single=True),
                 _layer_spec(l, D_FF, D_MODEL, single=True)]
    args += [mods, pw["g_ffn"], pw["w_out"], pw["w_ffn_in"], pw["conv_w"], pw["conv_b"], pw["w_ffn_out"]]
    if final:
        in_specs.append(pl.BlockSpec((1, D_MODEL), lambda i: (0, 0)))
        args.append(g_final)
    return pl.pallas_call(
        functools.partial(_ffn_kernel, seq_len, final, widths),
        grid=(nt,),
        in_specs=in_specs,
        out_specs=pl.BlockSpec((TM, D_MODEL), lambda i: (i, 0)),
        out_shape=jax.ShapeDtypeStruct((t, D_MODEL), F32),
        scratch_shapes=[pltpu.VMEM((ext, D_MODEL), BF16), pltpu.VMEM((ext, D_MODEL), BF16),
                        pltpu.VMEM((2, FF_LB, ext, LANES), F32), pltpu.VMEM((TM, D_FF), BF16)],
        compiler_params=_cparams(1),
        name=name,
    )(*args)


def _pack_mla_kernel(wuq_ref, wukv_ref, ouq_ref, oukv_ref):
    dq = MLA_NOPE + MLA_ROPE
    zq = jnp.zeros((MLA_Q_RANK, LANES - dq), F32)
    zk = jnp.zeros((MLA_KV_RANK, LANES - MLA_NOPE), F32)
    dkv = MLA_NOPE + MLA_V
    for hd in range(MLA_HEADS):
        ouq_ref[:, hd * LANES:(hd + 1) * LANES] = jnp.concatenate(
            [wuq_ref[:, hd * dq:(hd + 1) * dq], zq], axis=-1).astype(BF16)
        oukv_ref[:, hd * LANES:(hd + 1) * LANES] = jnp.concatenate(
            [wukv_ref[:, hd * dkv:hd * dkv + MLA_NOPE], zk], axis=-1).astype(BF16)
    voff = MLA_HEADS * LANES
    for p in range(MLA_HEADS // 2):
        oukv_ref[:, voff + p * LANES:voff + (p + 1) * LANES] = jnp.concatenate(
            [wukv_ref[:, (2 * p + j) * dkv + MLA_NOPE:(2 * p + j + 1) * dkv] for j in range(2)], axis=-1).astype(BF16)


def _pack_weights(w_in, w_uq, w_ukv, w_sgu, b_sgu, w_out, w_ffn_in, ffn_conv_w, ffn_conv_b, w_ffn_out,
                  g_mix, g_cq, g_ckv, g_sgu, g_ffn):
    nl = w_in.shape[0]
    wt = jnp.swapaxes(w_in, 1, 2).astype(BF16)
    zr = lambda n: jnp.zeros((nl, n, D_MODEL), BF16)
    w_in_p = jnp.concatenate([wt[:, :OFF_KR], zr(HALF), wt[:, OFF_KR:OFF_KR + MLA_ROPE],
                              zr(LANES - HALF - MLA_ROPE), wt[:, OFF_KR + MLA_ROPE:]], axis=1)
    w_uq_p, w_ukv_p = pl.pallas_call(
        _pack_mla_kernel,
        grid=(nl,),
        in_specs=[pl.BlockSpec((None,) + w_uq.shape[1:], lambda l: (l, 0, 0)),
                  pl.BlockSpec((None,) + w_ukv.shape[1:], lambda l: (l, 0, 0))],
        out_specs=[pl.BlockSpec((None, MLA_Q_RANK, MLA_HEADS * LANES), lambda l: (l, 0, 0)),
                   pl.BlockSpec((None, MLA_KV_RANK, KV_COLS), lambda l: (l, 0, 0))],
        out_shape=[jax.ShapeDtypeStruct((nl, MLA_Q_RANK, MLA_HEADS * LANES), BF16),
                   jax.ShapeDtypeStruct((nl, MLA_KV_RANK, KV_COLS), BF16)],
        compiler_params=_cparams(1),
        name="pack_w_mla",
    )(w_uq, w_ukv)
    b_sgu_p = jnp.repeat(jnp.swapaxes(b_sgu, 1, 2), SGU_WIDTH // SGU_GROUPS, axis=-1)
    return dict(
        w_in=w_in_p, w_uq=w_uq_p, w_ukv=w_ukv_p, w_sgu=w_sgu.astype(BF16), b_sgu=b_sgu_p,
        w_out=w_out.astype(BF16), w_ffn_in=w_ffn_in.astype(BF16), conv_w=ffn_conv_w,
        conv_b=ffn_conv_b[:, None, :], w_ffn_out=w_ffn_out.astype(BF16),
        g_mix=g_mix[:, None, :], g_cq=g_cq[:, None, :], g_ckv=g_ckv[:, None, :], g_sgu=g_sgu[:, None, :],
        g_ffn=g_ffn[:, None, :])


def _rope_tables(n_tokens):
    t = jnp.arange(n_tokens)
    n_freq = MLA_ROPE // 4
    inv_freq = ROPE_THETA ** (-jnp.arange(n_freq, dtype=F32) / n_freq)
    ang_r = (t // GRID_W).astype(F32)[:, None] * inv_freq
    ang_c = (t % GRID_W).astype(F32)[:, None] * inv_freq
    ones = jnp.ones((n_tokens, HALF), F32)
    tail = LANES - HALF - MLA_ROPE
    cos = jnp.concatenate([ones, jnp.cos(ang_r), jnp.cos(ang_r), jnp.cos(ang_c), jnp.cos(ang_c),
                           ones[:, :tail]], axis=-1)
    sin = jnp.concatenate([0 * ones, -jnp.sin(ang_r), jnp.sin(ang_r), -jnp.sin(ang_c), jnp.sin(ang_c),
                           0 * ones[:, :tail]], axis=-1)
    return cos, sin


def kernel(x_prompt, x_sample, cache_na_k, cache_na_v, cache_mla_ckv, cache_mla_krope, c, c_ctx, w_mod, b_mod,
           g_mix, w_in, na_rpb, g_cq, w_uq, g_ckv, w_ukv, g_sgu, w_sgu, b_sgu, w_out, g_ffn, w_ffn_in,
           ffn_conv_w, ffn_conv_b, w_ffn_out, g_final):
    n_ctx, n_lat = x_prompt.shape[0], x_sample.shape[0]
    t_ctx, t_lat = n_ctx * SEQ, n_lat * DEC_SEQ
    pw = _pack_weights(w_in, w_uq, w_ukv, w_sgu, b_sgu, w_out, w_ffn_in, ffn_conv_w, ffn_conv_b,
                       w_ffn_out, g_mix, g_cq, g_ckv, g_sgu, g_ffn)
    g_fin = g_final[None, :]

    cond = jnp.concatenate([c_ctx[None, :], c, jnp.zeros((8 - 1 - n_lat, D_MODEL), F32)], axis=0)
    mods = _modulation(cond, w_mod, b_mod)
    rope_tabs = _rope_tables(DEC_SEQ)
    na_bias = _na_bias(na_rpb)
    kr_pad = jnp.pad(cache_mla_krope, ((0, 0), (0, 0), (0, 0), (HALF, LANES - HALF - MLA_ROPE)))
    cache_kp, cache_vp = _cache_kv(cache_mla_ckv, kr_pad, pw["w_ukv"])
    channel_major = lambda a: jnp.transpose(a, (0, 1, 3, 4, 2)).reshape(n_lat, DEPTH, NA_WIDTH, PAST_LEN)
    cache_k = channel_major(cache_na_k)
    cache_v = channel_major(cache_na_v)

    lat_tiles = DEC_SEQ // TM
    ctx_row = lambda i: 0
    lat_row = lambda i: 1 + i // lat_tiles

    xp = x_prompt.reshape(t_ctx, D_MODEL)
    xs = x_sample.reshape(t_lat, D_MODEL)
    new_k, new_v, new_ckv, new_kr = [], [], [], []
    for l in range(DEPTH):
        last = l == DEPTH - 1
        m = mods[l]
        qa, ka, va, qm, kp, vp, oc, ka_f, va_f, ckv_f, kr_f = _project(xp, m, ctx_row, pw, l, None, True, 1)
        new_k.append(ka_f)
        new_v.append(va_f)
        new_ckv.append(ckv_f)
        new_kr.append(kr_f[:, HALF:HALF + MLA_ROPE])
        o_ab = _ctx_attention(qa, ka, va, qm, kp, vp, n_ctx)
        xp = _ffn(xp, (o_ab, oc), m, ctx_row, pw, l, SEQ, g_fin if last else None, "ctx_ffn")

        qa, ka, va, qm, kp, vp, oc = _project(xs, m, lat_row, pw, l, rope_tabs, False, lat_tiles)
        o_a = _na_attention(qa, ka, va, cache_k, cache_v, na_bias, l, n_lat)
        o_b = _lat_mla(qm, cache_kp, cache_vp, kp, vp, l, n_lat)
        xs = _ffn(xs, (o_a, o_b, oc), m, lat_row, pw, l, DEC_SEQ, g_fin if last else None, "lat_ffn")

    def stacked(parts, tail):
        return jnp.stack([a.reshape((n_ctx, SEQ) + tail) for a in parts], axis=1)

    def stacked_t(parts):
        return jnp.transpose(jnp.stack(parts, axis=1), (0, 1, 4, 2, 3))

    return (xp.reshape(n_ctx, SEQ, D_MODEL), xs.reshape(n_lat, DEC_SEQ, D_MODEL),
            stacked_t(new_k), stacked_t(new_v),
            stacked(new_ckv, (MLA_KV_RANK,)), stacked(new_kr, (MLA_ROPE,)))
```

```python
import functools
import math

import jax
import jax.numpy as jnp
from jax import lax
from jax.experimental import pallas as pl
from jax.experimental.pallas import tpu as pltpu

F32 = jnp.float32
BF16 = jnp.bfloat16

D_MODEL = 1024
DEPTH = 4
SEQ = 256
DEC_SEQ = 2048
PAST_LEN = 256
GRID_W = 64
HEAD_DIM = 64
NA_WIDTH = 256
NA_HEADS = 4
NA_WIN_R = 8
NA_WIN_C = 16
MLA_HEADS = 8
MLA_NOPE = 64
MLA_ROPE = 32
MLA_V = 64
MLA_WIDTH = MLA_HEADS * MLA_V
MLA_Q_RANK = 384
MLA_KV_RANK = 256
SGU_WIDTH = 256
SGU_GROUPS = 4
SGU_CHUNK = 128
D_FF = 2816
ROPE_THETA = 10000.0
EPS = 1e-6
NEG_INF = -1e30
LOG2E = math.log2(math.e)
NA_QSCALE = HEAD_DIM ** -0.5 * LOG2E
MLA_QSCALE = (MLA_NOPE + MLA_ROPE) ** -0.5 * LOG2E

LANES = 128
BF16_ROWS = 16
HALF = LANES // 2

OFF_QA, OFF_KA, OFF_VA = 0, 256, 512
OFF_CQ = 768
OFF_CKV = OFF_CQ + MLA_Q_RANK
OFF_KR = OFF_CKV + MLA_KV_RANK
OFF_UV = OFF_KR + LANES
IN_COLS = OFF_UV + 2 * SGU_WIDTH
KV_COLS = MLA_HEADS * LANES + MLA_HEADS * MLA_V

FF_CHUNK = 256
N_FF_CHUNKS = D_FF // FF_CHUNK
FF_LB = 2 * FF_CHUNK // LANES

TM = 512
PROJ_TM = 1024
CTX_SEQS = 4
NA_SUBS = 4
NA_QROWS = 4
NA_QBLK = NA_QROWS * GRID_W
NA_KBLKS = 3
VMEM_LIMIT = 56 * 1024 * 1024


def _cparams(n_axes):
    return pltpu.CompilerParams(dimension_semantics=("arbitrary",) * n_axes,
                                vmem_limit_bytes=VMEM_LIMIT)


def _layer_spec(l, *shape, single=False):
    mode = dict(pipeline_mode=pl.Buffered(1)) if single else {}
    return pl.BlockSpec((None,) + shape, lambda *_: (l,) + (0,) * len(shape), **mode)


def _rms(x, g):
    ms = jnp.mean(x * x, axis=-1, keepdims=True)
    return x * lax.rsqrt(ms + EPS) * g


def _dot(a, b):
    return jnp.dot(a, b, preferred_element_type=F32)


def _dot_nt(a, b):
    return lax.dot_general(a, b, (((1,), (1,)), ((), ())), preferred_element_type=F32)


def _half_mask(parity):
    lane = lax.broadcasted_iota(jnp.int32, (1, LANES), 1)
    return (lane // HALF) == parity


def _keep(mask, x):
    return jnp.where(mask, x, jnp.zeros_like(x))


def _with_ones(own, parity, v):
    lane = lax.broadcasted_iota(jnp.int32, (1, LANES), 1)
    ones = (lane == HALF * (1 - parity)).astype(v.dtype)
    return jnp.where(own, v, ones)


def _normalised(own, parity, parts):
    o = functools.reduce(jnp.add, parts)
    spare = HALF * (1 - parity)
    return jnp.where(own, o, 0.0) / o[:, spare:spare + 1]


def _mod_kernel(c_ref, w_ref, b_ref, o_ref):
    c = c_ref[...]
    s = c * jax.nn.sigmoid(c)
    o_ref[...] = _dot(s.astype(BF16), w_ref[...].astype(BF16)) + b_ref[...]


def _modulation(cond, w_mod, b_mod):
    n = 6
    wide = 2 * D_MODEL
    out = pl.pallas_call(
        _mod_kernel,
        grid=(DEPTH, n * D_MODEL // wide),
        in_specs=[
            pl.BlockSpec((8, D_MODEL), lambda l, j: (0, 0)),
            pl.BlockSpec((None, D_MODEL, wide), lambda l, j: (l, 0, j)),
            pl.BlockSpec((None, 1, wide), lambda l, j: (l, 0, j)),
        ],
        out_specs=pl.BlockSpec((None, 8, wide), lambda l, j: (l, 0, j)),
        out_shape=jax.ShapeDtypeStruct((DEPTH, 8, n * D_MODEL), F32),
        compiler_params=_cparams(2),
        name="modulation",
    )(cond, w_mod, b_mod.reshape(DEPTH, 1, n * D_MODEL))
    return out.reshape(DEPTH, 8, n, D_MODEL)


def _rope(x, cos, sin, lane_lo):
    up = pltpu.roll(x, LANES - 8, axis=1)
    dn = pltpu.roll(x, 8, axis=1)
    return x * cos + jnp.where(lane_lo, up, dn) * sin


def _proj_kernel(rope, emit_f32, x_ref, mod_ref, gmix_ref, win_ref, gcq_ref, wuq_ref, gckv_ref,
                 wukv_ref, gsgu_ref, wsgu_ref, bsg_ref, *refs):
    if rope:
        cos_ref, sin_ref = refs[:2]
        refs = refs[2:]
    qa_ref, ka_ref, va_ref, qm_ref, kp_ref, vp_ref, oc_ref = refs[:7]
    refs = refs[7:]
    if emit_f32:
        kaf_ref, vaf_ref, ckvf_ref, krf_ref = refs[:4]
        refs = refs[4:]
    hb_ref, z_ref = refs

    x = x_ref[...]
    h = _rms(x, gmix_ref[...]) * (1.0 + mod_ref[1:2, :]) + mod_ref[0:1, :]
    hb_ref[...] = h.astype(BF16)
    tm = x.shape[0]
    z_ref[:, OFF_UV:] = _dot_nt(hb_ref[...], win_ref[OFF_UV:, :])
    z_ref[:, OFF_CQ:OFF_UV] = _dot_nt(hb_ref[...], win_ref[OFF_CQ:OFF_UV, :])
    z_ref[:, :OFF_CQ] = _dot_nt(hb_ref[...], win_ref[:OFF_CQ, :])
    lane = lax.broadcasted_iota(jnp.int32, (1, LANES), 1)
    lane_lo = (lane % 16) < 8
    if rope:
        cos = cos_ref[...]
        sin = sin_ref[...]

    uv = jax.nn.gelu(z_ref[:, OFF_UV:OFF_UV + 2 * SGU_WIDTH])
    u = uv[:, :SGU_WIDTH]
    vn = _rms(uv[:, SGU_WIDTH:], gsgu_ref[...])
    even = _half_mask(0)
    for ch in range(tm // SGU_CHUNK):
        rows = slice(ch * SGU_CHUNK, (ch + 1) * SGU_CHUNK)
        for p in range(SGU_GROUPS // 2):
            sl = slice(p * LANES, (p + 1) * LANES)
            vc = vn[rows, sl]
            mixed = (_dot(wsgu_ref[2 * p], jnp.where(even, vc, 0.0).astype(BF16))
                     + _dot(wsgu_ref[2 * p + 1], jnp.where(even, 0.0, vc).astype(BF16))
                     + bsg_ref[:, sl])
            oc_ref[rows, sl] = (u[rows, sl] * mixed).astype(BF16)

    cq = z_ref[:, OFF_CQ:OFF_CQ + MLA_Q_RANK]
    cqn = _rms(cq, gcq_ref[...]).astype(BF16)
    qm = _dot(cqn, wuq_ref[...])
    for hd in range(MLA_HEADS):
        qh = qm[:, hd * LANES:(hd + 1) * LANES]
        if rope:
            qh = _rope(qh, cos, sin, lane_lo)
        qm_ref[hd] = (qh * MLA_QSCALE).astype(BF16)

    ckv = z_ref[:, OFF_CKV:OFF_CKV + MLA_KV_RANK]
    ckvn = _rms(ckv, gckv_ref[...])
    kr = z_ref[:, OFF_KR:OFF_KR + LANES]
    if emit_f32:
        ckvf_ref[...] = ckvn
        krf_ref[...] = kr
    if rope:
        kr = _rope(kr, cos, sin, lane_lo)
    kv = _dot(ckvn.astype(BF16), wukv_ref[...])
    for hd in range(MLA_HEADS):
        kp_ref[hd] = (kv[:, hd * LANES:(hd + 1) * LANES] + kr).astype(BF16)
    voff = MLA_HEADS * LANES
    for p in range(MLA_HEADS // 2):
        vp_ref[p] = kv[:, voff + p * LANES:voff + (p + 1) * LANES].astype(BF16)

    qa = z_ref[:, OFF_QA:OFF_QA + NA_WIDTH] * NA_QSCALE
    ka = z_ref[:, OFF_KA:OFF_KA + NA_WIDTH]
    va = z_ref[:, OFF_VA:OFF_VA + NA_WIDTH]
    for p in range(NA_HEADS // 2):
        sl = slice(p * LANES, (p + 1) * LANES)
        qa_ref[p] = qa[:, sl].astype(BF16)
        ka_ref[p] = ka[:, sl].astype(BF16)
        va_ref[p] = va[:, sl].astype(BF16)
    if emit_f32:
        for src, dst in ((ka, kaf_ref), (va, vaf_ref)):
            src_t = src.T
            for sq in range(tm // SEQ):
                for hd in range(NA_HEADS):
                    dst[sq, hd] = src_t[hd * HEAD_DIM:(hd + 1) * HEAD_DIM, sq * SEQ:(sq + 1) * SEQ]


def _project(x, mods, mod_row, pw, l, rope_tabs, emit_f32, tiles_per_seq):
    t = x.shape[0]
    tm = PROJ_TM
    nt = t // tm
    rope = rope_tabs is not None
    in_specs = [
        pl.BlockSpec((tm, D_MODEL), lambda i: (i, 0)),
        pl.BlockSpec((None, 6, D_MODEL), lambda i: (mod_row(i), 0, 0)),
        _layer_spec(l, 1, D_MODEL),
        _layer_spec(l, IN_COLS, D_MODEL),
        _layer_spec(l, 1, MLA_Q_RANK),
        _layer_spec(l, MLA_Q_RANK, MLA_HEADS * LANES),
        _layer_spec(l, 1, MLA_KV_RANK),
        _layer_spec(l, MLA_KV_RANK, KV_COLS),
        _layer_spec(l, 1, SGU_WIDTH),
        _layer_spec(l, SGU_GROUPS, SGU_CHUNK, SGU_CHUNK),
        _layer_spec(l, SGU_CHUNK, SGU_WIDTH),
    ]
    args = [x, mods, pw["g_mix"], pw["w_in"], pw["g_cq"], pw["w_uq"], pw["g_ckv"], pw["w_ukv"],
            pw["g_sgu"], pw["w_sgu"], pw["b_sgu"]]
    if rope:
        tab_spec = pl.BlockSpec((tm, LANES), lambda i: (i % tiles_per_seq, 0))
        in_specs += [tab_spec, tab_spec]
        args += list(rope_tabs)

    def heads(n):
        return (pl.BlockSpec((n, tm, LANES), lambda i: (0, i, 0)),
                jax.ShapeDtypeStruct((n, t, LANES), BF16))

    def flat(w, dt):
        return (pl.BlockSpec((tm, w), lambda i: (i, 0)), jax.ShapeDtypeStruct((t, w), dt))

    outs = [heads(2), heads(2), heads(2), heads(MLA_HEADS), heads(MLA_HEADS), heads(MLA_HEADS // 2),
            flat(SGU_WIDTH, BF16)]
    if emit_f32:
        per_head = (pl.BlockSpec((tm // SEQ, NA_HEADS, HEAD_DIM, SEQ), lambda i: (i, 0, 0, 0)),
                    jax.ShapeDtypeStruct((t // SEQ, NA_HEADS, HEAD_DIM, SEQ), F32))
        outs += [per_head, per_head, flat(MLA_KV_RANK, F32), flat(LANES, F32)]
    return pl.pallas_call(
        functools.partial(_proj_kernel, rope, emit_f32),
        grid=(nt,),
        in_specs=in_specs,
        out_specs=[o[0] for o in outs],
        out_shape=[o[1] for o in outs],
        scratch_shapes=[pltpu.VMEM((tm, D_MODEL), BF16), pltpu.VMEM((tm, IN_COLS), F32)],
        compiler_params=_cparams(1),
        name="project_lat" if rope else "project_ctx",
    )(*args)


def _cache_kv_kernel(ckv_ref, kr_ref, wukv_ref, kp_ref, vp_ref):
    kv = _dot(ckv_ref[...].astype(BF16), wukv_ref[...])
    kr = kr_ref[...]
    for hd in range(MLA_HEADS):
        kp_ref[hd] = (kv[:, hd * LANES:(hd + 1) * LANES] + kr).astype(BF16)
    voff = MLA_HEADS * LANES
    for p in range(MLA_HEADS // 2):
        vp_ref[p] = kv[:, voff + p * LANES:voff + (p + 1) * LANES].astype(BF16)


def _cache_kv(cache_ckv, cache_kr_pad, w_ukv):
    b = cache_ckv.shape[0]
    return pl.pallas_call(
        _cache_kv_kernel,
        grid=(DEPTH, b),
        in_specs=[
            pl.BlockSpec((None, None, PAST_LEN, MLA_KV_RANK), lambda l, i: (i, l, 0, 0)),
            pl.BlockSpec((None, None, PAST_LEN, LANES), lambda l, i: (i, l, 0, 0)),
            pl.BlockSpec((None, MLA_KV_RANK, KV_COLS), lambda l, i: (l, 0, 0)),
        ],
        out_specs=[
            pl.BlockSpec((None, MLA_HEADS, PAST_LEN, LANES), lambda l, i: (l, 0, i, 0)),
            pl.BlockSpec((None, MLA_HEADS // 2, PAST_LEN, LANES), lambda l, i: (l, 0, i, 0)),
        ],
        out_shape=[
            jax.ShapeDtypeStruct((DEPTH, MLA_HEADS, b * PAST_LEN, LANES), BF16),
            jax.ShapeDtypeStruct((DEPTH, MLA_HEADS // 2, b * PAST_LEN, LANES), BF16),
        ],
        compiler_params=_cparams(2),
        name="cache_kv",
    )(cache_ckv, cache_kr_pad, w_ukv)


def _ctx_attn_kernel(qa_ref, ka_ref, va_ref, qm_ref, kp_ref, vp_ref, o_ref):
    for sq in range(CTX_SEQS):
        rows = slice(sq * SEQ, (sq + 1) * SEQ)
        for p in range(NA_HEADS // 2 + MLA_HEADS // 2):
            out = None
            for half in range(2):
                own = _half_mask(half)
                if p < NA_HEADS // 2:
                    q = _keep(own, qa_ref[p, rows, :])
                    k = ka_ref[p, rows, :]
                    v = va_ref[p, rows, :]
                else:
                    hd = 2 * (p - NA_HEADS // 2) + half
                    q = qm_ref[hd, rows, :]
                    k = kp_ref[hd, rows, :]
                    v = vp_ref[hd // 2, rows, :]
                s = _dot_nt(q, k)
                p_ = jnp.exp2(s - jnp.max(s, axis=-1, keepdims=True))
                o = _dot(p_.astype(BF16), _keep(own, v)) / jnp.sum(p_, axis=-1, keepdims=True)
                out = o if out is None else out + o
            o_ref[rows, p * LANES:(p + 1) * LANES] = out.astype(BF16)


def _ctx_attention(qa, ka, va, qm, kp, vp, n_seq):
    rows = CTX_SEQS * SEQ
    heads = lambda n: pl.BlockSpec((n, rows, LANES), lambda b: (0, b, 0))
    width = NA_WIDTH + MLA_WIDTH
    return pl.pallas_call(
        _ctx_attn_kernel,
        grid=(n_seq // CTX_SEQS,),
        in_specs=[heads(2), heads(2), heads(2), heads(MLA_HEADS), heads(MLA_HEADS), heads(MLA_HEADS // 2)],
        out_specs=pl.BlockSpec((rows, width), lambda b: (b, 0)),
        out_shape=jax.ShapeDtypeStruct((n_seq * SEQ, width), BF16),
        compiler_params=_cparams(1),
        name="ctx_attention",
    )(qa, ka, va, qm, kp, vp)


def _lat_mla_kernel(q_ref, kc_ref, kl_ref, vc_ref, vl_ref, o_ref, s_ref, m_ref, p_ref):
    nc = kc_ref.shape[1]
    outs = {}

    def scores(hd):
        q = q_ref[hd]
        sc = _dot_nt(q, kc_ref[hd])
        sl = _dot_nt(q, kl_ref[hd])
        s_ref[hd % 2, :, 0:nc] = sc
        s_ref[hd % 2, :, nc:] = sl
        m_ref[hd % 2] = jnp.maximum(_lane_block_max(sc), _lane_block_max(sl))

    def softmax(hd):
        m = jnp.max(m_ref[hd % 2], axis=-1, keepdims=True)
        p_ref[hd % 2] = jnp.exp2(s_ref[hd % 2] - m).astype(BF16)

    def values(hd):
        own = _half_mask(hd % 2)
        o = _normalised(own, hd % 2, [_dot(p_ref[hd % 2, :, 0:nc], _with_ones(own, hd % 2, vc_ref[hd // 2])),
                                      _dot(p_ref[hd % 2, :, nc:], _with_ones(own, hd % 2, vl_ref[hd // 2]))])
        if hd % 2 == 0:
            outs[hd // 2] = o
        else:
            o_ref[:, (hd // 2) * LANES:(hd // 2 + 1) * LANES] = (outs.pop(hd // 2) + o).astype(BF16)

    for t in range(MLA_HEADS + 2):
        if t < MLA_HEADS:
            scores(t)
        if 0 <= t - 1 < MLA_HEADS:
            softmax(t - 1)
        if 0 <= t - 2 < MLA_HEADS:
            values(t - 2)


def _lat_mla(q, kc, vc, kl, vl, l, n_batch):
    nq = DEC_SEQ // TM
    n_keys = PAST_LEN + DEC_SEQ
    return pl.pallas_call(
        _lat_mla_kernel,
        grid=(n_batch, nq),
        in_specs=[pl.BlockSpec((MLA_HEADS, TM, LANES), lambda b, i: (0, b * nq + i, 0)),
                  pl.BlockSpec((None, MLA_HEADS, PAST_LEN, LANES), lambda b, i: (l, 0, b, 0)),
                  pl.BlockSpec((MLA_HEADS, DEC_SEQ, LANES), lambda b, i: (0, b, 0)),
                  pl.BlockSpec((None, MLA_HEADS // 2, PAST_LEN, LANES), lambda b, i: (l, 0, b, 0)),
                  pl.BlockSpec((MLA_HEADS // 2, DEC_SEQ, LANES), lambda b, i: (0, b, 0))],
        out_specs=pl.BlockSpec((TM, MLA_WIDTH), lambda b, i: (b * nq + i, 0)),
        out_shape=jax.ShapeDtypeStruct((n_batch * DEC_SEQ, MLA_WIDTH), BF16),
        scratch_shapes=[pltpu.VMEM((2, TM, n_keys), F32), pltpu.VMEM((2, TM, LANES), F32),
                        pltpu.VMEM((2, TM, n_keys), BF16)],
        compiler_params=_cparams(2),
        name="lat_mla",
    )(q, kc, kl, vc, vl)


def _na_bias_kernel(rpb_ref, o_ref, tab_ref):
    l = pl.program_id(0)
    hd = pl.program_id(1)
    n_dr = 2 * NA_WIN_R - 1
    n_dc = 2 * NA_WIN_C - 1
    base = (l * NA_HEADS + hd) * n_dr * n_dc
    qc = lax.broadcasted_iota(jnp.int32, (GRID_W, LANES), 0)
    kc = lax.broadcasted_iota(jnp.int32, (GRID_W, LANES), 1) % GRID_W
    diff = kc - qc
    cs = jnp.clip(qc - NA_WIN_C // 2, 0, GRID_W - NA_WIN_C)
    in_win = (kc >= cs) & (kc < cs + NA_WIN_C)
    neg = jnp.full((GRID_W, LANES), NEG_INF * LOG2E, F32)
    for dr in range(n_dr):
        acc = neg
        for dc in range(n_dc):
            acc = jnp.where(diff == dc - (NA_WIN_C - 1), rpb_ref[base + dr * n_dc + dc] * LOG2E, acc)
        tab_ref[dr] = jnp.where(in_win, acc, neg)
    left = lax.broadcasted_iota(jnp.int32, (GRID_W, LANES), 1) < GRID_W
    n_krows = NA_KBLKS * NA_QBLK // GRID_W
    cases = ((0, lambda qr: 0), (-NA_QROWS, lambda qr: qr), (-2 * NA_QROWS, lambda qr: NA_QROWS))
    for c, (shift, first) in enumerate(cases):
        for qr in range(NA_QROWS):
            for kp in range(n_krows // 2):
                tiles = []
                for kr in (2 * kp, 2 * kp + 1):
                    ok = first(qr) <= kr < first(qr) + NA_WIN_R
                    tiles.append(tab_ref[kr - qr + shift + NA_WIN_R - 1] if ok else neg)
                o_ref[c, qr * GRID_W:(qr + 1) * GRID_W, kp * LANES:(kp + 1) * LANES] = (
                    jnp.where(left, tiles[0], tiles[1]))


def _na_bias(na_rpb):
    n_k = NA_KBLKS * NA_QBLK
    return pl.pallas_call(
        _na_bias_kernel,
        grid=(DEPTH, NA_HEADS),
        in_specs=[pl.BlockSpec(memory_space=pltpu.SMEM)],
        out_specs=pl.BlockSpec((None, 3, None, NA_QBLK, n_k), lambda l, h: (l, 0, h, 0, 0)),
        out_shape=jax.ShapeDtypeStruct((DEPTH, 3, NA_HEADS, NA_QBLK, n_k), F32),
        scratch_shapes=[pltpu.VMEM((2 * NA_WIN_R - 1, GRID_W, LANES), F32)],
        compiler_params=_cparams(2),
        name="na_bias",
    )(na_rpb.reshape(-1))


def _lane_block_max(s):
    return functools.reduce(jnp.maximum, [s[:, j * LANES:(j + 1) * LANES] for j in range(s.shape[1] // LANES)])


def _na_kernel(n_sub, q_ref, *refs):
    per = 2 * NA_KBLKS + 1
    subs = [refs[r * per:(r + 1) * per] for r in range(n_sub)]
    kc_ref, vc_ref, o_ref, s_ref, m_ref, p_ref = refs[n_sub * per:]
    items = [(r, hd) for r in range(n_sub) for hd in range(NA_HEADS)]
    outs = {}
    n_loc = NA_KBLKS * NA_QBLK

    def scores(t):
        r, hd = items[t]
        k_refs, bias_ref = subs[r][:NA_KBLKS], subs[r][-1]
        p = hd // 2
        q = _keep(_half_mask(hd % 2), q_ref[p, r * NA_QBLK:(r + 1) * NA_QBLK, :])
        m = None
        for i in range(NA_KBLKS):
            s = _dot_nt(q, k_refs[i][p]) + bias_ref[hd, :, i * NA_QBLK:(i + 1) * NA_QBLK]
            s_ref[t % 2, :, i * NA_QBLK:(i + 1) * NA_QBLK] = s
            m = _lane_block_max(s) if m is None else jnp.maximum(m, _lane_block_max(s))
        s = _dot(q, kc_ref[p * LANES:(p + 1) * LANES, :].astype(BF16))
        s_ref[t % 2, :, n_loc:] = s
        m_ref[t % 2] = jnp.maximum(m, _lane_block_max(s))

    def softmax(t):
        m = jnp.max(m_ref[t % 2], axis=-1, keepdims=True)
        p_ref[t % 2] = jnp.exp2(s_ref[t % 2] - m).astype(BF16)

    def values(t):
        r, hd = items[t]
        v_refs = subs[r][NA_KBLKS:2 * NA_KBLKS]
        p = hd // 2
        own = _half_mask(hd % 2)
        row = lax.broadcasted_iota(jnp.int32, (LANES, 1), 0)
        vc = vc_ref[p * LANES:(p + 1) * LANES, :].astype(BF16)
        vc = jnp.where((row // HALF) == hd % 2, vc, (row == HALF * (1 - hd % 2)).astype(BF16))
        parts = [_dot_nt(p_ref[t % 2, :, n_loc:], vc)]
        for i in range(NA_KBLKS):
            parts.append(_dot(p_ref[t % 2, :, i * NA_QBLK:(i + 1) * NA_QBLK],
                              _with_ones(own, hd % 2, v_refs[i][p])))
        o = _normalised(own, hd % 2, parts)
        if hd % 2 == 0:
            outs[(r, p)] = o
        else:
            o_ref[r * NA_QBLK:(r + 1) * NA_QBLK, p * LANES:(p + 1) * LANES] = (outs.pop((r, p)) + o).astype(BF16)

    for t in range(len(items) + 2):
        if t < len(items):
            scores(t)
        if 0 <= t - 1 < len(items):
            softmax(t - 1)
        if 0 <= t - 2 < len(items):
            values(t - 2)


def _na_attention(qa, ka, va, cache_k, cache_v, bias, l, n_batch):
    nblk = DEC_SEQ // NA_QBLK
    max_start = nblk - NA_KBLKS
    n_sub = NA_SUBS
    nstep = nblk // n_sub
    blk = lambda jj, r: jj * n_sub + r

    def kspec(r, i):
        return pl.BlockSpec((NA_HEADS // 2, NA_QBLK, LANES),
                            lambda jj, b: (0, b * nblk + jnp.clip(blk(jj, r) - 1, 0, max_start) + i, 0))

    case = lambda j: jnp.where(j == 0, 0, jnp.where(j == nblk - 1, 2, 1))
    cache_spec = pl.BlockSpec((None, None, NA_WIDTH, PAST_LEN), lambda jj, b: (b, l, 0, 0))
    in_specs = [pl.BlockSpec((NA_HEADS // 2, n_sub * NA_QBLK, LANES), lambda jj, b: (0, b * nstep + jj, 0))]
    args = [qa]
    for r in range(n_sub):
        in_specs += [kspec(r, i) for i in range(NA_KBLKS)] + [kspec(r, i) for i in range(NA_KBLKS)]
        in_specs.append(pl.BlockSpec((None, None, NA_HEADS, NA_QBLK, NA_KBLKS * NA_QBLK),
                                     lambda jj, b, r=r: (l, case(blk(jj, r)), 0, 0, 0)))
        args += [ka] * NA_KBLKS + [va] * NA_KBLKS + [bias]
    in_specs += [cache_spec, cache_spec]
    args += [cache_k, cache_v]
    return pl.pallas_call(
        functools.partial(_na_kernel, n_sub),
        grid=(nstep, n_batch),
        in_specs=in_specs,
        out_specs=pl.BlockSpec((n_sub * NA_QBLK, NA_WIDTH), lambda jj, b: (b * nstep + jj, 0)),
        out_shape=jax.ShapeDtypeStruct((n_batch * DEC_SEQ, NA_WIDTH), BF16),
        scratch_shapes=[pltpu.VMEM((2, NA_QBLK, (NA_KBLKS + 1) * NA_QBLK), F32),
                        pltpu.VMEM((2, NA_QBLK, LANES), F32),
                        pltpu.VMEM((2, NA_QBLK, (NA_KBLKS + 1) * NA_QBLK), BF16)],
        compiler_params=_cparams(2),
        name="na_attention",
    )(*args)


def _ffn_kernel(seq_len, final, widths, *refs):
    halo = seq_len > TM
    n_in = 3 if halo else 1
    x_refs = refs[:n_in]
    refs = refs[n_in:]
    part_refs = [refs[i * n_in:(i + 1) * n_in] for i in range(len(widths))]
    refs = refs[n_in * len(widths):]
    mod_ref, gffn_ref, wo_ref, win_ref, cw_ref, cb_ref, wout_ref = refs[:7]
    refs = refs[7:]
    if final:
        gfin_ref = refs[0]
        refs = refs[1:]
    o_ref, mix_ref, hext_ref, a_ref, act_ref = refs
    tm = o_ref.shape[0]
    pad = BF16_ROWS
    if halo:
        tiles_per_seq = seq_len // tm
        pos = pl.program_id(0) % tiles_per_seq
        pieces = [(0, 0, slice(0, pad), pos != 0), (pad, 1, slice(0, tm), None),
                  (pad + tm, 2, slice(0, pad), pos != tiles_per_seq - 1)]
        frames = []
        starts = [pad]
        seg_len = tm
    else:
        nseg = tm // seq_len
        starts = [pad + s * (seq_len + pad) for s in range(nseg)]
        seg_len = seq_len
        pieces = [(starts[s], 0, slice(s * seq_len, (s + 1) * seq_len), None) for s in range(nseg)]
        frames = [s * (seq_len + pad) for s in range(nseg + 1)]
    for f in frames:
        mix_ref[f:f + pad] = jnp.zeros((pad, D_MODEL), BF16)
        hext_ref[f:f + pad] = jnp.zeros((pad, D_MODEL), BF16)
    off = 0
    for n, prefs in zip(widths, part_refs):
        for row, src, rows, _ in pieces:
            mix_ref[row:row + rows.stop - rows.start, off:off + n] = prefs[src][rows, :]
        off += n
    r = _dot(mix_ref[...], wo_ref[...])
    g1 = mod_ref[2:3, :]
    scale = gffn_ref[...] * (1.0 + mod_ref[4:5, :])
    shift = mod_ref[3:4, :]
    for row, src, rows, keep in pieces:
        n_rows = rows.stop - rows.start
        x1 = x_refs[src][rows, :] + g1 * r[row:row + n_rows]
        if src == n_in // 2:
            o_ref[rows, :] = x1
        h2 = (x1 * lax.rsqrt(jnp.mean(x1 * x1, axis=-1, keepdims=True) + EPS) * scale + shift).astype(BF16)
        if keep is not None:
            h2 = jnp.where(keep, h2, jnp.zeros_like(h2))
        hext_ref[row:row + n_rows] = h2

    def col(c, lb):
        half = FF_LB // 2
        return (lb // half) * D_FF + c * FF_CHUNK + (lb % half) * LANES

    def up(c):
        hext = hext_ref[...]
        for part in range(2):
            a = _dot(hext, win_ref[:, part * D_FF + c * FF_CHUNK:part * D_FF + (c + 1) * FF_CHUNK])
            for j in range(FF_LB // 2):
                a_ref[c % 2, part * (FF_LB // 2) + j] = a[:, j * LANES:(j + 1) * LANES]

    def conv(c, lb, st):
        cs = slice(col(c, lb), col(c, lb) + LANES)
        return (a_ref[c % 2, lb, pl.ds(st - 1, seg_len, stride=1), :] * cw_ref[0:1, cs]
                + a_ref[c % 2, lb, st:st + seg_len, :] * cw_ref[1:2, cs]
                + a_ref[c % 2, lb, pl.ds(st + 1, seg_len, stride=1), :] * cw_ref[2:3, cs] + cb_ref[:, cs])

    up(0)
    for c in range(N_FF_CHUNKS):
        if c + 1 < N_FF_CHUNKS:
            up(c + 1)
        for s, st in enumerate(starts):
            for lb in range(FF_LB // 2):
                gate = conv(c, lb, st)
                val = conv(c, lb + FF_LB // 2, st)
                act_ref[s * seg_len:(s + 1) * seg_len, c * FF_CHUNK + lb * LANES:c * FF_CHUNK + (lb + 1) * LANES] = (
                    gate * jax.nn.sigmoid(gate) * val).astype(BF16)
    y = o_ref[...] + mod_ref[5:6, :] * _dot(act_ref[...], wout_ref[...])
    if final:
        y = _rms(y, gfin_ref[...])
    o_ref[...] = y


def _ffn(x, parts, mods, mod_row, pw, l, seq_len, g_final, name):
    t = x.shape[0]
    nt = t // TM
    hb = TM // BF16_ROWS
    n_hblk = t // BF16_ROWS
    final = g_final is not None
    halo = seq_len > TM
    ext = TM + 2 * BF16_ROWS if halo else TM + (TM // seq_len + 1) * BF16_ROWS
    widths = tuple(p.shape[1] for p in parts)

    def tiled(a):
        w = a.shape[1]
        main = pl.BlockSpec((TM, w), lambda i: (i, 0))
        if not halo:
            return [main], [a]
        prev = pl.BlockSpec((BF16_ROWS, w), lambda i: (jnp.maximum(i * hb - 1, 0), 0))
        nxt = pl.BlockSpec((BF16_ROWS, w), lambda i: (jnp.minimum((i + 1) * hb, n_hblk - 1), 0))
        return [prev, main, nxt], [a, a, a]

    in_specs, args = [], []
    for a in (x,) + tuple(parts):
        sp, ar = tiled(a)
        in_specs += sp
        args += ar
    in_specs += [pl.BlockSpec((None, 6, D_MODEL), lambda i: (mod_row(i), 0, 0)),
                 _layer_spec(l, 1, D_MODEL),
                 _layer_spec(l, D_MODEL, D_MODEL, single=True),
                 _layer_spec(l, D_MODEL, 2 * D_FF, single=True),
                 _layer_spec(l, 3, 2 * D_FF, single=True),
                 _layer_spec(l, 1, 2 * D_FF, single=True),
                 _layer_spec(l, D_FF, D_MODEL, single=True)]
    args += [mods, pw["g_ffn"], pw["w_out"], pw["w_ffn_in"], pw["conv_w"], pw["conv_b"], pw["w_ffn_out"]]
    if final:
        in_specs.append(pl.BlockSpec((1, D_MODEL), lambda i: (0, 0)))
        args.append(g_final)
    return pl.pallas_call(
        functools.partial(_ffn_kernel, seq_len, final, widths),
        grid=(nt,),
        in_specs=in_specs,
        out_specs=pl.BlockSpec((TM, D_MODEL), lambda i: (i, 0)),
        out_shape=jax.ShapeDtypeStruct((t, D_MODEL), F32),
        scratch_shapes=[pltpu.VMEM((ext, D_MODEL), BF16), pltpu.VMEM((ext, D_MODEL), BF16),
                        pltpu.VMEM((2, FF_LB, ext, LANES), F32), pltpu.VMEM((TM, D_FF), BF16)],
        compiler_params=_cparams(1),
        name=name,
    )(*args)


def _pack_mla_kernel(wuq_ref, wukv_ref, ouq_ref, oukv_ref):
    dq = MLA_NOPE + MLA_ROPE
    zq = jnp.zeros((MLA_Q_RANK, LANES - dq), F32)
    zk = jnp.zeros((MLA_KV_RANK, LANES - MLA_NOPE), F32)
    dkv = MLA_NOPE + MLA_V
    for hd in range(MLA_HEADS):
        ouq_ref[:, hd * LANES:(hd + 1) * LANES] = jnp.concatenate(
            [wuq_ref[:, hd * dq:(hd + 1) * dq], zq], axis=-1).astype(BF16)
        oukv_ref[:, hd * LANES:(hd + 1) * LANES] = jnp.concatenate(
            [wukv_ref[:, hd * dkv:hd * dkv + MLA_NOPE], zk], axis=-1).astype(BF16)
    voff = MLA_HEADS * LANES
    for p in range(MLA_HEADS // 2):
        oukv_ref[:, voff + p * LANES:voff + (p + 1) * LANES] = jnp.concatenate(
            [wukv_ref[:, (2 * p + j) * dkv + MLA_NOPE:(2 * p + j + 1) * dkv] for j in range(2)], axis=-1).astype(BF16)


def _pack_weights(w_in, w_uq, w_ukv, w_sgu, b_sgu, w_out, w_ffn_in, ffn_conv_w, ffn_conv_b, w_ffn_out,
                  g_mix, g_cq, g_ckv, g_sgu, g_ffn):
    nl = w_in.shape[0]
    wt = jnp.swapaxes(w_in, 1, 2).astype(BF16)
    zr = lambda n: jnp.zeros((nl, n, D_MODEL), BF16)
    w_in_p = jnp.concatenate([wt[:, :OFF_KR], zr(HALF), wt[:, OFF_KR:OFF_KR + MLA_ROPE],
                              zr(LANES - HALF - MLA_ROPE), wt[:, OFF_KR + MLA_ROPE:]], axis=1)
    w_uq_p, w_ukv_p = pl.pallas_call(
        _pack_mla_kernel,
        grid=(nl,),
        in_specs=[pl.BlockSpec((None,) + w_uq.shape[1:], lambda l: (l, 0, 0)),
                  pl.BlockSpec((None,) + w_ukv.shape[1:], lambda l: (l, 0, 0))],
        out_specs=[pl.BlockSpec((None, MLA_Q_RANK, MLA_HEADS * LANES), lambda l: (l, 0, 0)),
                   pl.BlockSpec((None, MLA_KV_RANK, KV_COLS), lambda l: (l, 0, 0))],
        out_shape=[jax.ShapeDtypeStruct((nl, MLA_Q_RANK, MLA_HEADS * LANES), BF16),
                   jax.ShapeDtypeStruct((nl, MLA_KV_RANK, KV_COLS), BF16)],
        compiler_params=_cparams(1),
        name="pack_w_mla",
    )(w_uq, w_ukv)
    b_sgu_p = jnp.repeat(jnp.swapaxes(b_sgu, 1, 2), SGU_WIDTH // SGU_GROUPS, axis=-1)
    return dict(
        w_in=w_in_p, w_uq=w_uq_p, w_ukv=w_ukv_p, w_sgu=w_sgu.astype(BF16), b_sgu=b_sgu_p,
        w_out=w_out.astype(BF16), w_ffn_in=w_ffn_in.astype(BF16), conv_w=ffn_conv_w,
        conv_b=ffn_conv_b[:, None, :], w_ffn_out=w_ffn_out.astype(BF16),
        g_mix=g_mix[:, None, :], g_cq=g_cq[:, None, :], g_ckv=g_ckv[:, None, :], g_sgu=g_sgu[:, None, :],
        g_ffn=g_ffn[:, None, :])


def _rope_tables(n_tokens):
    t = jnp.arange(n_tokens)
    n_freq = MLA_ROPE // 4
    inv_freq = ROPE_THETA ** (-jnp.arange(n_freq, dtype=F32) / n_freq)
    ang_r = (t // GRID_W).astype(F32)[:, None] * inv_freq
    ang_c = (t % GRID_W).astype(F32)[:, None] * inv_freq
    ones = jnp.ones((n_tokens, HALF), F32)
    tail = LANES - HALF - MLA_ROPE
    cos = jnp.concatenate([ones, jnp.cos(ang_r), jnp.cos(ang_r), jnp.cos(ang_c), jnp.cos(ang_c),
                           ones[:, :tail]], axis=-1)
    sin = jnp.concatenate([0 * ones, -jnp.sin(ang_r), jnp.sin(ang_r), -jnp.sin(ang_c), jnp.sin(ang_c),
                           0 * ones[:, :tail]], axis=-1)
    return cos, sin


def kernel(x_prompt, x_sample, cache_na_k, cache_na_v, cache_mla_ckv, cache_mla_krope, c, c_ctx, w_mod, b_mod,
           g_mix, w_in, na_rpb, g_cq, w_uq, g_ckv, w_ukv, g_sgu, w_sgu, b_sgu, w_out, g_ffn, w_ffn_in,
           ffn_conv_w, ffn_conv_b, w_ffn_out, g_final):
    n_ctx, n_lat = x_prompt.shape[0], x_sample.shape[0]
    t_ctx, t_lat = n_ctx * SEQ, n_lat * DEC_SEQ
    pw = _pack_weights(w_in, w_uq, w_ukv, w_sgu, b_sgu, w_out, w_ffn_in, ffn_conv_w, ffn_conv_b,
                       w_ffn_out, g_mix, g_cq, g_ckv, g_sgu, g_ffn)
    g_fin = g_final[None, :]

    cond = jnp.concatenate([c_ctx[None, :], c, jnp.zeros((8 - 1 - n_lat, D_MODEL), F32)], axis=0)
    mods = _modulation(cond, w_mod, b_mod)
    rope_tabs = _rope_tables(DEC_SEQ)
    na_bias = _na_bias(na_rpb)
    kr_pad = jnp.pad(cache_mla_krope, ((0, 0), (0, 0), (0, 0), (HALF, LANES - HALF - MLA_ROPE)))
    cache_kp, cache_vp = _cache_kv(cache_mla_ckv, kr_pad, pw["w_ukv"])
    channel_major = lambda a: jnp.transpose(a, (0, 1, 3, 4, 2)).reshape(n_lat, DEPTH, NA_WIDTH, PAST_LEN)
    cache_k = channel_major(cache_na_k)
    cache_v = channel_major(cache_na_v)

    lat_tiles = DEC_SEQ // TM
    proj_tiles = DEC_SEQ // PROJ_TM
    ctx_row = lambda i: 0
    lat_row = lambda i: 1 + i // lat_tiles
    lat_proj_row = lambda i: 1 + i // proj_tiles

    xp = x_prompt.reshape(t_ctx, D_MODEL)
    xs = x_sample.reshape(t_lat, D_MODEL)
    new_k, new_v, new_ckv, new_kr = [], [], [], []
    for l in range(DEPTH):
        last = l == DEPTH - 1
        m = mods[l]
        qa, ka, va, qm, kp, vp, oc, ka_f, va_f, ckv_f, kr_f = _project(xp, m, ctx_row, pw, l, None, True, 1)
        new_k.append(ka_f)
        new_v.append(va_f)
        new_ckv.append(ckv_f)
        new_kr.append(kr_f[:, HALF:HALF + MLA_ROPE])
        o_ab = _ctx_attention(qa, ka, va, qm, kp, vp, n_ctx)
        xp = _ffn(xp, (o_ab, oc), m, ctx_row, pw, l, SEQ, g_fin if last else None, "ctx_ffn")

        qa, ka, va, qm, kp, vp, oc = _project(xs, m, lat_proj_row, pw, l, rope_tabs, False, proj_tiles)
        o_a = _na_attention(qa, ka, va, cache_k, cache_v, na_bias, l, n_lat)
        o_b = _lat_mla(qm, cache_kp, cache_vp, kp, vp, l, n_lat)
        xs = _ffn(xs, (o_a, o_b, oc), m, lat_row, pw, l, DEC_SEQ, g_fin if last else None, "lat_ffn")

    def stacked(parts, tail):
        return jnp.stack([a.reshape((n_ctx, SEQ) + tail) for a in parts], axis=1)

    def stacked_t(parts):
        return jnp.transpose(jnp.stack(parts, axis=1), (0, 1, 4, 2, 3))

    return (xp.reshape(n_ctx, SEQ, D_MODEL), xs.reshape(n_lat, DEC_SEQ, D_MODEL),
            stacked_t(new_k), stacked_t(new_v),
            stacked(new_ckv, (MLA_KV_RANK,)), stacked(new_kr, (MLA_ROPE,)))
```

```python
import functools
import math

import jax
import jax.numpy as jnp
from jax import lax
from jax.experimental import pallas as pl
from jax.experimental.pallas import tpu as pltpu

F32 = jnp.float32
BF16 = jnp.bfloat16

D_MODEL = 1024
DEPTH = 4
SEQ = 256
DEC_SEQ = 2048
PAST_LEN = 256
GRID_W = 64
HEAD_DIM = 64
NA_WIDTH = 256
NA_HEADS = 4
NA_WIN_R = 8
NA_WIN_C = 16
MLA_HEADS = 8
MLA_NOPE = 64
MLA_ROPE = 32
MLA_V = 64
MLA_WIDTH = MLA_HEADS * MLA_V
MLA_Q_RANK = 384
MLA_KV_RANK = 256
SGU_WIDTH = 256
SGU_GROUPS = 4
SGU_CHUNK = 128
D_FF = 2816
ROPE_THETA = 10000.0
EPS = 1e-6
NEG_INF = -1e30
LOG2E = math.log2(math.e)
NA_QSCALE = HEAD_DIM ** -0.5 * LOG2E
MLA_QSCALE = (MLA_NOPE + MLA_ROPE) ** -0.5 * LOG2E

LANES = 128
BF16_ROWS = 16
HALF = LANES // 2

OFF_QA, OFF_KA, OFF_VA = 0, 256, 512
OFF_CQ = 768
OFF_CKV = OFF_CQ + MLA_Q_RANK
OFF_KR = OFF_CKV + MLA_KV_RANK
OFF_UV = OFF_KR + LANES
IN_COLS = OFF_UV + 2 * SGU_WIDTH
KV_COLS = MLA_HEADS * LANES + MLA_HEADS * MLA_V

FF_CHUNK = 256
N_FF_CHUNKS = D_FF // FF_CHUNK
FF_LB = 2 * FF_CHUNK // LANES

TM = 512
PROJ_TM = 1024
CTX_SEQS = 4
NA_SUBS = 4
NA_QROWS = 4
NA_QBLK = NA_QROWS * GRID_W
NA_KBLKS = 3
VMEM_LIMIT = 56 * 1024 * 1024


def _cparams(n_axes):
    return pltpu.CompilerParams(dimension_semantics=("arbitrary",) * n_axes,
                                vmem_limit_bytes=VMEM_LIMIT)


def _layer_spec(l, *shape, single=False):
    mode = dict(pipeline_mode=pl.Buffered(1)) if single else {}
    return pl.BlockSpec((None,) + shape, lambda *_: (l,) + (0,) * len(shape), **mode)


def _rms(x, g):
    ms = jnp.mean(x * x, axis=-1, keepdims=True)
    return x * lax.rsqrt(ms + EPS) * g


def _dot(a, b):
    return jnp.dot(a, b, preferred_element_type=F32)


def _dot_nt(a, b):
    return lax.dot_general(a, b, (((1,), (1,)), ((), ())), preferred_element_type=F32)


def _half_mask(parity):
    lane = lax.broadcasted_iota(jnp.int32, (1, LANES), 1)
    return (lane // HALF) == parity


def _keep(mask, x):
    return jnp.where(mask, x, jnp.zeros_like(x))


def _with_ones(own, parity, v):
    lane = lax.broadcasted_iota(jnp.int32, (1, LANES), 1)
    ones = (lane == HALF * (1 - parity)).astype(v.dtype)
    return jnp.where(own, v, ones)


def _normalised(own, parity, parts):
    o = functools.reduce(jnp.add, parts)
    spare = HALF * (1 - parity)
    return jnp.where(own, o, 0.0) / o[:, spare:spare + 1]


def _mod_kernel(c_ref, w_ref, b_ref, o_ref):
    c = c_ref[...]
    s = c * jax.nn.sigmoid(c)
    o_ref[...] = _dot(s.astype(BF16), w_ref[...].astype(BF16)) + b_ref[...]


def _modulation(cond, w_mod, b_mod):
    n = 6
    wide = 2 * D_MODEL
    out = pl.pallas_call(
        _mod_kernel,
        grid=(DEPTH, n * D_MODEL // wide),
        in_specs=[
            pl.BlockSpec((8, D_MODEL), lambda l, j: (0, 0)),
            pl.BlockSpec((None, D_MODEL, wide), lambda l, j: (l, 0, j)),
            pl.BlockSpec((None, 1, wide), lambda l, j: (l, 0, j)),
        ],
        out_specs=pl.BlockSpec((None, 8, wide), lambda l, j: (l, 0, j)),
        out_shape=jax.ShapeDtypeStruct((DEPTH, 8, n * D_MODEL), F32),
        compiler_params=_cparams(2),
        name="modulation",
    )(cond, w_mod, b_mod.reshape(DEPTH, 1, n * D_MODEL))
    return out.reshape(DEPTH, 8, n, D_MODEL)


def _rope(x, cos, sin, lane_lo):
    up = pltpu.roll(x, LANES - 8, axis=1)
    dn = pltpu.roll(x, 8, axis=1)
    return x * cos + jnp.where(lane_lo, up, dn) * sin


def _proj_kernel(rope, emit_f32, x_ref, mod_ref, gmix_ref, win_ref, gcq_ref, wuq_ref, gckv_ref,
                 wukv_ref, gsgu_ref, wsgu_ref, bsg_ref, *refs):
    if rope:
        cos_ref, sin_ref = refs[:2]
        refs = refs[2:]
    if emit_f32:
        refs = refs[3:]
    qa_ref, ka_ref, va_ref, qm_ref, kp_ref, vp_ref, oc_ref = refs[:7]
    refs = refs[7:]
    if emit_f32:
        kaf_ref, vaf_ref, ckvf_ref, krf_ref = refs[:4]
        refs = refs[4:]
    hb_ref, z_ref = refs

    x = x_ref[...]
    h = _rms(x, gmix_ref[...]) * (1.0 + mod_ref[1:2, :]) + mod_ref[0:1, :]
    hb_ref[...] = h.astype(BF16)
    tm = x.shape[0]
    z_ref[:, OFF_UV:] = _dot_nt(hb_ref[...], win_ref[OFF_UV:, :])
    z_ref[:, OFF_CQ:OFF_UV] = _dot_nt(hb_ref[...], win_ref[OFF_CQ:OFF_UV, :])
    z_ref[:, :OFF_CQ] = _dot_nt(hb_ref[...], win_ref[:OFF_CQ, :])
    lane = lax.broadcasted_iota(jnp.int32, (1, LANES), 1)
    lane_lo = (lane % 16) < 8
    if rope:
        cos = cos_ref[...]
        sin = sin_ref[...]

    uv = jax.nn.gelu(z_ref[:, OFF_UV:OFF_UV + 2 * SGU_WIDTH])
    u = uv[:, :SGU_WIDTH]
    vn = _rms(uv[:, SGU_WIDTH:], gsgu_ref[...])
    even = _half_mask(0)
    for ch in range(tm // SGU_CHUNK):
        rows = slice(ch * SGU_CHUNK, (ch + 1) * SGU_CHUNK)
        for p in range(SGU_GROUPS // 2):
            sl = slice(p * LANES, (p + 1) * LANES)
            vc = vn[rows, sl]
            mixed = (_dot(wsgu_ref[2 * p], jnp.where(even, vc, 0.0).astype(BF16))
                     + _dot(wsgu_ref[2 * p + 1], jnp.where(even, 0.0, vc).astype(BF16))
                     + bsg_ref[:, sl])
            oc_ref[rows, sl] = (u[rows, sl] * mixed).astype(BF16)

    cq = z_ref[:, OFF_CQ:OFF_CQ + MLA_Q_RANK]
    cqn = _rms(cq, gcq_ref[...]).astype(BF16)
    qm = _dot(cqn, wuq_ref[...])
    for hd in range(MLA_HEADS):
        qh = qm[:, hd * LANES:(hd + 1) * LANES]
        if rope:
            qh = _rope(qh, cos, sin, lane_lo)
        qm_ref[hd] = (qh * MLA_QSCALE).astype(BF16)

    ckv = z_ref[:, OFF_CKV:OFF_CKV + MLA_KV_RANK]
    ckvn = _rms(ckv, gckv_ref[...])
    kr = z_ref[:, OFF_KR:OFF_KR + LANES]
    if emit_f32:
        for sq in range(tm // SEQ):
            ckvf_ref[sq] = ckvn[sq * SEQ:(sq + 1) * SEQ]
        krf_ref[...] = kr
    if rope:
        kr = _rope(kr, cos, sin, lane_lo)
    kv = _dot(ckvn.astype(BF16), wukv_ref[...])
    for hd in range(MLA_HEADS):
        kp_ref[hd] = (kv[:, hd * LANES:(hd + 1) * LANES] + kr).astype(BF16)
    voff = MLA_HEADS * LANES
    for p in range(MLA_HEADS // 2):
        vp_ref[p] = kv[:, voff + p * LANES:voff + (p + 1) * LANES].astype(BF16)

    qa = z_ref[:, OFF_QA:OFF_QA + NA_WIDTH] * NA_QSCALE
    ka = z_ref[:, OFF_KA:OFF_KA + NA_WIDTH]
    va = z_ref[:, OFF_VA:OFF_VA + NA_WIDTH]
    for p in range(NA_HEADS // 2):
        sl = slice(p * LANES, (p + 1) * LANES)
        qa_ref[p] = qa[:, sl].astype(BF16)
        ka_ref[p] = ka[:, sl].astype(BF16)
        va_ref[p] = va[:, sl].astype(BF16)
    if emit_f32:
        for src, dst in ((ka, kaf_ref), (va, vaf_ref)):
            src_t = src.T
            for sq in range(tm // SEQ):
                for hd in range(NA_HEADS):
                    dst[sq, hd] = src_t[hd * HEAD_DIM:(hd + 1) * HEAD_DIM, sq * SEQ:(sq + 1) * SEQ]


def _project(x, mods, mod_row, pw, l, rope_tabs, stacked, tiles_per_seq):
    emit_f32 = stacked is not None
    t = x.shape[0]
    tm = PROJ_TM
    nt = t // tm
    rope = rope_tabs is not None
    in_specs = [
        pl.BlockSpec((tm, D_MODEL), lambda i: (i, 0)),
        pl.BlockSpec((None, 6, D_MODEL), lambda i: (mod_row(i), 0, 0)),
        _layer_spec(l, 1, D_MODEL),
        _layer_spec(l, IN_COLS, D_MODEL),
        _layer_spec(l, 1, MLA_Q_RANK),
        _layer_spec(l, MLA_Q_RANK, MLA_HEADS * LANES),
        _layer_spec(l, 1, MLA_KV_RANK),
        _layer_spec(l, MLA_KV_RANK, KV_COLS),
        _layer_spec(l, 1, SGU_WIDTH),
        _layer_spec(l, SGU_GROUPS, SGU_CHUNK, SGU_CHUNK),
        _layer_spec(l, SGU_CHUNK, SGU_WIDTH),
    ]
    args = [x, mods, pw["g_mix"], pw["w_in"], pw["g_cq"], pw["w_uq"], pw["g_ckv"], pw["w_ukv"],
            pw["g_sgu"], pw["w_sgu"], pw["b_sgu"]]
    if rope:
        tab_spec = pl.BlockSpec((tm, LANES), lambda i: (i % tiles_per_seq, 0))
        in_specs += [tab_spec, tab_spec]
        args += list(rope_tabs)

    def heads(n):
        return (pl.BlockSpec((n, tm, LANES), lambda i: (0, i, 0)),
                jax.ShapeDtypeStruct((n, t, LANES), BF16))

    def flat(w, dt):
        return (pl.BlockSpec((tm, w), lambda i: (i, 0)), jax.ShapeDtypeStruct((t, w), dt))

    outs = [heads(2), heads(2), heads(2), heads(MLA_HEADS), heads(MLA_HEADS), heads(MLA_HEADS // 2),
            flat(SGU_WIDTH, BF16)]
    def layer_slab(tail):
        zeros = (0,) * len(tail)
        return pl.BlockSpec((tm // SEQ, None) + tail, lambda i: (i, l) + zeros)

    aliases = {}
    if emit_f32:
        for a in stacked:
            aliases[len(args)] = len(outs)
            in_specs.append(pl.BlockSpec(memory_space=pl.ANY))
            args.append(a)
            outs.append((layer_slab(a.shape[2:]), jax.ShapeDtypeStruct(a.shape, F32)))
        outs.append(flat(LANES, F32))
    return pl.pallas_call(
        functools.partial(_proj_kernel, rope, emit_f32),
        grid=(nt,),
        in_specs=in_specs,
        out_specs=[o[0] for o in outs],
        out_shape=[o[1] for o in outs],
        input_output_aliases=aliases,
        scratch_shapes=[pltpu.VMEM((tm, D_MODEL), BF16), pltpu.VMEM((tm, IN_COLS), F32)],
        compiler_params=_cparams(1),
        name="project_lat" if rope else "project_ctx",
    )(*args)


def _cache_kv_kernel(ckv_ref, kr_ref, wukv_ref, kp_ref, vp_ref):
    rows = kp_ref.shape[1]
    kv = _dot(ckv_ref[...].reshape(rows, MLA_KV_RANK).astype(BF16), wukv_ref[...])
    kr = kr_ref[...].reshape(rows, LANES)
    for hd in range(MLA_HEADS):
        kp_ref[hd] = (kv[:, hd * LANES:(hd + 1) * LANES] + kr).astype(BF16)
    voff = MLA_HEADS * LANES
    for p in range(MLA_HEADS // 2):
        vp_ref[p] = kv[:, voff + p * LANES:voff + (p + 1) * LANES].astype(BF16)


def _cache_kv(cache_ckv, cache_kr_pad, w_ukv):
    b = cache_ckv.shape[0]
    return pl.pallas_call(
        _cache_kv_kernel,
        grid=(DEPTH,),
        in_specs=[
            pl.BlockSpec((b, None, PAST_LEN, MLA_KV_RANK), lambda l: (0, l, 0, 0)),
            pl.BlockSpec((b, None, PAST_LEN, LANES), lambda l: (0, l, 0, 0)),
            pl.BlockSpec((None, MLA_KV_RANK, KV_COLS), lambda l: (l, 0, 0)),
        ],
        out_specs=[
            pl.BlockSpec((None, MLA_HEADS, b * PAST_LEN, LANES), lambda l: (l, 0, 0, 0)),
            pl.BlockSpec((None, MLA_HEADS // 2, b * PAST_LEN, LANES), lambda l: (l, 0, 0, 0)),
        ],
        out_shape=[
            jax.ShapeDtypeStruct((DEPTH, MLA_HEADS, b * PAST_LEN, LANES), BF16),
            jax.ShapeDtypeStruct((DEPTH, MLA_HEADS // 2, b * PAST_LEN, LANES), BF16),
        ],
        compiler_params=_cparams(1),
        name="cache_kv",
    )(cache_ckv, cache_kr_pad, w_ukv)


def _ctx_attn_kernel(qa_ref, ka_ref, va_ref, qm_ref, kp_ref, vp_ref, o_ref):
    for sq in range(CTX_SEQS):
        rows = slice(sq * SEQ, (sq + 1) * SEQ)
        for p in range(NA_HEADS // 2 + MLA_HEADS // 2):
            out = None
            for half in range(2):
                own = _half_mask(half)
                if p < NA_HEADS // 2:
                    q = _keep(own, qa_ref[p, rows, :])
                    k = ka_ref[p, rows, :]
                    v = va_ref[p, rows, :]
                else:
                    hd = 2 * (p - NA_HEADS // 2) + half
                    q = qm_ref[hd, rows, :]
                    k = kp_ref[hd, rows, :]
                    v = vp_ref[hd // 2, rows, :]
                s = _dot_nt(q, k)
                p_ = jnp.exp2(s - jnp.max(s, axis=-1, keepdims=True))
                o = _dot(p_.astype(BF16), _keep(own, v)) / jnp.sum(p_, axis=-1, keepdims=True)
                out = o if out is None else out + o
            o_ref[rows, p * LANES:(p + 1) * LANES] = out.astype(BF16)


def _ctx_attention(qa, ka, va, qm, kp, vp, n_seq):
    rows = CTX_SEQS * SEQ
    heads = lambda n: pl.BlockSpec((n, rows, LANES), lambda b: (0, b, 0))
    width = NA_WIDTH + MLA_WIDTH
    return pl.pallas_call(
        _ctx_attn_kernel,
        grid=(n_seq // CTX_SEQS,),
        in_specs=[heads(2), heads(2), heads(2), heads(MLA_HEADS), heads(MLA_HEADS), heads(MLA_HEADS // 2)],
        out_specs=pl.BlockSpec((rows, width), lambda b: (b, 0)),
        out_shape=jax.ShapeDtypeStruct((n_seq * SEQ, width), BF16),
        compiler_params=_cparams(1),
        name="ctx_attention",
    )(qa, ka, va, qm, kp, vp)


def _lat_mla_kernel(q_ref, kc_ref, kl_ref, vc_ref, vl_ref, o_ref, s_ref, m_ref, p_ref):
    nc = kc_ref.shape[1]
    outs = {}

    def scores(hd):
        q = q_ref[hd]
        sc = _dot_nt(q, kc_ref[hd])
        sl = _dot_nt(q, kl_ref[hd])
        s_ref[hd % 2, :, 0:nc] = sc
        s_ref[hd % 2, :, nc:] = sl
        m_ref[hd % 2] = jnp.maximum(_lane_block_max(sc), _lane_block_max(sl))

    def softmax(hd):
        m = jnp.max(m_ref[hd % 2], axis=-1, keepdims=True)
        p_ref[hd % 2] = jnp.exp2(s_ref[hd % 2] - m).astype(BF16)

    def values(hd):
        own = _half_mask(hd % 2)
        o = _normalised(own, hd % 2, [_dot(p_ref[hd % 2, :, 0:nc], _with_ones(own, hd % 2, vc_ref[hd // 2])),
                                      _dot(p_ref[hd % 2, :, nc:], _with_ones(own, hd % 2, vl_ref[hd // 2]))])
        if hd % 2 == 0:
            outs[hd // 2] = o
        else:
            o_ref[:, (hd // 2) * LANES:(hd // 2 + 1) * LANES] = (outs.pop(hd // 2) + o).astype(BF16)

    for t in range(MLA_HEADS + 2):
        if t < MLA_HEADS:
            scores(t)
        if 0 <= t - 1 < MLA_HEADS:
            softmax(t - 1)
        if 0 <= t - 2 < MLA_HEADS:
            values(t - 2)


def _lat_mla(q, kc, vc, kl, vl, l, n_batch):
    nq = DEC_SEQ // TM
    n_keys = PAST_LEN + DEC_SEQ
    return pl.pallas_call(
        _lat_mla_kernel,
        grid=(n_batch, nq),
        in_specs=[pl.BlockSpec((MLA_HEADS, TM, LANES), lambda b, i: (0, b * nq + i, 0)),
                  pl.BlockSpec((None, MLA_HEADS, PAST_LEN, LANES), lambda b, i: (l, 0, b, 0)),
                  pl.BlockSpec((MLA_HEADS, DEC_SEQ, LANES), lambda b, i: (0, b, 0)),
                  pl.BlockSpec((None, MLA_HEADS // 2, PAST_LEN, LANES), lambda b, i: (l, 0, b, 0)),
                  pl.BlockSpec((MLA_HEADS // 2, DEC_SEQ, LANES), lambda b, i: (0, b, 0))],
        out_specs=pl.BlockSpec((TM, MLA_WIDTH), lambda b, i: (b * nq + i, 0)),
        out_shape=jax.ShapeDtypeStruct((n_batch * DEC_SEQ, MLA_WIDTH), BF16),
        scratch_shapes=[pltpu.VMEM((2, TM, n_keys), F32), pltpu.VMEM((2, TM, LANES), F32),
                        pltpu.VMEM((2, TM, n_keys), BF16)],
        compiler_params=_cparams(2),
        name="lat_mla",
    )(q, kc, kl, vc, vl)


def _na_bias_kernel(rpb_ref, o_ref, tab_ref):
    l = pl.program_id(0)
    hd = pl.program_id(1)
    n_dr = 2 * NA_WIN_R - 1
    n_dc = 2 * NA_WIN_C - 1
    base = (l * NA_HEADS + hd) * n_dr * n_dc
    qc = lax.broadcasted_iota(jnp.int32, (GRID_W, LANES), 0)
    kc = lax.broadcasted_iota(jnp.int32, (GRID_W, LANES), 1) % GRID_W
    cs = jnp.clip(qc - NA_WIN_C // 2, 0, GRID_W - NA_WIN_C)
    in_win = (kc >= cs) & (kc < cs + NA_WIN_C)
    neg = jnp.full((GRID_W, LANES), NEG_INF * LOG2E, F32)
    lane = lax.broadcasted_iota(jnp.int32, (GRID_W, LANES), 1)
    for dr in range(n_dr):
        vec = neg
        for dc in range(n_dc):
            vec = jnp.where(lane == dc, rpb_ref[base + dr * n_dc + dc] * LOG2E, vec)
        rot = pltpu.roll(vec, LANES - (NA_WIN_C - 1), axis=1, stride=1, stride_axis=0)
        both = jnp.where(lane < GRID_W, rot, pltpu.roll(rot, GRID_W, axis=1))
        tab_ref[dr] = jnp.where(in_win, both, neg)
    left = lax.broadcasted_iota(jnp.int32, (GRID_W, LANES), 1) < GRID_W
    n_krows = NA_KBLKS * NA_QBLK // GRID_W
    cases = ((0, lambda qr: 0), (-NA_QROWS, lambda qr: qr), (-2 * NA_QROWS, lambda qr: NA_QROWS))
    for c, (shift, first) in enumerate(cases):
        for qr in range(NA_QROWS):
            for kp in range(n_krows // 2):
                tiles = []
                for kr in (2 * kp, 2 * kp + 1):
                    ok = first(qr) <= kr < first(qr) + NA_WIN_R
                    tiles.append(tab_ref[kr - qr + shift + NA_WIN_R - 1] if ok else neg)
                o_ref[c, qr * GRID_W:(qr + 1) * GRID_W, kp * LANES:(kp + 1) * LANES] = (
                    jnp.where(left, tiles[0], tiles[1]))


def _na_bias(na_rpb):
    n_k = NA_KBLKS * NA_QBLK
    return pl.pallas_call(
        _na_bias_kernel,
        grid=(DEPTH, NA_HEADS),
        in_specs=[pl.BlockSpec(memory_space=pltpu.SMEM)],
        out_specs=pl.BlockSpec((None, 3, None, NA_QBLK, n_k), lambda l, h: (l, 0, h, 0, 0)),
        out_shape=jax.ShapeDtypeStruct((DEPTH, 3, NA_HEADS, NA_QBLK, n_k), F32),
        scratch_shapes=[pltpu.VMEM((2 * NA_WIN_R - 1, GRID_W, LANES), F32)],
        compiler_params=_cparams(2),
        name="na_bias",
    )(na_rpb.reshape(-1))


def _lane_block_max(s):
    return functools.reduce(jnp.maximum, [s[:, j * LANES:(j + 1) * LANES] for j in range(s.shape[1] // LANES)])


def _na_kernel(n_sub, q_ref, *refs):
    per = 2 * NA_KBLKS + 1
    subs = [refs[r * per:(r + 1) * per] for r in range(n_sub)]
    kc_ref, vc_ref, o_ref, s_ref, m_ref, p_ref = refs[n_sub * per:]
    items = [(r, hd) for r in range(n_sub) for hd in range(NA_HEADS)]
    outs = {}
    n_loc = NA_KBLKS * NA_QBLK

    def scores(t):
        r, hd = items[t]
        k_refs, bias_ref = subs[r][:NA_KBLKS], subs[r][-1]
        p = hd // 2
        q = _keep(_half_mask(hd % 2), q_ref[p, r * NA_QBLK:(r + 1) * NA_QBLK, :])
        m = None
        for i in range(NA_KBLKS):
            s = _dot_nt(q, k_refs[i][p]) + bias_ref[hd, :, i * NA_QBLK:(i + 1) * NA_QBLK]
            s_ref[t % 2, :, i * NA_QBLK:(i + 1) * NA_QBLK] = s
            m = _lane_block_max(s) if m is None else jnp.maximum(m, _lane_block_max(s))
        s = _dot(q, kc_ref[p * LANES:(p + 1) * LANES, :].astype(BF16))
        s_ref[t % 2, :, n_loc:] = s
        m_ref[t % 2] = jnp.maximum(m, _lane_block_max(s))

    def softmax(t):
        m = jnp.max(m_ref[t % 2], axis=-1, keepdims=True)
        p_ref[t % 2] = jnp.exp2(s_ref[t % 2] - m).astype(BF16)

    def values(t):
        r, hd = items[t]
        v_refs = subs[r][NA_KBLKS:2 * NA_KBLKS]
        p = hd // 2
        own = _half_mask(hd % 2)
        row = lax.broadcasted_iota(jnp.int32, (LANES, 1), 0)
        vc = vc_ref[p * LANES:(p + 1) * LANES, :].astype(BF16)
        vc = jnp.where((row // HALF) == hd % 2, vc, (row == HALF * (1 - hd % 2)).astype(BF16))
        parts = [_dot_nt(p_ref[t % 2, :, n_loc:], vc)]
        for i in range(NA_KBLKS):
            parts.append(_dot(p_ref[t % 2, :, i * NA_QBLK:(i + 1) * NA_QBLK],
                              _with_ones(own, hd % 2, v_refs[i][p])))
        o = _normalised(own, hd % 2, parts)
        if hd % 2 == 0:
            outs[(r, p)] = o
        else:
            o_ref[r * NA_QBLK:(r + 1) * NA_QBLK, p * LANES:(p + 1) * LANES] = (outs.pop((r, p)) + o).astype(BF16)

    for t in range(len(items) + 2):
        if t < len(items):
            scores(t)
        if 0 <= t - 1 < len(items):
            softmax(t - 1)
        if 0 <= t - 2 < len(items):
            values(t - 2)


def _na_attention(qa, ka, va, cache_k, cache_v, bias, l, n_batch):
    nblk = DEC_SEQ // NA_QBLK
    max_start = nblk - NA_KBLKS
    n_sub = NA_SUBS
    nstep = nblk // n_sub
    blk = lambda jj, r: jj * n_sub + r

    def kspec(r, i):
        return pl.BlockSpec((NA_HEADS // 2, NA_QBLK, LANES),
                            lambda jj, b: (0, b * nblk + jnp.clip(blk(jj, r) - 1, 0, max_start) + i, 0))

    case = lambda j: jnp.where(j == 0, 0, jnp.where(j == nblk - 1, 2, 1))
    cache_spec = pl.BlockSpec((None, None, NA_WIDTH, PAST_LEN), lambda jj, b: (b, l, 0, 0))
    in_specs = [pl.BlockSpec((NA_HEADS // 2, n_sub * NA_QBLK, LANES), lambda jj, b: (0, b * nstep + jj, 0))]
    args = [qa]
    for r in range(n_sub):
        in_specs += [kspec(r, i) for i in range(NA_KBLKS)] + [kspec(r, i) for i in range(NA_KBLKS)]
        in_specs.append(pl.BlockSpec((None, None, NA_HEADS, NA_QBLK, NA_KBLKS * NA_QBLK),
                                     lambda jj, b, r=r: (l, case(blk(jj, r)), 0, 0, 0)))
        args += [ka] * NA_KBLKS + [va] * NA_KBLKS + [bias]
    in_specs += [cache_spec, cache_spec]
    args += [cache_k, cache_v]
    return pl.pallas_call(
        functools.partial(_na_kernel, n_sub),
        grid=(nstep, n_batch),
        in_specs=in_specs,
        out_specs=pl.BlockSpec((n_sub * NA_QBLK, NA_WIDTH), lambda jj, b: (b * nstep + jj, 0)),
        out_shape=jax.ShapeDtypeStruct((n_batch * DEC_SEQ, NA_WIDTH), BF16),
        scratch_shapes=[pltpu.VMEM((2, NA_QBLK, (NA_KBLKS + 1) * NA_QBLK), F32),
                        pltpu.VMEM((2, NA_QBLK, LANES), F32),
                        pltpu.VMEM((2, NA_QBLK, (NA_KBLKS + 1) * NA_QBLK), BF16)],
        compiler_params=_cparams(2),
        name="na_attention",
    )(*args)


def _ffn_kernel(seq_len, final, widths, *refs):
    halo = seq_len > TM
    n_in = 3 if halo else 1
    x_refs = refs[:n_in]
    refs = refs[n_in:]
    part_refs = [refs[i * n_in:(i + 1) * n_in] for i in range(len(widths))]
    refs = refs[n_in * len(widths):]
    mod_ref, gffn_ref, wo_ref, win_ref, cw_ref, cb_ref, wout_ref = refs[:7]
    refs = refs[7:]
    if final:
        gfin_ref = refs[0]
        refs = refs[1:]
    o_ref, mix_ref, hext_ref, a_ref, act_ref = refs
    tm = o_ref.shape[0]
    pad = BF16_ROWS
    if halo:
        tiles_per_seq = seq_len // tm
        pos = pl.program_id(0) % tiles_per_seq
        pieces = [(0, 0, slice(0, pad), pos != 0), (pad, 1, slice(0, tm), None),
                  (pad + tm, 2, slice(0, pad), pos != tiles_per_seq - 1)]
        frames = []
        starts = [pad]
        seg_len = tm
    else:
        nseg = tm // seq_len
        starts = [pad + s * (seq_len + pad) for s in range(nseg)]
        seg_len = seq_len
        pieces = [(starts[s], 0, slice(s * seq_len, (s + 1) * seq_len), None) for s in range(nseg)]
        frames = [s * (seq_len + pad) for s in range(nseg + 1)]
    for f in frames:
        mix_ref[f:f + pad] = jnp.zeros((pad, D_MODEL), BF16)
        hext_ref[f:f + pad] = jnp.zeros((pad, D_MODEL), BF16)
    off = 0
    for n, prefs in zip(widths, part_refs):
        for row, src, rows, _ in pieces:
            mix_ref[row:row + rows.stop - rows.start, off:off + n] = prefs[src][rows, :]
        off += n
    r = _dot(mix_ref[...], wo_ref[...])
    g1 = mod_ref[2:3, :]
    scale = gffn_ref[...] * (1.0 + mod_ref[4:5, :])
    shift = mod_ref[3:4, :]
    for row, src, rows, keep in pieces:
        n_rows = rows.stop - rows.start
        x1 = x_refs[src][rows, :] + g1 * r[row:row + n_rows]
        if src == n_in // 2:
            o_ref[rows, :] = x1
        h2 = (x1 * lax.rsqrt(jnp.mean(x1 * x1, axis=-1, keepdims=True) + EPS) * scale + shift).astype(BF16)
        if keep is not None:
            h2 = jnp.where(keep, h2, jnp.zeros_like(h2))
        hext_ref[row:row + n_rows] = h2

    def col(c, lb):
        half = FF_LB // 2
        return (lb // half) * D_FF + c * FF_CHUNK + (lb % half) * LANES

    def up(c):
        hext = hext_ref[...]
        for part in range(2):
            a = _dot(hext, win_ref[:, part * D_FF + c * FF_CHUNK:part * D_FF + (c + 1) * FF_CHUNK])
            for j in range(FF_LB // 2):
                a_ref[c % 2, part * (FF_LB // 2) + j] = a[:, j * LANES:(j + 1) * LANES]

    def conv(c, lb, st):
        cs = slice(col(c, lb), col(c, lb) + LANES)
        return (a_ref[c % 2, lb, pl.ds(st - 1, seg_len, stride=1), :] * cw_ref[0:1, cs]
                + a_ref[c % 2, lb, st:st + seg_len, :] * cw_ref[1:2, cs]
                + a_ref[c % 2, lb, pl.ds(st + 1, seg_len, stride=1), :] * cw_ref[2:3, cs] + cb_ref[:, cs])

    up(0)
    for c in range(N_FF_CHUNKS):
        if c + 1 < N_FF_CHUNKS:
            up(c + 1)
        for s, st in enumerate(starts):
            for lb in range(FF_LB // 2):
                gate = conv(c, lb, st)
                val = conv(c, lb + FF_LB // 2, st)
                act_ref[s * seg_len:(s + 1) * seg_len, c * FF_CHUNK + lb * LANES:c * FF_CHUNK + (lb + 1) * LANES] = (
                    gate * jax.nn.sigmoid(gate) * val).astype(BF16)
    y = o_ref[...] + mod_ref[5:6, :] * _dot(act_ref[...], wout_ref[...])
    if final:
        y = _rms(y, gfin_ref[...])
    o_ref[...] = y


def _ffn(x, parts, mods, mod_row, pw, l, seq_len, g_final, name):
    t = x.shape[0]
    nt = t // TM
    hb = TM // BF16_ROWS
    n_hblk = t // BF16_ROWS
    final = g_final is not None
    halo = seq_len > TM
    ext = TM + 2 * BF16_ROWS if halo else TM + (TM // seq_len + 1) * BF16_ROWS
    widths = tuple(p.shape[1] for p in parts)

    def tiled(a):
        w = a.shape[1]
        main = pl.BlockSpec((TM, w), lambda i: (i, 0))
        if not halo:
            return [main], [a]
        prev = pl.BlockSpec((BF16_ROWS, w), lambda i: (jnp.maximum(i * hb - 1, 0), 0))
        nxt = pl.BlockSpec((BF16_ROWS, w), lambda i: (jnp.minimum((i + 1) * hb, n_hblk - 1), 0))
        return [prev, main, nxt], [a, a, a]

    in_specs, args = [], []
    for a in (x,) + tuple(parts):
        sp, ar = tiled(a)
        in_specs += sp
        args += ar
    in_specs += [pl.BlockSpec((None, 6, D_MODEL), lambda i: (mod_row(i), 0, 0)),
                 _layer_spec(l, 1, D_MODEL),
                 _layer_spec(l, D_MODEL, D_MODEL, single=True),
                 _layer_spec(l, D_MODEL, 2 * D_FF, single=True),
                 _layer_spec(l, 3, 2 * D_FF, single=True),
                 _layer_spec(l, 1, 2 * D_FF, single=True),
                 _layer_spec(l, D_FF, D_MODEL, single=True)]
    args += [mods, pw["g_ffn"], pw["w_out"], pw["w_ffn_in"], pw["conv_w"], pw["conv_b"], pw["w_ffn_out"]]
    if final:
        in_specs.append(pl.BlockSpec((1, D_MODEL), lambda i: (0, 0)))
        args.append(g_final)
    return pl.pallas_call(
        functools.partial(_ffn_kernel, seq_len, final, widths),
        grid=(nt,),
        in_specs=in_specs,
        out_specs=pl.BlockSpec((TM, D_MODEL), lambda i: (i, 0)),
        out_shape=jax.ShapeDtypeStruct((t, D_MODEL), F32),
        scratch_shapes=[pltpu.VMEM((ext, D_MODEL), BF16), pltpu.VMEM((ext, D_MODEL), BF16),
                        pltpu.VMEM((2, FF_LB, ext, LANES), F32), pltpu.VMEM((TM, D_FF), BF16)],
        compiler_params=_cparams(1),
        name=name,
    )(*args)


def _pack_mla_kernel(wuq_ref, wukv_ref, ouq_ref, oukv_ref):
    dq = MLA_NOPE + MLA_ROPE
    zq = jnp.zeros((MLA_Q_RANK, LANES - dq), F32)
    zk = jnp.zeros((MLA_KV_RANK, LANES - MLA_NOPE), F32)
    dkv = MLA_NOPE + MLA_V
    for hd in range(MLA_HEADS):
        ouq_ref[:, hd * LANES:(hd + 1) * LANES] = jnp.concatenate(
            [wuq_ref[:, hd * dq:(hd + 1) * dq], zq], axis=-1).astype(BF16)
        oukv_ref[:, hd * LANES:(hd + 1) * LANES] = jnp.concatenate(
            [wukv_ref[:, hd * dkv:hd * dkv + MLA_NOPE], zk], axis=-1).astype(BF16)
    voff = MLA_HEADS * LANES
    for p in range(MLA_HEADS // 2):
        oukv_ref[:, voff + p * LANES:voff + (p + 1) * LANES] = jnp.concatenate(
            [wukv_ref[:, (2 * p + j) * dkv + MLA_NOPE:(2 * p + j + 1) * dkv] for j in range(2)], axis=-1).astype(BF16)


def _pack_weights(w_in, w_uq, w_ukv, w_sgu, b_sgu, w_out, w_ffn_in, ffn_conv_w, ffn_conv_b, w_ffn_out,
                  g_mix, g_cq, g_ckv, g_sgu, g_ffn):
    nl = w_in.shape[0]
    wt = jnp.swapaxes(w_in, 1, 2).astype(BF16)
    zr = lambda n: jnp.zeros((nl, n, D_MODEL), BF16)
    w_in_p = jnp.concatenate([wt[:, :OFF_KR], zr(HALF), wt[:, OFF_KR:OFF_KR + MLA_ROPE],
                              zr(LANES - HALF - MLA_ROPE), wt[:, OFF_KR + MLA_ROPE:]], axis=1)
    w_uq_p, w_ukv_p = pl.pallas_call(
        _pack_mla_kernel,
        grid=(nl,),
        in_specs=[pl.BlockSpec((None,) + w_uq.shape[1:], lambda l: (l, 0, 0)),
                  pl.BlockSpec((None,) + w_ukv.shape[1:], lambda l: (l, 0, 0))],
        out_specs=[pl.BlockSpec((None, MLA_Q_RANK, MLA_HEADS * LANES), lambda l: (l, 0, 0)),
                   pl.BlockSpec((None, MLA_KV_RANK, KV_COLS), lambda l: (l, 0, 0))],
        out_shape=[jax.ShapeDtypeStruct((nl, MLA_Q_RANK, MLA_HEADS * LANES), BF16),
                   jax.ShapeDtypeStruct((nl, MLA_KV_RANK, KV_COLS), BF16)],
        compiler_params=_cparams(1),
        name="pack_w_mla",
    )(w_uq, w_ukv)
    b_sgu_p = jnp.repeat(jnp.swapaxes(b_sgu, 1, 2), SGU_WIDTH // SGU_GROUPS, axis=-1)
    return dict(
        w_in=w_in_p, w_uq=w_uq_p, w_ukv=w_ukv_p, w_sgu=w_sgu.astype(BF16), b_sgu=b_sgu_p,
        w_out=w_out.astype(BF16), w_ffn_in=w_ffn_in.astype(BF16), conv_w=ffn_conv_w,
        conv_b=ffn_conv_b[:, None, :], w_ffn_out=w_ffn_out.astype(BF16),
        g_mix=g_mix[:, None, :], g_cq=g_cq[:, None, :], g_ckv=g_ckv[:, None, :], g_sgu=g_sgu[:, None, :],
        g_ffn=g_ffn[:, None, :])


def _rope_tables(n_tokens):
    t = jnp.arange(n_tokens)
    n_freq = MLA_ROPE // 4
    inv_freq = ROPE_THETA ** (-jnp.arange(n_freq, dtype=F32) / n_freq)
    ang_r = (t // GRID_W).astype(F32)[:, None] * inv_freq
    ang_c = (t % GRID_W).astype(F32)[:, None] * inv_freq
    ones = jnp.ones((n_tokens, HALF), F32)
    tail = LANES - HALF - MLA_ROPE
    cos = jnp.concatenate([ones, jnp.cos(ang_r), jnp.cos(ang_r), jnp.cos(ang_c), jnp.cos(ang_c),
                           ones[:, :tail]], axis=-1)
    sin = jnp.concatenate([0 * ones, -jnp.sin(ang_r), jnp.sin(ang_r), -jnp.sin(ang_c), jnp.sin(ang_c),
                           0 * ones[:, :tail]], axis=-1)
    return cos, sin


def kernel(x_prompt, x_sample, cache_na_k, cache_na_v, cache_mla_ckv, cache_mla_krope, c, c_ctx, w_mod, b_mod,
           g_mix, w_in, na_rpb, g_cq, w_uq, g_ckv, w_ukv, g_sgu, w_sgu, b_sgu, w_out, g_ffn, w_ffn_in,
           ffn_conv_w, ffn_conv_b, w_ffn_out, g_final):
    n_ctx, n_lat = x_prompt.shape[0], x_sample.shape[0]
    t_ctx, t_lat = n_ctx * SEQ, n_lat * DEC_SEQ
    pw = _pack_weights(w_in, w_uq, w_ukv, w_sgu, b_sgu, w_out, w_ffn_in, ffn_conv_w, ffn_conv_b,
                       w_ffn_out, g_mix, g_cq, g_ckv, g_sgu, g_ffn)
    g_fin = g_final[None, :]

    cond = jnp.concatenate([c_ctx[None, :], c, jnp.zeros((8 - 1 - n_lat, D_MODEL), F32)], axis=0)
    mods = _modulation(cond, w_mod, b_mod)
    rope_tabs = _rope_tables(DEC_SEQ)
    na_bias = _na_bias(na_rpb)
    kr_pad = jnp.pad(cache_mla_krope, ((0, 0), (0, 0), (0, 0), (HALF, LANES - HALF - MLA_ROPE)))
    cache_kp, cache_vp = _cache_kv(cache_mla_ckv, kr_pad, pw["w_ukv"])
    channel_major = lambda a: jnp.transpose(a, (0, 1, 3, 4, 2)).reshape(n_lat, DEPTH, NA_WIDTH, PAST_LEN)
    cache_k = channel_major(cache_na_k)
    cache_v = channel_major(cache_na_v)

    lat_tiles = DEC_SEQ // TM
    proj_tiles = DEC_SEQ // PROJ_TM
    ctx_row = lambda i: 0
    lat_row = lambda i: 1 + i // lat_tiles
    lat_proj_row = lambda i: 1 + i // proj_tiles

    xp = x_prompt.reshape(t_ctx, D_MODEL)
    xs = x_sample.reshape(t_lat, D_MODEL)
    per_head = (n_ctx, DEPTH, NA_HEADS, HEAD_DIM, SEQ)
    ctx_out = (jnp.zeros(per_head, F32), jnp.zeros(per_head, F32), jnp.zeros((n_ctx, DEPTH, SEQ, MLA_KV_RANK), F32))
    new_kr = []
    for l in range(DEPTH):
        last = l == DEPTH - 1
        m = mods[l]
        qa, ka, va, qm, kp, vp, oc, *ctx_out, kr_f = _project(xp, m, ctx_row, pw, l, None, ctx_out, 1)
        new_kr.append(kr_f[:, HALF:HALF + MLA_ROPE])
        o_ab = _ctx_attention(qa, ka, va, qm, kp, vp, n_ctx)
        xp = _ffn(xp, (o_ab, oc), m, ctx_row, pw, l, SEQ, g_fin if last else None, "ctx_ffn")

        qa, ka, va, qm, kp, vp, oc = _project(xs, m, lat_proj_row, pw, l, rope_tabs, None, proj_tiles)
        o_a = _na_attention(qa, ka, va, cache_k, cache_v, na_bias, l, n_lat)
        o_b = _lat_mla(qm, cache_kp, cache_vp, kp, vp, l, n_lat)
        xs = _ffn(xs, (o_a, o_b, oc), m, lat_row, pw, l, DEC_SEQ, g_fin if last else None, "lat_ffn")

    token_major = lambda a: jnp.transpose(a, (0, 1, 4, 2, 3))
    new_k, new_v, new_ckv = ctx_out
    return (xp.reshape(n_ctx, SEQ, D_MODEL), xs.reshape(n_lat, DEC_SEQ, D_MODEL),
            token_major(new_k), token_major(new_v), new_ckv,
            jnp.stack([a.reshape(n_ctx, SEQ, MLA_ROPE) for a in new_kr], axis=1))
```

```python
import functools
import math

import jax
import jax.numpy as jnp
from jax import lax
from jax.experimental import pallas as pl
from jax.experimental.pallas import tpu as pltpu

F32 = jnp.float32
BF16 = jnp.bfloat16

D_MODEL = 1024
DEPTH = 4
SEQ = 256
DEC_SEQ = 2048
PAST_LEN = 256
GRID_W = 64
HEAD_DIM = 64
NA_WIDTH = 256
NA_HEADS = 4
NA_WIN_R = 8
NA_WIN_C = 16
MLA_HEADS = 8
MLA_NOPE = 64
MLA_ROPE = 32
MLA_V = 64
MLA_WIDTH = MLA_HEADS * MLA_V
MLA_Q_RANK = 384
MLA_KV_RANK = 256
SGU_WIDTH = 256
SGU_GROUPS = 4
SGU_CHUNK = 128
D_FF = 2816
ROPE_THETA = 10000.0
EPS = 1e-6
NEG_INF = -1e30
LOG2E = math.log2(math.e)
NA_QSCALE = HEAD_DIM ** -0.5 * LOG2E
MLA_QSCALE = (MLA_NOPE + MLA_ROPE) ** -0.5 * LOG2E

LANES = 128
BF16_ROWS = 16
HALF = LANES // 2

OFF_QA, OFF_KA, OFF_VA = 0, 256, 512
OFF_CQ = 768
OFF_CKV = OFF_CQ + MLA_Q_RANK
OFF_KR = OFF_CKV + MLA_KV_RANK
OFF_UV = OFF_KR + LANES
IN_COLS = OFF_UV + 2 * SGU_WIDTH
KV_COLS = MLA_HEADS * LANES + MLA_HEADS * MLA_V

FF_CHUNK = 256
N_FF_CHUNKS = D_FF // FF_CHUNK
FF_LB = 2 * FF_CHUNK // LANES

TM = 512
PROJ_TM = 1024
CTX_SEQS = 8
NA_SUBS = 8
NA_QROWS = 4
NA_QBLK = NA_QROWS * GRID_W
NA_KBLKS = 3
VMEM_LIMIT = 56 * 1024 * 1024


def _cparams(n_axes):
    return pltpu.CompilerParams(dimension_semantics=("arbitrary",) * n_axes,
                                vmem_limit_bytes=VMEM_LIMIT)


def _layer_spec(l, *shape, single=False):
    mode = dict(pipeline_mode=pl.Buffered(1)) if single else {}
    return pl.BlockSpec((None,) + shape, lambda *_: (l,) + (0,) * len(shape), **mode)


def _rms(x, g):
    ms = jnp.mean(x * x, axis=-1, keepdims=True)
    return x * lax.rsqrt(ms + EPS) * g


def _dot(a, b):
    return jnp.dot(a, b, preferred_element_type=F32)


def _dot_nt(a, b):
    return lax.dot_general(a, b, (((1,), (1,)), ((), ())), preferred_element_type=F32)


def _half_mask(parity):
    lane = lax.broadcasted_iota(jnp.int32, (1, LANES), 1)
    return (lane // HALF) == parity


def _keep(mask, x):
    return jnp.where(mask, x, jnp.zeros_like(x))


def _with_ones(own, parity, v):
    lane = lax.broadcasted_iota(jnp.int32, (1, LANES), 1)
    ones = (lane == HALF * (1 - parity)).astype(v.dtype)
    return jnp.where(own, v, ones)


def _normalised(own, parity, parts):
    o = functools.reduce(jnp.add, parts)
    spare = HALF * (1 - parity)
    return jnp.where(own, o, 0.0) / o[:, spare:spare + 1]


def _mod_kernel(c_ref, w_ref, b_ref, o_ref):
    c = c_ref[...]
    s = c * jax.nn.sigmoid(c)
    o_ref[...] = _dot(s.astype(BF16), w_ref[...].astype(BF16)) + b_ref[...]


def _modulation(cond, w_mod, b_mod):
    n = 6
    wide = 3 * D_MODEL
    out = pl.pallas_call(
        _mod_kernel,
        grid=(DEPTH, n * D_MODEL // wide),
        in_specs=[
            pl.BlockSpec((8, D_MODEL), lambda l, j: (0, 0)),
            pl.BlockSpec((None, D_MODEL, wide), lambda l, j: (l, 0, j)),
            pl.BlockSpec((None, 1, wide), lambda l, j: (l, 0, j)),
        ],
        out_specs=pl.BlockSpec((None, 8, wide), lambda l, j: (l, 0, j)),
        out_shape=jax.ShapeDtypeStruct((DEPTH, 8, n * D_MODEL), F32),
        compiler_params=_cparams(2),
        name="modulation",
    )(cond, w_mod, b_mod.reshape(DEPTH, 1, n * D_MODEL))
    return out.reshape(DEPTH, 8, n, D_MODEL)


def _rope(x, cos, sin, lane_lo):
    up = pltpu.roll(x, LANES - 8, axis=1)
    dn = pltpu.roll(x, 8, axis=1)
    return x * cos + jnp.where(lane_lo, up, dn) * sin


def _proj_kernel(rope, emit_f32, x_ref, mod_ref, gmix_ref, win_ref, gcq_ref, wuq_ref, gckv_ref,
                 wukv_ref, gsgu_ref, wsgu_ref, bsg_ref, *refs):
    if rope:
        cos_ref, sin_ref = refs[:2]
        refs = refs[2:]
    if emit_f32:
        refs = refs[3:]
    qa_ref, ka_ref, va_ref, qm_ref, kp_ref, vp_ref, oc_ref = refs[:7]
    refs = refs[7:]
    if emit_f32:
        kaf_ref, vaf_ref, ckvf_ref, krf_ref = refs[:4]
        refs = refs[4:]
    hb_ref, z_ref = refs

    x = x_ref[...]
    h = _rms(x, gmix_ref[...]) * (1.0 + mod_ref[1:2, :]) + mod_ref[0:1, :]
    hb_ref[...] = h.astype(BF16)
    tm = x.shape[0]
    z_ref[:, OFF_UV:] = _dot_nt(hb_ref[...], win_ref[OFF_UV:, :])
    z_ref[:, OFF_CQ:OFF_UV] = _dot_nt(hb_ref[...], win_ref[OFF_CQ:OFF_UV, :])
    z_ref[:, :OFF_CQ] = _dot_nt(hb_ref[...], win_ref[:OFF_CQ, :])
    lane = lax.broadcasted_iota(jnp.int32, (1, LANES), 1)
    lane_lo = (lane % 16) < 8
    if rope:
        cos = cos_ref[...]
        sin = sin_ref[...]

    uv = jax.nn.gelu(z_ref[:, OFF_UV:OFF_UV + 2 * SGU_WIDTH])
    u = uv[:, :SGU_WIDTH]
    vn = _rms(uv[:, SGU_WIDTH:], gsgu_ref[...])
    even = _half_mask(0)
    for ch in range(tm // SGU_CHUNK):
        rows = slice(ch * SGU_CHUNK, (ch + 1) * SGU_CHUNK)
        for p in range(SGU_GROUPS // 2):
            sl = slice(p * LANES, (p + 1) * LANES)
            vc = vn[rows, sl]
            mixed = (_dot(wsgu_ref[2 * p], jnp.where(even, vc, 0.0).astype(BF16))
                     + _dot(wsgu_ref[2 * p + 1], jnp.where(even, 0.0, vc).astype(BF16))
                     + bsg_ref[:, sl])
            oc_ref[rows, sl] = (u[rows, sl] * mixed).astype(BF16)

    cq = z_ref[:, OFF_CQ:OFF_CQ + MLA_Q_RANK]
    cqn = _rms(cq, gcq_ref[...]).astype(BF16)
    qm = _dot(cqn, wuq_ref[...])
    for hd in range(MLA_HEADS):
        qh = qm[:, hd * LANES:(hd + 1) * LANES]
        if rope:
            qh = _rope(qh, cos, sin, lane_lo)
        qm_ref[hd] = (qh * MLA_QSCALE).astype(BF16)

    ckv = z_ref[:, OFF_CKV:OFF_CKV + MLA_KV_RANK]
    ckvn = _rms(ckv, gckv_ref[...])
    kr = z_ref[:, OFF_KR:OFF_KR + LANES]
    if emit_f32:
        for sq in range(tm // SEQ):
            ckvf_ref[sq] = ckvn[sq * SEQ:(sq + 1) * SEQ]
        krf_ref[...] = kr
    if rope:
        kr = _rope(kr, cos, sin, lane_lo)
    kv = _dot(ckvn.astype(BF16), wukv_ref[...])
    for hd in range(MLA_HEADS):
        kp_ref[hd] = (kv[:, hd * LANES:(hd + 1) * LANES] + kr).astype(BF16)
    voff = MLA_HEADS * LANES
    for p in range(MLA_HEADS // 2):
        vp_ref[p] = kv[:, voff + p * LANES:voff + (p + 1) * LANES].astype(BF16)

    qa = z_ref[:, OFF_QA:OFF_QA + NA_WIDTH] * NA_QSCALE
    ka = z_ref[:, OFF_KA:OFF_KA + NA_WIDTH]
    va = z_ref[:, OFF_VA:OFF_VA + NA_WIDTH]
    for p in range(NA_HEADS // 2):
        sl = slice(p * LANES, (p + 1) * LANES)
        qa_ref[p] = qa[:, sl].astype(BF16)
        ka_ref[p] = ka[:, sl].astype(BF16)
        va_ref[p] = va[:, sl].astype(BF16)
    if emit_f32:
        for src, dst in ((ka, kaf_ref), (va, vaf_ref)):
            src_t = src.T
            for sq in range(tm // SEQ):
                for hd in range(NA_HEADS):
                    dst[sq, hd] = src_t[hd * HEAD_DIM:(hd + 1) * HEAD_DIM, sq * SEQ:(sq + 1) * SEQ]


def _project(x, mods, mod_row, pw, l, rope_tabs, stacked, tiles_per_seq):
    emit_f32 = stacked is not None
    t = x.shape[0]
    tm = PROJ_TM
    nt = t // tm
    rope = rope_tabs is not None
    in_specs = [
        pl.BlockSpec((tm, D_MODEL), lambda i: (i, 0)),
        pl.BlockSpec((None, 6, D_MODEL), lambda i: (mod_row(i), 0, 0)),
        _layer_spec(l, 1, D_MODEL),
        _layer_spec(l, IN_COLS, D_MODEL),
        _layer_spec(l, 1, MLA_Q_RANK),
        _layer_spec(l, MLA_Q_RANK, MLA_HEADS * LANES),
        _layer_spec(l, 1, MLA_KV_RANK),
        _layer_spec(l, MLA_KV_RANK, KV_COLS),
        _layer_spec(l, 1, SGU_WIDTH),
        _layer_spec(l, SGU_GROUPS, SGU_CHUNK, SGU_CHUNK),
        _layer_spec(l, SGU_CHUNK, SGU_WIDTH),
    ]
    args = [x, mods, pw["g_mix"], pw["w_in"], pw["g_cq"], pw["w_uq"], pw["g_ckv"], pw["w_ukv"],
            pw["g_sgu"], pw["w_sgu"], pw["b_sgu"]]
    if rope:
        tab_spec = pl.BlockSpec((tm, LANES), lambda i: (i % tiles_per_seq, 0))
        in_specs += [tab_spec, tab_spec]
        args += list(rope_tabs)

    def heads(n):
        return (pl.BlockSpec((n, tm, LANES), lambda i: (0, i, 0)),
                jax.ShapeDtypeStruct((n, t, LANES), BF16))

    def flat(w, dt):
        return (pl.BlockSpec((tm, w), lambda i: (i, 0)), jax.ShapeDtypeStruct((t, w), dt))

    outs = [heads(2), heads(2), heads(2), heads(MLA_HEADS), heads(MLA_HEADS), heads(MLA_HEADS // 2),
            flat(SGU_WIDTH, BF16)]
    def layer_slab(tail):
        zeros = (0,) * len(tail)
        return pl.BlockSpec((tm // SEQ, None) + tail, lambda i: (i, l) + zeros)

    aliases = {}
    if emit_f32:
        for a in stacked:
            aliases[len(args)] = len(outs)
            in_specs.append(pl.BlockSpec(memory_space=pl.ANY))
            args.append(a)
            outs.append((layer_slab(a.shape[2:]), jax.ShapeDtypeStruct(a.shape, F32)))
        outs.append(flat(LANES, F32))
    return pl.pallas_call(
        functools.partial(_proj_kernel, rope, emit_f32),
        grid=(nt,),
        in_specs=in_specs,
        out_specs=[o[0] for o in outs],
        out_shape=[o[1] for o in outs],
        input_output_aliases=aliases,
        scratch_shapes=[pltpu.VMEM((tm, D_MODEL), BF16), pltpu.VMEM((tm, IN_COLS), F32)],
        compiler_params=_cparams(1),
        name="project_lat" if rope else "project_ctx",
    )(*args)


def _cache_kv_kernel(ckv_ref, kr_ref, wukv_ref, kp_ref, vp_ref):
    rows = kp_ref.shape[1]
    kv = _dot(ckv_ref[...].reshape(rows, MLA_KV_RANK).astype(BF16), wukv_ref[...])
    kr = kr_ref[...].reshape(rows, LANES)
    for hd in range(MLA_HEADS):
        kp_ref[hd] = (kv[:, hd * LANES:(hd + 1) * LANES] + kr).astype(BF16)
    voff = MLA_HEADS * LANES
    for p in range(MLA_HEADS // 2):
        vp_ref[p] = kv[:, voff + p * LANES:voff + (p + 1) * LANES].astype(BF16)


def _cache_kv(cache_ckv, cache_kr_pad, w_ukv):
    b = cache_ckv.shape[0]
    return pl.pallas_call(
        _cache_kv_kernel,
        grid=(DEPTH,),
        in_specs=[
            pl.BlockSpec((b, None, PAST_LEN, MLA_KV_RANK), lambda l: (0, l, 0, 0)),
            pl.BlockSpec((b, None, PAST_LEN, LANES), lambda l: (0, l, 0, 0)),
            pl.BlockSpec((None, MLA_KV_RANK, KV_COLS), lambda l: (l, 0, 0)),
        ],
        out_specs=[
            pl.BlockSpec((None, MLA_HEADS, b * PAST_LEN, LANES), lambda l: (l, 0, 0, 0)),
            pl.BlockSpec((None, MLA_HEADS // 2, b * PAST_LEN, LANES), lambda l: (l, 0, 0, 0)),
        ],
        out_shape=[
            jax.ShapeDtypeStruct((DEPTH, MLA_HEADS, b * PAST_LEN, LANES), BF16),
            jax.ShapeDtypeStruct((DEPTH, MLA_HEADS // 2, b * PAST_LEN, LANES), BF16),
        ],
        compiler_params=_cparams(1),
        name="cache_kv",
    )(cache_ckv, cache_kr_pad, w_ukv)


def _ctx_attn_kernel(qa_ref, ka_ref, va_ref, qm_ref, kp_ref, vp_ref, o_ref):
    for sq in range(CTX_SEQS):
        rows = slice(sq * SEQ, (sq + 1) * SEQ)
        for p in range(NA_HEADS // 2 + MLA_HEADS // 2):
            out = None
            for half in range(2):
                own = _half_mask(half)
                if p < NA_HEADS // 2:
                    q = _keep(own, qa_ref[p, rows, :])
                    k = ka_ref[p, rows, :]
                    v = va_ref[p, rows, :]
                else:
                    hd = 2 * (p - NA_HEADS // 2) + half
                    q = qm_ref[hd, rows, :]
                    k = kp_ref[hd, rows, :]
                    v = vp_ref[hd // 2, rows, :]
                s = _dot_nt(q, k)
                p_ = jnp.exp2(s - jnp.max(s, axis=-1, keepdims=True))
                o = _dot(p_.astype(BF16), _keep(own, v)) / jnp.sum(p_, axis=-1, keepdims=True)
                out = o if out is None else out + o
            o_ref[rows, p * LANES:(p + 1) * LANES] = out.astype(BF16)


def _ctx_attention(qa, ka, va, qm, kp, vp, n_seq):
    rows = CTX_SEQS * SEQ
    heads = lambda n: pl.BlockSpec((n, rows, LANES), lambda b: (0, b, 0))
    width = NA_WIDTH + MLA_WIDTH
    return pl.pallas_call(
        _ctx_attn_kernel,
        grid=(n_seq // CTX_SEQS,),
        in_specs=[heads(2), heads(2), heads(2), heads(MLA_HEADS), heads(MLA_HEADS), heads(MLA_HEADS // 2)],
        out_specs=pl.BlockSpec((rows, width), lambda b: (b, 0)),
        out_shape=jax.ShapeDtypeStruct((n_seq * SEQ, width), BF16),
        compiler_params=_cparams(1),
        name="ctx_attention",
    )(qa, ka, va, qm, kp, vp)


def _lat_mla_kernel(q_ref, kc_ref, kl_ref, vc_ref, vl_ref, o_ref, s_ref, m_ref, p_ref):
    nc = kc_ref.shape[1]
    outs = {}

    def scores(hd):
        q = q_ref[hd]
        sc = _dot_nt(q, kc_ref[hd])
        sl = _dot_nt(q, kl_ref[hd])
        s_ref[hd % 2, :, 0:nc] = sc
        s_ref[hd % 2, :, nc:] = sl
        m_ref[hd % 2] = jnp.maximum(_lane_block_max(sc), _lane_block_max(sl))

    def softmax(hd):
        m = jnp.max(m_ref[hd % 2], axis=-1, keepdims=True)
        p_ref[hd % 2] = jnp.exp2(s_ref[hd % 2] - m).astype(BF16)

    def values(hd):
        own = _half_mask(hd % 2)
        o = _normalised(own, hd % 2, [_dot(p_ref[hd % 2, :, 0:nc], _with_ones(own, hd % 2, vc_ref[hd // 2])),
                                      _dot(p_ref[hd % 2, :, nc:], _with_ones(own, hd % 2, vl_ref[hd // 2]))])
        if hd % 2 == 0:
            outs[hd // 2] = o
        else:
            o_ref[:, (hd // 2) * LANES:(hd // 2 + 1) * LANES] = (outs.pop(hd // 2) + o).astype(BF16)

    for t in range(MLA_HEADS + 2):
        if t < MLA_HEADS:
            scores(t)
        if 0 <= t - 1 < MLA_HEADS:
            softmax(t - 1)
        if 0 <= t - 2 < MLA_HEADS:
            values(t - 2)


def _lat_mla(q, kc, vc, kl, vl, l, n_batch):
    nq = DEC_SEQ // TM
    n_keys = PAST_LEN + DEC_SEQ
    return pl.pallas_call(
        _lat_mla_kernel,
        grid=(n_batch, nq),
        in_specs=[pl.BlockSpec((MLA_HEADS, TM, LANES), lambda b, i: (0, b * nq + i, 0)),
                  pl.BlockSpec((None, MLA_HEADS, PAST_LEN, LANES), lambda b, i: (l, 0, b, 0)),
                  pl.BlockSpec((MLA_HEADS, DEC_SEQ, LANES), lambda b, i: (0, b, 0)),
                  pl.BlockSpec((None, MLA_HEADS // 2, PAST_LEN, LANES), lambda b, i: (l, 0, b, 0)),
                  pl.BlockSpec((MLA_HEADS // 2, DEC_SEQ, LANES), lambda b, i: (0, b, 0))],
        out_specs=pl.BlockSpec((TM, MLA_WIDTH), lambda b, i: (b * nq + i, 0)),
        out_shape=jax.ShapeDtypeStruct((n_batch * DEC_SEQ, MLA_WIDTH), BF16),
        scratch_shapes=[pltpu.VMEM((2, TM, n_keys), F32), pltpu.VMEM((2, TM, LANES), F32),
                        pltpu.VMEM((2, TM, n_keys), BF16)],
        compiler_params=_cparams(2),
        name="lat_mla",
    )(q, kc, kl, vc, vl)


def _na_bias_kernel(rpb_ref, o_ref, tab_ref):
    l = pl.program_id(0)
    hd = pl.program_id(1)
    n_dr = 2 * NA_WIN_R - 1
    n_dc = 2 * NA_WIN_C - 1
    base = (l * NA_HEADS + hd) * n_dr * n_dc
    qc = lax.broadcasted_iota(jnp.int32, (GRID_W, LANES), 0)
    kc = lax.broadcasted_iota(jnp.int32, (GRID_W, LANES), 1) % GRID_W
    cs = jnp.clip(qc - NA_WIN_C // 2, 0, GRID_W - NA_WIN_C)
    in_win = (kc >= cs) & (kc < cs + NA_WIN_C)
    neg = jnp.full((GRID_W, LANES), NEG_INF * LOG2E, F32)
    lane = lax.broadcasted_iota(jnp.int32, (GRID_W, LANES), 1)
    for dr in range(n_dr):
        vec = neg
        for dc in range(n_dc):
            vec = jnp.where(lane == dc, rpb_ref[base + dr * n_dc + dc] * LOG2E, vec)
        rot = pltpu.roll(vec, LANES - (NA_WIN_C - 1), axis=1, stride=1, stride_axis=0)
        both = jnp.where(lane < GRID_W, rot, pltpu.roll(rot, GRID_W, axis=1))
        tab_ref[dr] = jnp.where(in_win, both, neg)
    left = lax.broadcasted_iota(jnp.int32, (GRID_W, LANES), 1) < GRID_W
    n_krows = NA_KBLKS * NA_QBLK // GRID_W
    cases = ((0, lambda qr: 0), (-NA_QROWS, lambda qr: qr), (-2 * NA_QROWS, lambda qr: NA_QROWS))
    for c, (shift, first) in enumerate(cases):
        for qr in range(NA_QROWS):
            for kp in range(n_krows // 2):
                tiles = []
                for kr in (2 * kp, 2 * kp + 1):
                    ok = first(qr) <= kr < first(qr) + NA_WIN_R
                    tiles.append(tab_ref[kr - qr + shift + NA_WIN_R - 1] if ok else neg)
                o_ref[c, qr * GRID_W:(qr + 1) * GRID_W, kp * LANES:(kp + 1) * LANES] = (
                    jnp.where(left, tiles[0], tiles[1]))


def _na_bias(na_rpb):
    n_k = NA_KBLKS * NA_QBLK
    return pl.pallas_call(
        _na_bias_kernel,
        grid=(DEPTH, NA_HEADS),
        in_specs=[pl.BlockSpec(memory_space=pltpu.SMEM)],
        out_specs=pl.BlockSpec((None, 3, None, NA_QBLK, n_k), lambda l, h: (l, 0, h, 0, 0)),
        out_shape=jax.ShapeDtypeStruct((DEPTH, 3, NA_HEADS, NA_QBLK, n_k), F32),
        scratch_shapes=[pltpu.VMEM((2 * NA_WIN_R - 1, GRID_W, LANES), F32)],
        compiler_params=_cparams(2),
        name="na_bias",
    )(na_rpb.reshape(-1))


def _lane_block_max(s):
    return functools.reduce(jnp.maximum, [s[:, j * LANES:(j + 1) * LANES] for j in range(s.shape[1] // LANES)])


def _na_kernel(n_sub, q_ref, *refs):
    per = 2 * NA_KBLKS + 1
    subs = [refs[r * per:(r + 1) * per] for r in range(n_sub)]
    kc_ref, vc_ref, o_ref, s_ref, m_ref, p_ref = refs[n_sub * per:]
    items = [(r, hd) for r in range(n_sub) for hd in range(NA_HEADS)]
    outs = {}
    n_loc = NA_KBLKS * NA_QBLK

    def scores(t):
        r, hd = items[t]
        k_refs, bias_ref = subs[r][:NA_KBLKS], subs[r][-1]
        p = hd // 2
        q = _keep(_half_mask(hd % 2), q_ref[p, r * NA_QBLK:(r + 1) * NA_QBLK, :])
        m = None
        for i in range(NA_KBLKS):
            s = _dot_nt(q, k_refs[i][p]) + bias_ref[hd, :, i * NA_QBLK:(i + 1) * NA_QBLK]
            s_ref[t % 2, :, i * NA_QBLK:(i + 1) * NA_QBLK] = s
            m = _lane_block_max(s) if m is None else jnp.maximum(m, _lane_block_max(s))
        s = _dot(q, kc_ref[p * LANES:(p + 1) * LANES, :].astype(BF16))
        s_ref[t % 2, :, n_loc:] = s
        m_ref[t % 2] = jnp.maximum(m, _lane_block_max(s))

    def softmax(t):
        m = jnp.max(m_ref[t % 2], axis=-1, keepdims=True)
        p_ref[t % 2] = jnp.exp2(s_ref[t % 2] - m).astype(BF16)

    def values(t):
        r, hd = items[t]
        v_refs = subs[r][NA_KBLKS:2 * NA_KBLKS]
        p = hd // 2
        own = _half_mask(hd % 2)
        row = lax.broadcasted_iota(jnp.int32, (LANES, 1), 0)
        vc = vc_ref[p * LANES:(p + 1) * LANES, :].astype(BF16)
        vc = jnp.where((row // HALF) == hd % 2, vc, (row == HALF * (1 - hd % 2)).astype(BF16))
        parts = [_dot_nt(p_ref[t % 2, :, n_loc:], vc)]
        for i in range(NA_KBLKS):
            parts.append(_dot(p_ref[t % 2, :, i * NA_QBLK:(i + 1) * NA_QBLK],
                              _with_ones(own, hd % 2, v_refs[i][p])))
        o = _normalised(own, hd % 2, parts)
        if hd % 2 == 0:
            outs[(r, p)] = o
        else:
            o_ref[r * NA_QBLK:(r + 1) * NA_QBLK, p * LANES:(p + 1) * LANES] = (outs.pop((r, p)) + o).astype(BF16)

    for t in range(len(items) + 2):
        if t < len(items):
            scores(t)
        if 0 <= t - 1 < len(items):
            softmax(t - 1)
        if 0 <= t - 2 < len(items):
            values(t - 2)


def _na_attention(qa, ka, va, cache_k, cache_v, bias, l, n_batch):
    nblk = DEC_SEQ // NA_QBLK
    max_start = nblk - NA_KBLKS
    n_sub = NA_SUBS
    nstep = nblk // n_sub
    blk = lambda jj, r: jj * n_sub + r

    def kspec(r, i):
        return pl.BlockSpec((NA_HEADS // 2, NA_QBLK, LANES),
                            lambda jj, b: (0, b * nblk + jnp.clip(blk(jj, r) - 1, 0, max_start) + i, 0))

    case = lambda j: jnp.where(j == 0, 0, jnp.where(j == nblk - 1, 2, 1))
    cache_spec = pl.BlockSpec((None, None, NA_WIDTH, PAST_LEN), lambda jj, b: (b, l, 0, 0))
    in_specs = [pl.BlockSpec((NA_HEADS // 2, n_sub * NA_QBLK, LANES), lambda jj, b: (0, b * nstep + jj, 0))]
    args = [qa]
    for r in range(n_sub):
        in_specs += [kspec(r, i) for i in range(NA_KBLKS)] + [kspec(r, i) for i in range(NA_KBLKS)]
        in_specs.append(pl.BlockSpec((None, None, NA_HEADS, NA_QBLK, NA_KBLKS * NA_QBLK),
                                     lambda jj, b, r=r: (l, case(blk(jj, r)), 0, 0, 0)))
        args += [ka] * NA_KBLKS + [va] * NA_KBLKS + [bias]
    in_specs += [cache_spec, cache_spec]
    args += [cache_k, cache_v]
    return pl.pallas_call(
        functools.partial(_na_kernel, n_sub),
        grid=(nstep, n_batch),
        in_specs=in_specs,
        out_specs=pl.BlockSpec((n_sub * NA_QBLK, NA_WIDTH), lambda jj, b: (b * nstep + jj, 0)),
        out_shape=jax.ShapeDtypeStruct((n_batch * DEC_SEQ, NA_WIDTH), BF16),
        scratch_shapes=[pltpu.VMEM((2, NA_QBLK, (NA_KBLKS + 1) * NA_QBLK), F32),
                        pltpu.VMEM((2, NA_QBLK, LANES), F32),
                        pltpu.VMEM((2, NA_QBLK, (NA_KBLKS + 1) * NA_QBLK), BF16)],
        compiler_params=_cparams(2),
        name="na_attention",
    )(*args)


def _ffn_kernel(seq_len, final, widths, *refs):
    halo = seq_len > TM
    n_in = 3 if halo else 1
    x_refs = refs[:n_in]
    refs = refs[n_in:]
    part_refs = [refs[i * n_in:(i + 1) * n_in] for i in range(len(widths))]
    refs = refs[n_in * len(widths):]
    mod_ref, gffn_ref, wo_ref, win_ref, cw_ref, cb_ref, wout_ref = refs[:7]
    refs = refs[7:]
    if final:
        gfin_ref = refs[0]
        refs = refs[1:]
    o_ref, mix_ref, hext_ref, a_ref, act_ref = refs
    tm = o_ref.shape[0]
    pad = BF16_ROWS
    if halo:
        tiles_per_seq = seq_len // tm
        pos = pl.program_id(0) % tiles_per_seq
        pieces = [(0, 0, slice(0, pad), pos != 0), (pad, 1, slice(0, tm), None),
                  (pad + tm, 2, slice(0, pad), pos != tiles_per_seq - 1)]
        frames = []
        starts = [pad]
        seg_len = tm
    else:
        nseg = tm // seq_len
        starts = [pad + s * (seq_len + pad) for s in range(nseg)]
        seg_len = seq_len
        pieces = [(starts[s], 0, slice(s * seq_len, (s + 1) * seq_len), None) for s in range(nseg)]
        frames = [s * (seq_len + pad) for s in range(nseg + 1)]
    for f in frames:
        mix_ref[f:f + pad] = jnp.zeros((pad, D_MODEL), BF16)
        hext_ref[f:f + pad] = jnp.zeros((pad, D_MODEL), BF16)
    off = 0
    for n, prefs in zip(widths, part_refs):
        for row, src, rows, _ in pieces:
            mix_ref[row:row + rows.stop - rows.start, off:off + n] = prefs[src][rows, :]
        off += n
    r = _dot(mix_ref[...], wo_ref[...])
    g1 = mod_ref[2:3, :]
    scale = gffn_ref[...] * (1.0 + mod_ref[4:5, :])
    shift = mod_ref[3:4, :]
    for row, src, rows, keep in pieces:
        n_rows = rows.stop - rows.start
        x1 = x_refs[src][rows, :] + g1 * r[row:row + n_rows]
        if src == n_in // 2:
            o_ref[rows, :] = x1
        h2 = (x1 * lax.rsqrt(jnp.mean(x1 * x1, axis=-1, keepdims=True) + EPS) * scale + shift).astype(BF16)
        if keep is not None:
            h2 = jnp.where(keep, h2, jnp.zeros_like(h2))
        hext_ref[row:row + n_rows] = h2

    def col(c, lb):
        half = FF_LB // 2
        return (lb // half) * D_FF + c * FF_CHUNK + (lb % half) * LANES

    def up(c):
        hext = hext_ref[...]
        for part in range(2):
            a = _dot(hext, win_ref[:, part * D_FF + c * FF_CHUNK:part * D_FF + (c + 1) * FF_CHUNK])
            for j in range(FF_LB // 2):
                a_ref[c % 2, part * (FF_LB // 2) + j] = a[:, j * LANES:(j + 1) * LANES]

    def conv(c, lb, st):
        cs = slice(col(c, lb), col(c, lb) + LANES)
        return (a_ref[c % 2, lb, pl.ds(st - 1, seg_len, stride=1), :] * cw_ref[0:1, cs]
                + a_ref[c % 2, lb, st:st + seg_len, :] * cw_ref[1:2, cs]
                + a_ref[c % 2, lb, pl.ds(st + 1, seg_len, stride=1), :] * cw_ref[2:3, cs] + cb_ref[:, cs])

    up(0)
    for c in range(N_FF_CHUNKS):
        if c + 1 < N_FF_CHUNKS:
            up(c + 1)
        for s, st in enumerate(starts):
            for lb in range(FF_LB // 2):
                gate = conv(c, lb, st)
                val = conv(c, lb + FF_LB // 2, st)
                act_ref[s * seg_len:(s + 1) * seg_len, c * FF_CHUNK + lb * LANES:c * FF_CHUNK + (lb + 1) * LANES] = (
                    gate * jax.nn.sigmoid(gate) * val).astype(BF16)
    y = o_ref[...] + mod_ref[5:6, :] * _dot(act_ref[...], wout_ref[...])
    if final:
        y = _rms(y, gfin_ref[...])
    o_ref[...] = y


def _ffn(x, parts, mods, mod_row, pw, l, seq_len, g_final, name):
    t = x.shape[0]
    nt = t // TM
    hb = TM // BF16_ROWS
    n_hblk = t // BF16_ROWS
    final = g_final is not None
    halo = seq_len > TM
    ext = TM + 2 * BF16_ROWS if halo else TM + (TM // seq_len + 1) * BF16_ROWS
    widths = tuple(p.shape[1] for p in parts)

    def tiled(a):
        w = a.shape[1]
        main = pl.BlockSpec((TM, w), lambda i: (i, 0))
        if not halo:
            return [main], [a]
        prev = pl.BlockSpec((BF16_ROWS, w), lambda i: (jnp.maximum(i * hb - 1, 0), 0))
        nxt = pl.BlockSpec((BF16_ROWS, w), lambda i: (jnp.minimum((i + 1) * hb, n_hblk - 1), 0))
        return [prev, main, nxt], [a, a, a]

    in_specs, args = [], []
    for a in (x,) + tuple(parts):
        sp, ar = tiled(a)
        in_specs += sp
        args += ar
    in_specs += [pl.BlockSpec((None, 6, D_MODEL), lambda i: (mod_row(i), 0, 0)),
                 _layer_spec(l, 1, D_MODEL),
                 _layer_spec(l, D_MODEL, D_MODEL, single=True),
                 _layer_spec(l, D_MODEL, 2 * D_FF, single=True),
                 _layer_spec(l, 3, 2 * D_FF, single=True),
                 _layer_spec(l, 1, 2 * D_FF, single=True),
                 _layer_spec(l, D_FF, D_MODEL, single=True)]
    args += [mods, pw["g_ffn"], pw["w_out"], pw["w_ffn_in"], pw["conv_w"], pw["conv_b"], pw["w_ffn_out"]]
    if final:
        in_specs.append(pl.BlockSpec((1, D_MODEL), lambda i: (0, 0)))
        args.append(g_final)
    return pl.pallas_call(
        functools.partial(_ffn_kernel, seq_len, final, widths),
        grid=(nt,),
        in_specs=in_specs,
        out_specs=pl.BlockSpec((TM, D_MODEL), lambda i: (i, 0)),
        out_shape=jax.ShapeDtypeStruct((t, D_MODEL), F32),
        scratch_shapes=[pltpu.VMEM((ext, D_MODEL), BF16), pltpu.VMEM((ext, D_MODEL), BF16),
                        pltpu.VMEM((2, FF_LB, ext, LANES), F32), pltpu.VMEM((TM, D_FF), BF16)],
        compiler_params=_cparams(1),
        name=name,
    )(*args)


def _pack_mla_kernel(wuq_ref, wukv_ref, ouq_ref, oukv_ref):
    dq = MLA_NOPE + MLA_ROPE
    zq = jnp.zeros((MLA_Q_RANK, LANES - dq), F32)
    zk = jnp.zeros((MLA_KV_RANK, LANES - MLA_NOPE), F32)
    dkv = MLA_NOPE + MLA_V
    for hd in range(MLA_HEADS):
        ouq_ref[:, hd * LANES:(hd + 1) * LANES] = jnp.concatenate(
            [wuq_ref[:, hd * dq:(hd + 1) * dq], zq], axis=-1).astype(BF16)
        oukv_ref[:, hd * LANES:(hd + 1) * LANES] = jnp.concatenate(
            [wukv_ref[:, hd * dkv:hd * dkv + MLA_NOPE], zk], axis=-1).astype(BF16)
    voff = MLA_HEADS * LANES
    for p in range(MLA_HEADS // 2):
        oukv_ref[:, voff + p * LANES:voff + (p + 1) * LANES] = jnp.concatenate(
            [wukv_ref[:, (2 * p + j) * dkv + MLA_NOPE:(2 * p + j + 1) * dkv] for j in range(2)], axis=-1).astype(BF16)


def _pack_weights(w_in, w_uq, w_ukv, w_sgu, b_sgu, w_out, w_ffn_in, ffn_conv_w, ffn_conv_b, w_ffn_out,
                  g_mix, g_cq, g_ckv, g_sgu, g_ffn):
    nl = w_in.shape[0]
    wt = jnp.swapaxes(w_in, 1, 2).astype(BF16)
    zr = lambda n: jnp.zeros((nl, n, D_MODEL), BF16)
    w_in_p = jnp.concatenate([wt[:, :OFF_KR], zr(HALF), wt[:, OFF_KR:OFF_KR + MLA_ROPE],
                              zr(LANES - HALF - MLA_ROPE), wt[:, OFF_KR + MLA_ROPE:]], axis=1)
    w_uq_p, w_ukv_p = pl.pallas_call(
        _pack_mla_kernel,
        grid=(nl,),
        in_specs=[pl.BlockSpec((None,) + w_uq.shape[1:], lambda l: (l, 0, 0)),
                  pl.BlockSpec((None,) + w_ukv.shape[1:], lambda l: (l, 0, 0))],
        out_specs=[pl.BlockSpec((None, MLA_Q_RANK, MLA_HEADS * LANES), lambda l: (l, 0, 0)),
                   pl.BlockSpec((None, MLA_KV_RANK, KV_COLS), lambda l: (l, 0, 0))],
        out_shape=[jax.ShapeDtypeStruct((nl, MLA_Q_RANK, MLA_HEADS * LANES), BF16),
                   jax.ShapeDtypeStruct((nl, MLA_KV_RANK, KV_COLS), BF16)],
        compiler_params=_cparams(1),
        name="pack_w_mla",
    )(w_uq, w_ukv)
    b_sgu_p = jnp.repeat(jnp.swapaxes(b_sgu, 1, 2), SGU_WIDTH // SGU_GROUPS, axis=-1)
    return dict(
        w_in=w_in_p, w_uq=w_uq_p, w_ukv=w_ukv_p, w_sgu=w_sgu.astype(BF16), b_sgu=b_sgu_p,
        w_out=w_out.astype(BF16), w_ffn_in=w_ffn_in.astype(BF16), conv_w=ffn_conv_w,
        conv_b=ffn_conv_b[:, None, :], w_ffn_out=w_ffn_out.astype(BF16),
        g_mix=g_mix[:, None, :], g_cq=g_cq[:, None, :], g_ckv=g_ckv[:, None, :], g_sgu=g_sgu[:, None, :],
        g_ffn=g_ffn[:, None, :])


def _rope_tables(n_tokens):
    t = jnp.arange(n_tokens)[:, None]
    n_freq = MLA_ROPE // 4
    inv_freq = ROPE_THETA ** (-jnp.arange(n_freq, dtype=F32) / n_freq)
    j = jnp.arange(LANES)[None, :] - HALF
    rotary = (j >= 0) & (j < MLA_ROPE)
    pos = jnp.where(j < MLA_ROPE // 2, t // GRID_W, t % GRID_W).astype(F32)
    ang = pos * inv_freq[j % n_freq]
    sign = jnp.where(j % (2 * n_freq) < n_freq, -1.0, 1.0)
    cos = jnp.where(rotary, jnp.cos(ang), 1.0)
    sin = jnp.where(rotary, sign * jnp.sin(ang), 0.0)
    return cos, sin


def kernel(x_prompt, x_sample, cache_na_k, cache_na_v, cache_mla_ckv, cache_mla_krope, c, c_ctx, w_mod, b_mod,
           g_mix, w_in, na_rpb, g_cq, w_uq, g_ckv, w_ukv, g_sgu, w_sgu, b_sgu, w_out, g_ffn, w_ffn_in,
           ffn_conv_w, ffn_conv_b, w_ffn_out, g_final):
    n_ctx, n_lat = x_prompt.shape[0], x_sample.shape[0]
    t_ctx, t_lat = n_ctx * SEQ, n_lat * DEC_SEQ
    pw = _pack_weights(w_in, w_uq, w_ukv, w_sgu, b_sgu, w_out, w_ffn_in, ffn_conv_w, ffn_conv_b,
                       w_ffn_out, g_mix, g_cq, g_ckv, g_sgu, g_ffn)
    g_fin = g_final[None, :]

    cond = jnp.concatenate([c_ctx[None, :], c, jnp.zeros((8 - 1 - n_lat, D_MODEL), F32)], axis=0)
    mods = _modulation(cond, w_mod, b_mod)
    rope_tabs = _rope_tables(DEC_SEQ)
    na_bias = _na_bias(na_rpb)
    kr_pad = jnp.pad(cache_mla_krope, ((0, 0), (0, 0), (0, 0), (HALF, LANES - HALF - MLA_ROPE)))
    cache_kp, cache_vp = _cache_kv(cache_mla_ckv, kr_pad, pw["w_ukv"])
    channel_major = lambda a: jnp.transpose(a, (0, 1, 3, 4, 2)).reshape(n_lat, DEPTH, NA_WIDTH, PAST_LEN)
    cache_k = channel_major(cache_na_k)
    cache_v = channel_major(cache_na_v)

    lat_tiles = DEC_SEQ // TM
    proj_tiles = DEC_SEQ // PROJ_TM
    ctx_row = lambda i: 0
    lat_row = lambda i: 1 + i // lat_tiles
    lat_proj_row = lambda i: 1 + i // proj_tiles

    xp = x_prompt.reshape(t_ctx, D_MODEL)
    xs = x_sample.reshape(t_lat, D_MODEL)
    per_head = (n_ctx, DEPTH, NA_HEADS, HEAD_DIM, SEQ)
    ctx_out = (jnp.zeros(per_head, F32), jnp.zeros(per_head, F32), jnp.zeros((n_ctx, DEPTH, SEQ, MLA_KV_RANK), F32))
    new_kr = []
    for l in range(DEPTH):
        last = l == DEPTH - 1
        m = mods[l]
        qa, ka, va, qm, kp, vp, oc, *ctx_out, kr_f = _project(xp, m, ctx_row, pw, l, None, ctx_out, 1)
        new_kr.append(kr_f[:, HALF:HALF + MLA_ROPE])
        o_ab = _ctx_attention(qa, ka, va, qm, kp, vp, n_ctx)
        xp = _ffn(xp, (o_ab, oc), m, ctx_row, pw, l, SEQ, g_fin if last else None, "ctx_ffn")

        qa, ka, va, qm, kp, vp, oc = _project(xs, m, lat_proj_row, pw, l, rope_tabs, None, proj_tiles)
        o_a = _na_attention(qa, ka, va, cache_k, cache_v, na_bias, l, n_lat)
        o_b = _lat_mla(qm, cache_kp, cache_vp, kp, vp, l, n_lat)
        xs = _ffn(xs, (o_a, o_b, oc), m, lat_row, pw, l, DEC_SEQ, g_fin if last else None, "lat_ffn")

    token_major = lambda a: jnp.transpose(a, (0, 1, 4, 2, 3))
    new_k, new_v, new_ckv = ctx_out
    return (xp.reshape(n_ctx, SEQ, D_MODEL), xs.reshape(n_lat, DEC_SEQ, D_MODEL),
            token_major(new_k), token_major(new_v), new_ckv,
            jnp.stack([a.reshape(n_ctx, SEQ, MLA_ROPE) for a in new_kr], axis=1))
```

```python
import functools
import math

import jax
import jax.numpy as jnp
from jax import lax
from jax.experimental import pallas as pl
from jax.experimental.pallas import tpu as pltpu

F32 = jnp.float32
BF16 = jnp.bfloat16

D_MODEL = 1024
DEPTH = 4
SEQ = 256
DEC_SEQ = 2048
PAST_LEN = 256
GRID_W = 64
HEAD_DIM = 64
NA_WIDTH = 256
NA_HEADS = 4
NA_WIN_R = 8
NA_WIN_C = 16
MLA_HEADS = 8
MLA_NOPE = 64
MLA_ROPE = 32
MLA_V = 64
MLA_WIDTH = MLA_HEADS * MLA_V
MLA_Q_RANK = 384
MLA_KV_RANK = 256
SGU_WIDTH = 256
SGU_GROUPS = 4
SGU_CHUNK = 128
D_FF = 2816
ROPE_THETA = 10000.0
EPS = 1e-6
NEG_INF = -1e30
LOG2E = math.log2(math.e)
NA_QSCALE = HEAD_DIM ** -0.5 * LOG2E
MLA_QSCALE = (MLA_NOPE + MLA_ROPE) ** -0.5 * LOG2E

LANES = 128
BF16_ROWS = 16
HALF = LANES // 2

OFF_QA, OFF_KA, OFF_VA = 0, 256, 512
OFF_CQ = 768
OFF_CKV = OFF_CQ + MLA_Q_RANK
OFF_KR = OFF_CKV + MLA_KV_RANK
OFF_UV = OFF_KR + LANES
IN_COLS = OFF_UV + 2 * SGU_WIDTH
KV_COLS = MLA_HEADS * LANES + MLA_HEADS * MLA_V

FF_CHUNK = 256
N_FF_CHUNKS = D_FF // FF_CHUNK
FF_LB = 2 * FF_CHUNK // LANES

TM = 512
PROJ_TM = 1024
CTX_SEQS = 4
NA_SUBS = 4
NA_QROWS = 4
NA_QBLK = NA_QROWS * GRID_W
NA_KBLKS = 3
VMEM_LIMIT = 56 * 1024 * 1024


def _cparams(n_axes):
    return pltpu.CompilerParams(dimension_semantics=("arbitrary",) * n_axes,
                                vmem_limit_bytes=VMEM_LIMIT)


def _layer_spec(l, *shape, single=False):
    mode = dict(pipeline_mode=pl.Buffered(1)) if single else {}
    return pl.BlockSpec((None,) + shape, lambda *_: (l,) + (0,) * len(shape), **mode)


def _rms(x, g):
    ms = jnp.mean(x * x, axis=-1, keepdims=True)
    return x * lax.rsqrt(ms + EPS) * g


def _dot(a, b):
    return jnp.dot(a, b, preferred_element_type=F32)


def _dot_nt(a, b):
    return lax.dot_general(a, b, (((1,), (1,)), ((), ())), preferred_element_type=F32)


def _half_mask(parity):
    lane = lax.broadcasted_iota(jnp.int32, (1, LANES), 1)
    return (lane // HALF) == parity


def _keep(mask, x):
    return jnp.where(mask, x, jnp.zeros_like(x))


def _with_ones(own, parity, v):
    lane = lax.broadcasted_iota(jnp.int32, (1, LANES), 1)
    ones = (lane == HALF * (1 - parity)).astype(v.dtype)
    return jnp.where(own, v, ones)


def _normalised(own, parity, parts):
    o = functools.reduce(jnp.add, parts)
    spare = HALF * (1 - parity)
    return jnp.where(own, o, 0.0) / o[:, spare:spare + 1]


def _mod_kernel(c_ref, w_ref, b_ref, o_ref):
    c = c_ref[...]
    s = c * jax.nn.sigmoid(c)
    o_ref[...] = _dot(s.astype(BF16), w_ref[...].astype(BF16)) + b_ref[...]


def _modulation(cond, w_mod, b_mod):
    n = 6
    wide = 2 * D_MODEL
    out = pl.pallas_call(
        _mod_kernel,
        grid=(DEPTH, n * D_MODEL // wide),
        in_specs=[
            pl.BlockSpec((8, D_MODEL), lambda l, j: (0, 0)),
            pl.BlockSpec((None, D_MODEL, wide), lambda l, j: (l, 0, j)),
            pl.BlockSpec((None, 1, wide), lambda l, j: (l, 0, j)),
        ],
        out_specs=pl.BlockSpec((None, 8, wide), lambda l, j: (l, 0, j)),
        out_shape=jax.ShapeDtypeStruct((DEPTH, 8, n * D_MODEL), F32),
        compiler_params=_cparams(2),
        name="modulation",
    )(cond, w_mod, b_mod.reshape(DEPTH, 1, n * D_MODEL))
    return out.reshape(DEPTH, 8, n, D_MODEL)


def _rope(x, cos, sin, lane_lo):
    up = pltpu.roll(x, LANES - 8, axis=1)
    dn = pltpu.roll(x, 8, axis=1)
    return x * cos + jnp.where(lane_lo, up, dn) * sin


def _proj_kernel(rope, emit_f32, x_ref, mod_ref, gmix_ref, win_ref, gcq_ref, wuq_ref, gckv_ref,
                 wukv_ref, gsgu_ref, wsgu_ref, bsg_ref, *refs):
    if rope:
        cos_ref, sin_ref = refs[:2]
        refs = refs[2:]
    if emit_f32:
        refs = refs[3:]
    if emit_f32:
        attn_ref, oc_ref, kaf_ref, vaf_ref, ckvf_ref, krf_ref = refs[:6]
        hb_ref, z_ref, qa_ref, ka_ref, va_ref, qm_ref, kp_ref, vp_ref = refs[6:]
    else:
        qa_ref, ka_ref, va_ref, qm_ref, kp_ref, vp_ref, oc_ref = refs[:7]
        hb_ref, z_ref = refs[7:]

    x = x_ref[...]
    h = _rms(x, gmix_ref[...]) * (1.0 + mod_ref[1:2, :]) + mod_ref[0:1, :]
    hb_ref[...] = h.astype(BF16)
    tm = x.shape[0]
    z_ref[:, OFF_UV:] = _dot_nt(hb_ref[...], win_ref[OFF_UV:, :])
    z_ref[:, OFF_CQ:OFF_UV] = _dot_nt(hb_ref[...], win_ref[OFF_CQ:OFF_UV, :])
    z_ref[:, :OFF_CQ] = _dot_nt(hb_ref[...], win_ref[:OFF_CQ, :])
    lane = lax.broadcasted_iota(jnp.int32, (1, LANES), 1)
    lane_lo = (lane % 16) < 8
    if rope:
        cos = cos_ref[...]
        sin = sin_ref[...]

    uv = jax.nn.gelu(z_ref[:, OFF_UV:OFF_UV + 2 * SGU_WIDTH])
    u = uv[:, :SGU_WIDTH]
    vn = _rms(uv[:, SGU_WIDTH:], gsgu_ref[...])
    even = _half_mask(0)
    for ch in range(tm // SGU_CHUNK):
        rows = slice(ch * SGU_CHUNK, (ch + 1) * SGU_CHUNK)
        for p in range(SGU_GROUPS // 2):
            sl = slice(p * LANES, (p + 1) * LANES)
            vc = vn[rows, sl]
            mixed = (_dot(wsgu_ref[2 * p], jnp.where(even, vc, 0.0).astype(BF16))
                     + _dot(wsgu_ref[2 * p + 1], jnp.where(even, 0.0, vc).astype(BF16))
                     + bsg_ref[:, sl])
            oc_ref[rows, sl] = (u[rows, sl] * mixed).astype(BF16)

    cq = z_ref[:, OFF_CQ:OFF_CQ + MLA_Q_RANK]
    cqn = _rms(cq, gcq_ref[...]).astype(BF16)
    qm = _dot(cqn, wuq_ref[...])
    for hd in range(MLA_HEADS):
        qh = qm[:, hd * LANES:(hd + 1) * LANES]
        if rope:
            qh = _rope(qh, cos, sin, lane_lo)
        qm_ref[hd] = (qh * MLA_QSCALE).astype(BF16)

    ckv = z_ref[:, OFF_CKV:OFF_CKV + MLA_KV_RANK]
    ckvn = _rms(ckv, gckv_ref[...])
    kr = z_ref[:, OFF_KR:OFF_KR + LANES]
    if emit_f32:
        for sq in range(tm // SEQ):
            ckvf_ref[sq] = ckvn[sq * SEQ:(sq + 1) * SEQ]
        krf_ref[...] = kr
    if rope:
        kr = _rope(kr, cos, sin, lane_lo)
    kv = _dot(ckvn.astype(BF16), wukv_ref[...])
    for hd in range(MLA_HEADS):
        kp_ref[hd] = (kv[:, hd * LANES:(hd + 1) * LANES] + kr).astype(BF16)
    voff = MLA_HEADS * LANES
    for p in range(MLA_HEADS // 2):
        vp_ref[p] = kv[:, voff + p * LANES:voff + (p + 1) * LANES].astype(BF16)

    qa = z_ref[:, OFF_QA:OFF_QA + NA_WIDTH] * NA_QSCALE
    ka = z_ref[:, OFF_KA:OFF_KA + NA_WIDTH]
    va = z_ref[:, OFF_VA:OFF_VA + NA_WIDTH]
    for p in range(NA_HEADS // 2):
        sl = slice(p * LANES, (p + 1) * LANES)
        qa_ref[p] = qa[:, sl].astype(BF16)
        ka_ref[p] = ka[:, sl].astype(BF16)
        va_ref[p] = va[:, sl].astype(BF16)
    if emit_f32:
        for src, dst in ((ka, kaf_ref), (va, vaf_ref)):
            src_t = src.T
            for sq in range(tm // SEQ):
                for hd in range(NA_HEADS):
                    dst[sq, hd] = src_t[hd * HEAD_DIM:(hd + 1) * HEAD_DIM, sq * SEQ:(sq + 1) * SEQ]
        _ctx_attn_kernel(qa_ref, ka_ref, va_ref, qm_ref, kp_ref, vp_ref, attn_ref)


def _project(x, mods, mod_row, pw, l, rope_tabs, stacked, tiles_per_seq):
    emit_f32 = stacked is not None
    t = x.shape[0]
    tm = PROJ_TM
    nt = t // tm
    rope = rope_tabs is not None
    in_specs = [
        pl.BlockSpec((tm, D_MODEL), lambda i: (i, 0)),
        pl.BlockSpec((None, 6, D_MODEL), lambda i: (mod_row(i), 0, 0)),
        _layer_spec(l, 1, D_MODEL),
        _layer_spec(l, IN_COLS, D_MODEL),
        _layer_spec(l, 1, MLA_Q_RANK),
        _layer_spec(l, MLA_Q_RANK, MLA_HEADS * LANES),
        _layer_spec(l, 1, MLA_KV_RANK),
        _layer_spec(l, MLA_KV_RANK, KV_COLS),
        _layer_spec(l, 1, SGU_WIDTH),
        _layer_spec(l, SGU_GROUPS, SGU_CHUNK, SGU_CHUNK),
        _layer_spec(l, SGU_CHUNK, SGU_WIDTH),
    ]
    args = [x, mods, pw["g_mix"], pw["w_in"], pw["g_cq"], pw["w_uq"], pw["g_ckv"], pw["w_ukv"],
            pw["g_sgu"], pw["w_sgu"], pw["b_sgu"]]
    if rope:
        tab_spec = pl.BlockSpec((tm, LANES), lambda i: (i % tiles_per_seq, 0))
        in_specs += [tab_spec, tab_spec]
        args += list(rope_tabs)

    def heads(n):
        return (pl.BlockSpec((n, tm, LANES), lambda i: (0, i, 0)),
                jax.ShapeDtypeStruct((n, t, LANES), BF16))

    def flat(w, dt):
        return (pl.BlockSpec((tm, w), lambda i: (i, 0)), jax.ShapeDtypeStruct((t, w), dt))

    head_major = [heads(2), heads(2), heads(2), heads(MLA_HEADS), heads(MLA_HEADS), heads(MLA_HEADS // 2)]
    scratch = [pltpu.VMEM((tm, D_MODEL), BF16), pltpu.VMEM((tm, IN_COLS), F32)]
    if emit_f32:
        assert tm == CTX_SEQS * SEQ
        outs = [flat(NA_WIDTH + MLA_WIDTH, BF16), flat(SGU_WIDTH, BF16)]
        scratch += [pltpu.VMEM(spec.block_shape, BF16) for spec, _ in head_major]
    else:
        outs = head_major + [flat(SGU_WIDTH, BF16)]

    def layer_slab(tail):
        zeros = (0,) * len(tail)
        return pl.BlockSpec((tm // SEQ, None) + tail, lambda i: (i, l) + zeros)

    aliases = {}
    if emit_f32:
        for a in stacked:
            aliases[len(args)] = len(outs)
            in_specs.append(pl.BlockSpec(memory_space=pl.ANY))
            args.append(a)
            outs.append((layer_slab(a.shape[2:]), jax.ShapeDtypeStruct(a.shape, F32)))
        outs.append(flat(LANES, F32))
    return pl.pallas_call(
        functools.partial(_proj_kernel, rope, emit_f32),
        grid=(nt,),
        in_specs=in_specs,
        out_specs=[o[0] for o in outs],
        out_shape=[o[1] for o in outs],
        input_output_aliases=aliases,
        scratch_shapes=scratch,
        compiler_params=_cparams(1),
        name="project_lat" if rope else "project_ctx",
    )(*args)


def _cache_kv_kernel(ckv_ref, kr_ref, wukv_ref, kp_ref, vp_ref):
    rows = kp_ref.shape[1]
    kv = _dot(ckv_ref[...].reshape(rows, MLA_KV_RANK).astype(BF16), wukv_ref[...])
    kr = kr_ref[...].reshape(rows, LANES)
    for hd in range(MLA_HEADS):
        kp_ref[hd] = (kv[:, hd * LANES:(hd + 1) * LANES] + kr).astype(BF16)
    voff = MLA_HEADS * LANES
    for p in range(MLA_HEADS // 2):
        vp_ref[p] = kv[:, voff + p * LANES:voff + (p + 1) * LANES].astype(BF16)


def _cache_kv(cache_ckv, cache_kr_pad, w_ukv):
    b = cache_ckv.shape[0]
    return pl.pallas_call(
        _cache_kv_kernel,
        grid=(DEPTH,),
        in_specs=[
            pl.BlockSpec((b, None, PAST_LEN, MLA_KV_RANK), lambda l: (0, l, 0, 0)),
            pl.BlockSpec((b, None, PAST_LEN, LANES), lambda l: (0, l, 0, 0)),
            pl.BlockSpec((None, MLA_KV_RANK, KV_COLS), lambda l: (l, 0, 0)),
        ],
        out_specs=[
            pl.BlockSpec((None, MLA_HEADS, b * PAST_LEN, LANES), lambda l: (l, 0, 0, 0)),
            pl.BlockSpec((None, MLA_HEADS // 2, b * PAST_LEN, LANES), lambda l: (l, 0, 0, 0)),
        ],
        out_shape=[
            jax.ShapeDtypeStruct((DEPTH, MLA_HEADS, b * PAST_LEN, LANES), BF16),
            jax.ShapeDtypeStruct((DEPTH, MLA_HEADS // 2, b * PAST_LEN, LANES), BF16),
        ],
        compiler_params=_cparams(1),
        name="cache_kv",
    )(cache_ckv, cache_kr_pad, w_ukv)


def _ctx_attn_kernel(qa_ref, ka_ref, va_ref, qm_ref, kp_ref, vp_ref, o_ref):
    for sq in range(CTX_SEQS):
        rows = slice(sq * SEQ, (sq + 1) * SEQ)
        for p in range(NA_HEADS // 2 + MLA_HEADS // 2):
            out = None
            for half in range(2):
                own = _half_mask(half)
                if p < NA_HEADS // 2:
                    q = _keep(own, qa_ref[p, rows, :])
                    k = ka_ref[p, rows, :]
                    v = va_ref[p, rows, :]
                else:
                    hd = 2 * (p - NA_HEADS // 2) + half
                    q = qm_ref[hd, rows, :]
                    k = kp_ref[hd, rows, :]
                    v = vp_ref[hd // 2, rows, :]
                s = _dot_nt(q, k)
                p_ = jnp.exp2(s - jnp.max(s, axis=-1, keepdims=True))
                o = _dot(p_.astype(BF16), _keep(own, v)) / jnp.sum(p_, axis=-1, keepdims=True)
                out = o if out is None else out + o
            o_ref[rows, p * LANES:(p + 1) * LANES] = out.astype(BF16)


def _ctx_attention(qa, ka, va, qm, kp, vp, n_seq):
    rows = CTX_SEQS * SEQ
    heads = lambda n: pl.BlockSpec((n, rows, LANES), lambda b: (0, b, 0))
    width = NA_WIDTH + MLA_WIDTH
    return pl.pallas_call(
        _ctx_attn_kernel,
        grid=(n_seq // CTX_SEQS,),
        in_specs=[heads(2), heads(2), heads(2), heads(MLA_HEADS), heads(MLA_HEADS), heads(MLA_HEADS // 2)],
        out_specs=pl.BlockSpec((rows, width), lambda b: (b, 0)),
        out_shape=jax.ShapeDtypeStruct((n_seq * SEQ, width), BF16),
        compiler_params=_cparams(1),
        name="ctx_attention",
    )(qa, ka, va, qm, kp, vp)


def _lat_mla_kernel(q_ref, kc_ref, kl_ref, vc_ref, vl_ref, o_ref, s_ref, m_ref, p_ref):
    nc = kc_ref.shape[1]
    outs = {}

    def scores(hd):
        q = q_ref[hd]
        sc = _dot_nt(q, kc_ref[hd])
        sl = _dot_nt(q, kl_ref[hd])
        s_ref[hd % 2, :, 0:nc] = sc
        s_ref[hd % 2, :, nc:] = sl
        m_ref[hd % 2] = jnp.maximum(_lane_block_max(sc), _lane_block_max(sl))

    def softmax(hd):
        m = jnp.max(m_ref[hd % 2], axis=-1, keepdims=True)
        p_ref[hd % 2] = jnp.exp2(s_ref[hd % 2] - m).astype(BF16)

    def values(hd):
        own = _half_mask(hd % 2)
        o = _normalised(own, hd % 2, [_dot(p_ref[hd % 2, :, 0:nc], _with_ones(own, hd % 2, vc_ref[hd // 2])),
                                      _dot(p_ref[hd % 2, :, nc:], _with_ones(own, hd % 2, vl_ref[hd // 2]))])
        if hd % 2 == 0:
            outs[hd // 2] = o
        else:
            o_ref[:, (hd // 2) * LANES:(hd // 2 + 1) * LANES] = (outs.pop(hd // 2) + o).astype(BF16)

    for t in range(MLA_HEADS + 2):
        if t < MLA_HEADS:
            scores(t)
        if 0 <= t - 1 < MLA_HEADS:
            softmax(t - 1)
        if 0 <= t - 2 < MLA_HEADS:
            values(t - 2)


def _lat_mla(q, kc, vc, kl, vl, l, n_batch):
    nq = DEC_SEQ // TM
    n_keys = PAST_LEN + DEC_SEQ
    return pl.pallas_call(
        _lat_mla_kernel,
        grid=(n_batch, nq),
        in_specs=[pl.BlockSpec((MLA_HEADS, TM, LANES), lambda b, i: (0, b * nq + i, 0)),
                  pl.BlockSpec((None, MLA_HEADS, PAST_LEN, LANES), lambda b, i: (l, 0, b, 0)),
                  pl.BlockSpec((MLA_HEADS, DEC_SEQ, LANES), lambda b, i: (0, b, 0)),
                  pl.BlockSpec((None, MLA_HEADS // 2, PAST_LEN, LANES), lambda b, i: (l, 0, b, 0)),
                  pl.BlockSpec((MLA_HEADS // 2, DEC_SEQ, LANES), lambda b, i: (0, b, 0))],
        out_specs=pl.BlockSpec((TM, MLA_WIDTH), lambda b, i: (b * nq + i, 0)),
        out_shape=jax.ShapeDtypeStruct((n_batch * DEC_SEQ, MLA_WIDTH), BF16),
        scratch_shapes=[pltpu.VMEM((2, TM, n_keys), F32), pltpu.VMEM((2, TM, LANES), F32),
                        pltpu.VMEM((2, TM, n_keys), BF16)],
        compiler_params=_cparams(2),
        name="lat_mla",
    )(q, kc, kl, vc, vl)


def _na_bias_kernel(rpb_ref, o_ref, tab_ref):
    l = pl.program_id(0)
    hd = pl.program_id(1)
    n_dr = 2 * NA_WIN_R - 1
    n_dc = 2 * NA_WIN_C - 1
    base = (l * NA_HEADS + hd) * n_dr * n_dc
    qc = lax.broadcasted_iota(jnp.int32, (GRID_W, LANES), 0)
    kc = lax.broadcasted_iota(jnp.int32, (GRID_W, LANES), 1) % GRID_W
    cs = jnp.clip(qc - NA_WIN_C // 2, 0, GRID_W - NA_WIN_C)
    in_win = (kc >= cs) & (kc < cs + NA_WIN_C)
    neg = jnp.full((GRID_W, LANES), NEG_INF * LOG2E, F32)
    lane = lax.broadcasted_iota(jnp.int32, (GRID_W, LANES), 1)
    for dr in range(n_dr):
        vec = neg
        for dc in range(n_dc):
            vec = jnp.where(lane == dc, rpb_ref[base + dr * n_dc + dc] * LOG2E, vec)
        rot = pltpu.roll(vec, LANES - (NA_WIN_C - 1), axis=1, stride=1, stride_axis=0)
        both = jnp.where(lane < GRID_W, rot, pltpu.roll(rot, GRID_W, axis=1))
        tab_ref[dr] = jnp.where(in_win, both, neg)
    left = lax.broadcasted_iota(jnp.int32, (GRID_W, LANES), 1) < GRID_W
    n_krows = NA_KBLKS * NA_QBLK // GRID_W
    cases = ((0, lambda qr: 0), (-NA_QROWS, lambda qr: qr), (-2 * NA_QROWS, lambda qr: NA_QROWS))
    for c, (shift, first) in enumerate(cases):
        for qr in range(NA_QROWS):
            for kp in range(n_krows // 2):
                tiles = []
                for kr in (2 * kp, 2 * kp + 1):
                    ok = first(qr) <= kr < first(qr) + NA_WIN_R
                    tiles.append(tab_ref[kr - qr + shift + NA_WIN_R - 1] if ok else neg)
                o_ref[c, qr * GRID_W:(qr + 1) * GRID_W, kp * LANES:(kp + 1) * LANES] = (
                    jnp.where(left, tiles[0], tiles[1]))


def _na_bias(na_rpb):
    n_k = NA_KBLKS * NA_QBLK
    return pl.pallas_call(
        _na_bias_kernel,
        grid=(DEPTH, NA_HEADS),
        in_specs=[pl.BlockSpec(memory_space=pltpu.SMEM)],
        out_specs=pl.BlockSpec((None, 3, None, NA_QBLK, n_k), lambda l, h: (l, 0, h, 0, 0)),
        out_shape=jax.ShapeDtypeStruct((DEPTH, 3, NA_HEADS, NA_QBLK, n_k), F32),
        scratch_shapes=[pltpu.VMEM((2 * NA_WIN_R - 1, GRID_W, LANES), F32)],
        compiler_params=_cparams(2),
        name="na_bias",
    )(na_rpb.reshape(-1))


def _lane_block_max(s):
    return functools.reduce(jnp.maximum, [s[:, j * LANES:(j + 1) * LANES] for j in range(s.shape[1] // LANES)])


def _na_kernel(n_sub, q_ref, *refs):
    per = 2 * NA_KBLKS + 1
    subs = [refs[r * per:(r + 1) * per] for r in range(n_sub)]
    kc_ref, vc_ref, o_ref, s_ref, m_ref, p_ref = refs[n_sub * per:]
    items = [(r, hd) for r in range(n_sub) for hd in range(NA_HEADS)]
    outs = {}
    n_loc = NA_KBLKS * NA_QBLK

    def scores(t):
        r, hd = items[t]
        k_refs, bias_ref = subs[r][:NA_KBLKS], subs[r][-1]
        p = hd // 2
        q = _keep(_half_mask(hd % 2), q_ref[p, r * NA_QBLK:(r + 1) * NA_QBLK, :])
        m = None
        for i in range(NA_KBLKS):
            s = _dot_nt(q, k_refs[i][p]) + bias_ref[hd, :, i * NA_QBLK:(i + 1) * NA_QBLK]
            s_ref[t % 2, :, i * NA_QBLK:(i + 1) * NA_QBLK] = s
            m = _lane_block_max(s) if m is None else jnp.maximum(m, _lane_block_max(s))
        s = _dot(q, kc_ref[p * LANES:(p + 1) * LANES, :].astype(BF16))
        s_ref[t % 2, :, n_loc:] = s
        m_ref[t % 2] = jnp.maximum(m, _lane_block_max(s))

    def softmax(t):
        m = jnp.max(m_ref[t % 2], axis=-1, keepdims=True)
        p_ref[t % 2] = jnp.exp2(s_ref[t % 2] - m).astype(BF16)

    def values(t):
        r, hd = items[t]
        v_refs = subs[r][NA_KBLKS:2 * NA_KBLKS]
        p = hd // 2
        own = _half_mask(hd % 2)
        row = lax.broadcasted_iota(jnp.int32, (LANES, 1), 0)
        vc = vc_ref[p * LANES:(p + 1) * LANES, :].astype(BF16)
        vc = jnp.where((row // HALF) == hd % 2, vc, (row == HALF * (1 - hd % 2)).astype(BF16))
        parts = [_dot_nt(p_ref[t % 2, :, n_loc:], vc)]
        for i in range(NA_KBLKS):
            parts.append(_dot(p_ref[t % 2, :, i * NA_QBLK:(i + 1) * NA_QBLK],
                              _with_ones(own, hd % 2, v_refs[i][p])))
        o = _normalised(own, hd % 2, parts)
        if hd % 2 == 0:
            outs[(r, p)] = o
        else:
            o_ref[r * NA_QBLK:(r + 1) * NA_QBLK, p * LANES:(p + 1) * LANES] = (outs.pop((r, p)) + o).astype(BF16)

    for t in range(len(items) + 2):
        if t < len(items):
            scores(t)
        if 0 <= t - 1 < len(items):
            softmax(t - 1)
        if 0 <= t - 2 < len(items):
            values(t - 2)


def _na_attention(qa, ka, va, cache_k, cache_v, bias, l, n_batch):
    nblk = DEC_SEQ // NA_QBLK
    max_start = nblk - NA_KBLKS
    n_sub = NA_SUBS
    nstep = nblk // n_sub
    blk = lambda jj, r: jj * n_sub + r

    def kspec(r, i):
        return pl.BlockSpec((NA_HEADS // 2, NA_QBLK, LANES),
                            lambda jj, b: (0, b * nblk + jnp.clip(blk(jj, r) - 1, 0, max_start) + i, 0))

    case = lambda j: jnp.where(j == 0, 0, jnp.where(j == nblk - 1, 2, 1))
    cache_spec = pl.BlockSpec((None, None, NA_WIDTH, PAST_LEN), lambda jj, b: (b, l, 0, 0))
    in_specs = [pl.BlockSpec((NA_HEADS // 2, n_sub * NA_QBLK, LANES), lambda jj, b: (0, b * nstep + jj, 0))]
    args = [qa]
    for r in range(n_sub):
        in_specs += [kspec(r, i) for i in range(NA_KBLKS)] + [kspec(r, i) for i in range(NA_KBLKS)]
        in_specs.append(pl.BlockSpec((None, None, NA_HEADS, NA_QBLK, NA_KBLKS * NA_QBLK),
                                     lambda jj, b, r=r: (l, case(blk(jj, r)), 0, 0, 0)))
        args += [ka] * NA_KBLKS + [va] * NA_KBLKS + [bias]
    in_specs += [cache_spec, cache_spec]
    args += [cache_k, cache_v]
    return pl.pallas_call(
        functools.partial(_na_kernel, n_sub),
        grid=(nstep, n_batch),
        in_specs=in_specs,
        out_specs=pl.BlockSpec((n_sub * NA_QBLK, NA_WIDTH), lambda jj, b: (b * nstep + jj, 0)),
        out_shape=jax.ShapeDtypeStruct((n_batch * DEC_SEQ, NA_WIDTH), BF16),
        scratch_shapes=[pltpu.VMEM((2, NA_QBLK, (NA_KBLKS + 1) * NA_QBLK), F32),
                        pltpu.VMEM((2, NA_QBLK, LANES), F32),
                        pltpu.VMEM((2, NA_QBLK, (NA_KBLKS + 1) * NA_QBLK), BF16)],
        compiler_params=_cparams(2),
        name="na_attention",
    )(*args)


def _ffn_kernel(seq_len, final, widths, *refs):
    halo = seq_len > TM
    n_in = 3 if halo else 1
    x_refs = refs[:n_in]
    refs = refs[n_in:]
    part_refs = [refs[i * n_in:(i + 1) * n_in] for i in range(len(widths))]
    refs = refs[n_in * len(widths):]
    mod_ref, gffn_ref, wo_ref, win_ref, cw_ref, cb_ref, wout_ref = refs[:7]
    refs = refs[7:]
    if final:
        gfin_ref = refs[0]
        refs = refs[1:]
    o_ref, mix_ref, hext_ref, a_ref, act_ref = refs
    tm = o_ref.shape[0]
    pad = BF16_ROWS
    if halo:
        tiles_per_seq = seq_len // tm
        pos = pl.program_id(0) % tiles_per_seq
        pieces = [(0, 0, slice(0, pad), pos != 0), (pad, 1, slice(0, tm), None),
                  (pad + tm, 2, slice(0, pad), pos != tiles_per_seq - 1)]
        frames = []
        starts = [pad]
        seg_len = tm
    else:
        nseg = tm // seq_len
        starts = [pad + s * (seq_len + pad) for s in range(nseg)]
        seg_len = seq_len
        pieces = [(starts[s], 0, slice(s * seq_len, (s + 1) * seq_len), None) for s in range(nseg)]
        frames = [s * (seq_len + pad) for s in range(nseg + 1)]
    for f in frames:
        mix_ref[f:f + pad] = jnp.zeros((pad, D_MODEL), BF16)
        hext_ref[f:f + pad] = jnp.zeros((pad, D_MODEL), BF16)
    off = 0
    for n, prefs in zip(widths, part_refs):
        for row, src, rows, _ in pieces:
            mix_ref[row:row + rows.stop - rows.start, off:off + n] = prefs[src][rows, :]
        off += n
    r = _dot(mix_ref[...], wo_ref[...])
    g1 = mod_ref[2:3, :]
    scale = gffn_ref[...] * (1.0 + mod_ref[4:5, :])
    shift = mod_ref[3:4, :]
    for row, src, rows, keep in pieces:
        n_rows = rows.stop - rows.start
        x1 = x_refs[src][rows, :] + g1 * r[row:row + n_rows]
        if src == n_in // 2:
            o_ref[rows, :] = x1
        h2 = (x1 * lax.rsqrt(jnp.mean(x1 * x1, axis=-1, keepdims=True) + EPS) * scale + shift).astype(BF16)
        if keep is not None:
            h2 = jnp.where(keep, h2, jnp.zeros_like(h2))
        hext_ref[row:row + n_rows] = h2

    def col(c, lb):
        half = FF_LB // 2
        return (lb // half) * D_FF + c * FF_CHUNK + (lb % half) * LANES

    def up(c):
        hext = hext_ref[...]
        for part in range(2):
            a = _dot(hext, win_ref[:, part * D_FF + c * FF_CHUNK:part * D_FF + (c + 1) * FF_CHUNK])
            for j in range(FF_LB // 2):
                a_ref[c % 2, part * (FF_LB // 2) + j] = a[:, j * LANES:(j + 1) * LANES]

    def conv(c, lb, st):
        cs = slice(col(c, lb), col(c, lb) + LANES)
        return (a_ref[c % 2, lb, pl.ds(st - 1, seg_len, stride=1), :] * cw_ref[0:1, cs]
                + a_ref[c % 2, lb, st:st + seg_len, :] * cw_ref[1:2, cs]
                + a_ref[c % 2, lb, pl.ds(st + 1, seg_len, stride=1), :] * cw_ref[2:3, cs] + cb_ref[:, cs])

    up(0)
    for c in range(N_FF_CHUNKS):
        if c + 1 < N_FF_CHUNKS:
            up(c + 1)
        for s, st in enumerate(starts):
            for lb in range(FF_LB // 2):
                gate = conv(c, lb, st)
                val = conv(c, lb + FF_LB // 2, st)
                act_ref[s * seg_len:(s + 1) * seg_len, c * FF_CHUNK + lb * LANES:c * FF_CHUNK + (lb + 1) * LANES] = (
                    gate * jax.nn.sigmoid(gate) * val).astype(BF16)
    y = o_ref[...] + mod_ref[5:6, :] * _dot(act_ref[...], wout_ref[...])
    if final:
        y = _rms(y, gfin_ref[...])
    o_ref[...] = y


def _ffn(x, parts, mods, mod_row, pw, l, seq_len, g_final, name):
    t = x.shape[0]
    nt = t // TM
    hb = TM // BF16_ROWS
    n_hblk = t // BF16_ROWS
    final = g_final is not None
    halo = seq_len > TM
    ext = TM + 2 * BF16_ROWS if halo else TM + (TM // seq_len + 1) * BF16_ROWS
    widths = tuple(p.shape[1] for p in parts)

    def tiled(a):
        w = a.shape[1]
        main = pl.BlockSpec((TM, w), lambda i: (i, 0))
        if not halo:
            return [main], [a]
        prev = pl.BlockSpec((BF16_ROWS, w), lambda i: (jnp.maximum(i * hb - 1, 0), 0))
        nxt = pl.BlockSpec((BF16_ROWS, w), lambda i: (jnp.minimum((i + 1) * hb, n_hblk - 1), 0))
        return [prev, main, nxt], [a, a, a]

    in_specs, args = [], []
    for a in (x,) + tuple(parts):
        sp, ar = tiled(a)
        in_specs += sp
        args += ar
    in_specs += [pl.BlockSpec((None, 6, D_MODEL), lambda i: (mod_row(i), 0, 0)),
                 _layer_spec(l, 1, D_MODEL),
                 _layer_spec(l, D_MODEL, D_MODEL, single=True),
                 _layer_spec(l, D_MODEL, 2 * D_FF, single=True),
                 _layer_spec(l, 3, 2 * D_FF, single=True),
                 _layer_spec(l, 1, 2 * D_FF, single=True),
                 _layer_spec(l, D_FF, D_MODEL, single=True)]
    args += [mods, pw["g_ffn"], pw["w_out"], pw["w_ffn_in"], pw["conv_w"], pw["conv_b"], pw["w_ffn_out"]]
    if final:
        in_specs.append(pl.BlockSpec((1, D_MODEL), lambda i: (0, 0)))
        args.append(g_final)
    return pl.pallas_call(
        functools.partial(_ffn_kernel, seq_len, final, widths),
        grid=(nt,),
        in_specs=in_specs,
        out_specs=pl.BlockSpec((TM, D_MODEL), lambda i: (i, 0)),
        out_shape=jax.ShapeDtypeStruct((t, D_MODEL), F32),
        scratch_shapes=[pltpu.VMEM((ext, D_MODEL), BF16), pltpu.VMEM((ext, D_MODEL), BF16),
                        pltpu.VMEM((2, FF_LB, ext, LANES), F32), pltpu.VMEM((TM, D_FF), BF16)],
        compiler_params=_cparams(1),
        name=name,
    )(*args)


def _pack_mla_kernel(wuq_ref, wukv_ref, ouq_ref, oukv_ref):
    dq = MLA_NOPE + MLA_ROPE
    zq = jnp.zeros((MLA_Q_RANK, LANES - dq), F32)
    zk = jnp.zeros((MLA_KV_RANK, LANES - MLA_NOPE), F32)
    dkv = MLA_NOPE + MLA_V
    for hd in range(MLA_HEADS):
        ouq_ref[:, hd * LANES:(hd + 1) * LANES] = jnp.concatenate(
            [wuq_ref[:, hd * dq:(hd + 1) * dq], zq], axis=-1).astype(BF16)
        oukv_ref[:, hd * LANES:(hd + 1) * LANES] = jnp.concatenate(
            [wukv_ref[:, hd * dkv:hd * dkv + MLA_NOPE], zk], axis=-1).astype(BF16)
    voff = MLA_HEADS * LANES
    for p in range(MLA_HEADS // 2):
        oukv_ref[:, voff + p * LANES:voff + (p + 1) * LANES] = jnp.concatenate(
            [wukv_ref[:, (2 * p + j) * dkv + MLA_NOPE:(2 * p + j + 1) * dkv] for j in range(2)], axis=-1).astype(BF16)


def _pack_weights(w_in, w_uq, w_ukv, w_sgu, b_sgu, w_out, w_ffn_in, ffn_conv_w, ffn_conv_b, w_ffn_out,
                  g_mix, g_cq, g_ckv, g_sgu, g_ffn):
    nl = w_in.shape[0]
    wt = jnp.swapaxes(w_in, 1, 2).astype(BF16)
    zr = lambda n: jnp.zeros((nl, n, D_MODEL), BF16)
    w_in_p = jnp.concatenate([wt[:, :OFF_KR], zr(HALF), wt[:, OFF_KR:OFF_KR + MLA_ROPE],
                              zr(LANES - HALF - MLA_ROPE), wt[:, OFF_KR + MLA_ROPE:]], axis=1)
    w_uq_p, w_ukv_p = pl.pallas_call(
        _pack_mla_kernel,
        grid=(nl,),
        in_specs=[pl.BlockSpec((None,) + w_uq.shape[1:], lambda l: (l, 0, 0)),
                  pl.BlockSpec((None,) + w_ukv.shape[1:], lambda l: (l, 0, 0))],
        out_specs=[pl.BlockSpec((None, MLA_Q_RANK, MLA_HEADS * LANES), lambda l: (l, 0, 0)),
                   pl.BlockSpec((None, MLA_KV_RANK, KV_COLS), lambda l: (l, 0, 0))],
        out_shape=[jax.ShapeDtypeStruct((nl, MLA_Q_RANK, MLA_HEADS * LANES), BF16),
                   jax.ShapeDtypeStruct((nl, MLA_KV_RANK, KV_COLS), BF16)],
        compiler_params=_cparams(1),
        name="pack_w_mla",
    )(w_uq, w_ukv)
    b_sgu_p = jnp.repeat(jnp.swapaxes(b_sgu, 1, 2), SGU_WIDTH // SGU_GROUPS, axis=-1)
    return dict(
        w_in=w_in_p, w_uq=w_uq_p, w_ukv=w_ukv_p, w_sgu=w_sgu.astype(BF16), b_sgu=b_sgu_p,
        w_out=w_out.astype(BF16), w_ffn_in=w_ffn_in.astype(BF16), conv_w=ffn_conv_w,
        conv_b=ffn_conv_b[:, None, :], w_ffn_out=w_ffn_out.astype(BF16),
        g_mix=g_mix[:, None, :], g_cq=g_cq[:, None, :], g_ckv=g_ckv[:, None, :], g_sgu=g_sgu[:, None, :],
        g_ffn=g_ffn[:, None, :])


def _rope_tables(n_tokens):
    t = jnp.arange(n_tokens)
    n_freq = MLA_ROPE // 4
    inv_freq = ROPE_THETA ** (-jnp.arange(n_freq, dtype=F32) / n_freq)
    ang_r = (t // GRID_W).astype(F32)[:, None] * inv_freq
    ang_c = (t % GRID_W).astype(F32)[:, None] * inv_freq
    ones = jnp.ones((n_tokens, HALF), F32)
    tail = LANES - HALF - MLA_ROPE
    cos = jnp.concatenate([ones, jnp.cos(ang_r), jnp.cos(ang_r), jnp.cos(ang_c), jnp.cos(ang_c),
                           ones[:, :tail]], axis=-1)
    sin = jnp.concatenate([0 * ones, -jnp.sin(ang_r), jnp.sin(ang_r), -jnp.sin(ang_c), jnp.sin(ang_c),
                           0 * ones[:, :tail]], axis=-1)
    return cos, sin


def kernel(x_prompt, x_sample, cache_na_k, cache_na_v, cache_mla_ckv, cache_mla_krope, c, c_ctx, w_mod, b_mod,
           g_mix, w_in, na_rpb, g_cq, w_uq, g_ckv, w_ukv, g_sgu, w_sgu, b_sgu, w_out, g_ffn, w_ffn_in,
           ffn_conv_w, ffn_conv_b, w_ffn_out, g_final):
    n_ctx, n_lat = x_prompt.shape[0], x_sample.shape[0]
    t_ctx, t_lat = n_ctx * SEQ, n_lat * DEC_SEQ
    pw = _pack_weights(w_in, w_uq, w_ukv, w_sgu, b_sgu, w_out, w_ffn_in, ffn_conv_w, ffn_conv_b,
                       w_ffn_out, g_mix, g_cq, g_ckv, g_sgu, g_ffn)
    g_fin = g_final[None, :]

    cond = jnp.concatenate([c_ctx[None, :], c, jnp.zeros((8 - 1 - n_lat, D_MODEL), F32)], axis=0)
    mods = _modulation(cond, w_mod, b_mod)
    rope_tabs = _rope_tables(DEC_SEQ)
    na_bias = _na_bias(na_rpb)
    kr_pad = jnp.pad(cache_mla_krope, ((0, 0), (0, 0), (0, 0), (HALF, LANES - HALF - MLA_ROPE)))
    cache_kp, cache_vp = _cache_kv(cache_mla_ckv, kr_pad, pw["w_ukv"])
    channel_major = lambda a: jnp.transpose(a, (0, 1, 3, 4, 2)).reshape(n_lat, DEPTH, NA_WIDTH, PAST_LEN)
    cache_k = channel_major(cache_na_k)
    cache_v = channel_major(cache_na_v)

    lat_tiles = DEC_SEQ // TM
    proj_tiles = DEC_SEQ // PROJ_TM
    ctx_row = lambda i: 0
    lat_row = lambda i: 1 + i // lat_tiles
    lat_proj_row = lambda i: 1 + i // proj_tiles

    xp = x_prompt.reshape(t_ctx, D_MODEL)
    xs = x_sample.reshape(t_lat, D_MODEL)
    per_head = (n_ctx, DEPTH, NA_HEADS, HEAD_DIM, SEQ)
    ctx_out = (jnp.zeros(per_head, F32), jnp.zeros(per_head, F32), jnp.zeros((n_ctx, DEPTH, SEQ, MLA_KV_RANK), F32))
    new_kr = []
    for l in range(DEPTH):
        last = l == DEPTH - 1
        m = mods[l]
        o_ab, oc, *ctx_out, kr_f = _project(xp, m, ctx_row, pw, l, None, ctx_out, 1)
        new_kr.append(kr_f[:, HALF:HALF + MLA_ROPE])
        xp = _ffn(xp, (o_ab, oc), m, ctx_row, pw, l, SEQ, g_fin if last else None, "ctx_ffn")

        qa, ka, va, qm, kp, vp, oc = _project(xs, m, lat_proj_row, pw, l, rope_tabs, None, proj_tiles)
        o_a = _na_attention(qa, ka, va, cache_k, cache_v, na_bias, l, n_lat)
        o_b = _lat_mla(qm, cache_kp, cache_vp, kp, vp, l, n_lat)
        xs = _ffn(xs, (o_a, o_b, oc), m, lat_row, pw, l, DEC_SEQ, g_fin if last else None, "lat_ffn")

    token_major = lambda a: jnp.transpose(a, (0, 1, 4, 2, 3))
    new_k, new_v, new_ckv = ctx_out
    return (xp.reshape(n_ctx, SEQ, D_MODEL), xs.reshape(n_lat, DEC_SEQ, D_MODEL),
            token_major(new_k), token_major(new_v), new_ckv,
            jnp.stack([a.reshape(n_ctx, SEQ, MLA_ROPE) for a in new_kr], axis=1))
```

```python
import functools
import math

import jax
import jax.numpy as jnp
from jax import lax
from jax.experimental import pallas as pl
from jax.experimental.pallas import tpu as pltpu

F32 = jnp.float32
BF16 = jnp.bfloat16

D_MODEL = 1024
DEPTH = 4
SEQ = 256
DEC_SEQ = 2048
PAST_LEN = 256
GRID_W = 64
HEAD_DIM = 64
NA_WIDTH = 256
NA_HEADS = 4
NA_WIN_R = 8
NA_WIN_C = 16
MLA_HEADS = 8
MLA_NOPE = 64
MLA_ROPE = 32
MLA_V = 64
MLA_WIDTH = MLA_HEADS * MLA_V
MLA_Q_RANK = 384
MLA_KV_RANK = 256
SGU_WIDTH = 256
SGU_GROUPS = 4
SGU_CHUNK = 128
D_FF = 2816
ROPE_THETA = 10000.0
EPS = 1e-6
NEG_INF = -1e30
LOG2E = math.log2(math.e)
NA_QSCALE = HEAD_DIM ** -0.5 * LOG2E
MLA_QSCALE = (MLA_NOPE + MLA_ROPE) ** -0.5 * LOG2E

LANES = 128
BF16_ROWS = 16
HALF = LANES // 2

OFF_QA, OFF_KA, OFF_VA = 0, 256, 512
OFF_CQ = 768
OFF_CKV = OFF_CQ + MLA_Q_RANK
OFF_KR = OFF_CKV + MLA_KV_RANK
OFF_UV = OFF_KR + LANES
IN_COLS = OFF_UV + 2 * SGU_WIDTH
KV_COLS = MLA_HEADS * LANES + MLA_HEADS * MLA_V

FF_CHUNK = 256
N_FF_CHUNKS = D_FF // FF_CHUNK
FF_LB = 2 * FF_CHUNK // LANES

TM = 512
PROJ_TM = 1024
CTX_SEQS = 4
NA_SUBS = 4
NA_QROWS = 4
NA_QBLK = NA_QROWS * GRID_W
NA_KBLKS = 3
VMEM_LIMIT = 56 * 1024 * 1024


def _cparams(n_axes):
    return pltpu.CompilerParams(dimension_semantics=("arbitrary",) * n_axes,
                                vmem_limit_bytes=VMEM_LIMIT)


def _layer_spec(l, *shape, single=False):
    mode = dict(pipeline_mode=pl.Buffered(1)) if single else {}
    return pl.BlockSpec((None,) + shape, lambda *_: (l,) + (0,) * len(shape), **mode)


def _rms(x, g):
    ms = jnp.mean(x * x, axis=-1, keepdims=True)
    return x * lax.rsqrt(ms + EPS) * g


def _dot(a, b):
    return jnp.dot(a, b, preferred_element_type=F32)


def _dot_nt(a, b):
    return lax.dot_general(a, b, (((1,), (1,)), ((), ())), preferred_element_type=F32)


def _half_mask(parity):
    lane = lax.broadcasted_iota(jnp.int32, (1, LANES), 1)
    return (lane // HALF) == parity


def _keep(mask, x):
    return jnp.where(mask, x, jnp.zeros_like(x))


def _with_ones(own, parity, v):
    lane = lax.broadcasted_iota(jnp.int32, (1, LANES), 1)
    ones = (lane == HALF * (1 - parity)).astype(v.dtype)
    return jnp.where(own, v, ones)


def _normalised(own, parity, parts):
    o = functools.reduce(jnp.add, parts)
    spare = HALF * (1 - parity)
    return jnp.where(own, o, 0.0) / o[:, spare:spare + 1]


def _mod_kernel(c_ref, w_ref, b_ref, o_ref):
    c = c_ref[...]
    s = c * jax.nn.sigmoid(c)
    o_ref[...] = _dot(s.astype(BF16), w_ref[...].astype(BF16)) + b_ref[...]


def _modulation(cond, w_mod, b_mod):
    n = 6
    wide = 2 * D_MODEL
    out = pl.pallas_call(
        _mod_kernel,
        grid=(DEPTH, n * D_MODEL // wide),
        in_specs=[
            pl.BlockSpec((8, D_MODEL), lambda l, j: (0, 0)),
            pl.BlockSpec((None, D_MODEL, wide), lambda l, j: (l, 0, j)),
            pl.BlockSpec((None, 1, wide), lambda l, j: (l, 0, j)),
        ],
        out_specs=pl.BlockSpec((None, 8, wide), lambda l, j: (l, 0, j)),
        out_shape=jax.ShapeDtypeStruct((DEPTH, 8, n * D_MODEL), F32),
        compiler_params=_cparams(2),
        name="modulation",
    )(cond, w_mod, b_mod.reshape(DEPTH, 1, n * D_MODEL))
    return out.reshape(DEPTH, 8, n, D_MODEL)


def _rope(x, cos, sin, lane_lo):
    up = pltpu.roll(x, LANES - 8, axis=1)
    dn = pltpu.roll(x, 8, axis=1)
    return x * cos + jnp.where(lane_lo, up, dn) * sin


def _proj_kernel(rope, emit_f32, x_ref, mod_ref, gmix_ref, win_ref, gcq_ref, wuq_ref, gckv_ref,
                 wukv_ref, gsgu_ref, wsgu_ref, bsg_ref, *refs):
    if rope:
        cos_ref, sin_ref = refs[:2]
        refs = refs[2:]
    if emit_f32:
        refs = refs[3:]
    if emit_f32:
        attn_ref, oc_ref, kaf_ref, vaf_ref, ckvf_ref, krf_ref = refs[:6]
        hb_ref, z_ref, qa_ref, ka_ref, va_ref, qm_ref, kp_ref, vp_ref = refs[6:]
    else:
        qa_ref, ka_ref, va_ref, qm_ref, kp_ref, vp_ref, oc_ref = refs[:7]
        hb_ref, z_ref = refs[7:]

    x = x_ref[...]
    h = _rms(x, gmix_ref[...]) * (1.0 + mod_ref[1:2, :]) + mod_ref[0:1, :]
    hb_ref[...] = h.astype(BF16)
    tm = x.shape[0]
    z_ref[:, OFF_UV:] = _dot_nt(hb_ref[...], win_ref[OFF_UV:, :])
    z_ref[:, OFF_CQ:OFF_UV] = _dot_nt(hb_ref[...], win_ref[OFF_CQ:OFF_UV, :])
    z_ref[:, :OFF_CQ] = _dot_nt(hb_ref[...], win_ref[:OFF_CQ, :])
    lane = lax.broadcasted_iota(jnp.int32, (1, LANES), 1)
    lane_lo = (lane % 16) < 8
    if rope:
        cos = cos_ref[...]
        sin = sin_ref[...]

    uv = jax.nn.gelu(z_ref[:, OFF_UV:OFF_UV + 2 * SGU_WIDTH])
    u = uv[:, :SGU_WIDTH]
    vn = _rms(uv[:, SGU_WIDTH:], gsgu_ref[...])
    even = _half_mask(0)
    for ch in range(tm // SGU_CHUNK):
        rows = slice(ch * SGU_CHUNK, (ch + 1) * SGU_CHUNK)
        for p in range(SGU_GROUPS // 2):
            sl = slice(p * LANES, (p + 1) * LANES)
            vc = vn[rows, sl]
            mixed = (_dot(wsgu_ref[2 * p], jnp.where(even, vc, 0.0).astype(BF16))
                     + _dot(wsgu_ref[2 * p + 1], jnp.where(even, 0.0, vc).astype(BF16))
                     + bsg_ref[:, sl])
            oc_ref[rows, sl] = (u[rows, sl] * mixed).astype(BF16)

    cq = z_ref[:, OFF_CQ:OFF_CQ + MLA_Q_RANK]
    cqn = _rms(cq, gcq_ref[...]).astype(BF16)
    qm = _dot(cqn, wuq_ref[...])
    for hd in range(MLA_HEADS):
        qh = qm[:, hd * LANES:(hd + 1) * LANES]
        if rope:
            qh = _rope(qh, cos, sin, lane_lo)
        qm_ref[hd] = (qh * MLA_QSCALE).astype(BF16)

    ckv = z_ref[:, OFF_CKV:OFF_CKV + MLA_KV_RANK]
    ckvn = _rms(ckv, gckv_ref[...])
    kr = z_ref[:, OFF_KR:OFF_KR + LANES]
    if emit_f32:
        for sq in range(tm // SEQ):
            ckvf_ref[sq] = ckvn[sq * SEQ:(sq + 1) * SEQ]
        krf_ref[...] = kr
    if rope:
        kr = _rope(kr, cos, sin, lane_lo)
    kv = _dot(ckvn.astype(BF16), wukv_ref[...])
    for hd in range(MLA_HEADS):
        kp_ref[hd] = (kv[:, hd * LANES:(hd + 1) * LANES] + kr).astype(BF16)
    voff = MLA_HEADS * LANES
    for p in range(MLA_HEADS // 2):
        vp_ref[p] = kv[:, voff + p * LANES:voff + (p + 1) * LANES].astype(BF16)

    qa = z_ref[:, OFF_QA:OFF_QA + NA_WIDTH] * NA_QSCALE
    ka = z_ref[:, OFF_KA:OFF_KA + NA_WIDTH]
    va = z_ref[:, OFF_VA:OFF_VA + NA_WIDTH]
    for p in range(NA_HEADS // 2):
        sl = slice(p * LANES, (p + 1) * LANES)
        qa_ref[p] = qa[:, sl].astype(BF16)
        ka_ref[p] = ka[:, sl].astype(BF16)
        va_ref[p] = va[:, sl].astype(BF16)
    if emit_f32:
        for src, dst in ((ka, kaf_ref), (va, vaf_ref)):
            src_t = src.T
            for sq in range(tm // SEQ):
                for hd in range(NA_HEADS):
                    dst[sq, hd] = src_t[hd * HEAD_DIM:(hd + 1) * HEAD_DIM, sq * SEQ:(sq + 1) * SEQ]
        _ctx_attn_kernel(qa_ref, ka_ref, va_ref, qm_ref, kp_ref, vp_ref, attn_ref)


def _project(x, mods, mod_row, pw, l, rope_tabs, stacked, tiles_per_seq):
    emit_f32 = stacked is not None
    t = x.shape[0]
    tm = PROJ_TM
    nt = t // tm
    rope = rope_tabs is not None
    in_specs = [
        pl.BlockSpec((tm, D_MODEL), lambda i: (i, 0)),
        pl.BlockSpec((None, 6, D_MODEL), lambda i: (mod_row(i), 0, 0)),
        _layer_spec(l, 1, D_MODEL),
        _layer_spec(l, IN_COLS, D_MODEL),
        _layer_spec(l, 1, MLA_Q_RANK),
        _layer_spec(l, MLA_Q_RANK, MLA_HEADS * LANES),
        _layer_spec(l, 1, MLA_KV_RANK),
        _layer_spec(l, MLA_KV_RANK, KV_COLS),
        _layer_spec(l, 1, SGU_WIDTH),
        _layer_spec(l, SGU_GROUPS, SGU_CHUNK, SGU_CHUNK),
        _layer_spec(l, SGU_CHUNK, SGU_WIDTH),
    ]
    args = [x, mods, pw["g_mix"], pw["w_in"], pw["g_cq"], pw["w_uq"], pw["g_ckv"], pw["w_ukv"],
            pw["g_sgu"], pw["w_sgu"], pw["b_sgu"]]
    if rope:
        tab_spec = pl.BlockSpec((tm, LANES), lambda i: (i % tiles_per_seq, 0))
        in_specs += [tab_spec, tab_spec]
        args += list(rope_tabs)

    def heads(n):
        return (pl.BlockSpec((n, tm, LANES), lambda i: (0, i, 0)),
                jax.ShapeDtypeStruct((n, t, LANES), BF16))

    def flat(w, dt):
        return (pl.BlockSpec((tm, w), lambda i: (i, 0)), jax.ShapeDtypeStruct((t, w), dt))

    head_major = [heads(2), heads(2), heads(2), heads(MLA_HEADS), heads(MLA_HEADS), heads(MLA_HEADS // 2)]
    scratch = [pltpu.VMEM((tm, D_MODEL), BF16), pltpu.VMEM((tm, IN_COLS), F32)]
    if emit_f32:
        assert tm == CTX_SEQS * SEQ
        outs = [flat(NA_WIDTH + MLA_WIDTH, BF16), flat(SGU_WIDTH, BF16)]
        scratch += [pltpu.VMEM(spec.block_shape, BF16) for spec, _ in head_major]
    else:
        outs = head_major + [flat(SGU_WIDTH, BF16)]

    def layer_slab(tail):
        zeros = (0,) * len(tail)
        return pl.BlockSpec((tm // SEQ, None) + tail, lambda i: (i, l) + zeros)

    aliases = {}
    if emit_f32:
        for a in stacked:
            aliases[len(args)] = len(outs)
            in_specs.append(pl.BlockSpec(memory_space=pl.ANY))
            args.append(a)
            outs.append((layer_slab(a.shape[2:]), jax.ShapeDtypeStruct(a.shape, F32)))
        outs.append(flat(LANES, F32))
    return pl.pallas_call(
        functools.partial(_proj_kernel, rope, emit_f32),
        grid=(nt,),
        in_specs=in_specs,
        out_specs=[o[0] for o in outs],
        out_shape=[o[1] for o in outs],
        input_output_aliases=aliases,
        scratch_shapes=scratch,
        compiler_params=_cparams(1),
        name="project_lat" if rope else "project_ctx",
    )(*args)


def _cache_kv_kernel(ckv_ref, kr_ref, wukv_ref, kp_ref, vp_ref):
    rows = kp_ref.shape[1]
    kv = _dot(ckv_ref[...].reshape(rows, MLA_KV_RANK).astype(BF16), wukv_ref[...])
    kr = kr_ref[...].reshape(rows, LANES)
    for hd in range(MLA_HEADS):
        kp_ref[hd] = (kv[:, hd * LANES:(hd + 1) * LANES] + kr).astype(BF16)
    voff = MLA_HEADS * LANES
    for p in range(MLA_HEADS // 2):
        vp_ref[p] = kv[:, voff + p * LANES:voff + (p + 1) * LANES].astype(BF16)


def _cache_kv(cache_ckv, cache_kr_pad, w_ukv):
    b = cache_ckv.shape[0]
    return pl.pallas_call(
        _cache_kv_kernel,
        grid=(DEPTH,),
        in_specs=[
            pl.BlockSpec((b, None, PAST_LEN, MLA_KV_RANK), lambda l: (0, l, 0, 0)),
            pl.BlockSpec((b, None, PAST_LEN, LANES), lambda l: (0, l, 0, 0)),
            pl.BlockSpec((None, MLA_KV_RANK, KV_COLS), lambda l: (l, 0, 0)),
        ],
        out_specs=[
            pl.BlockSpec((None, MLA_HEADS, b * PAST_LEN, LANES), lambda l: (l, 0, 0, 0)),
            pl.BlockSpec((None, MLA_HEADS // 2, b * PAST_LEN, LANES), lambda l: (l, 0, 0, 0)),
        ],
        out_shape=[
            jax.ShapeDtypeStruct((DEPTH, MLA_HEADS, b * PAST_LEN, LANES), BF16),
            jax.ShapeDtypeStruct((DEPTH, MLA_HEADS // 2, b * PAST_LEN, LANES), BF16),
        ],
        compiler_params=_cparams(1),
        name="cache_kv",
    )(cache_ckv, cache_kr_pad, w_ukv)


def _ctx_attn_kernel(qa_ref, ka_ref, va_ref, qm_ref, kp_ref, vp_ref, o_ref):
    for sq in range(CTX_SEQS):
        rows = slice(sq * SEQ, (sq + 1) * SEQ)
        for p in range(NA_HEADS // 2 + MLA_HEADS // 2):
            out = None
            for half in range(2):
                own = _half_mask(half)
                if p < NA_HEADS // 2:
                    q = _keep(own, qa_ref[p, rows, :])
                    k = ka_ref[p, rows, :]
                    v = va_ref[p, rows, :]
                else:
                    hd = 2 * (p - NA_HEADS // 2) + half
                    q = qm_ref[hd, rows, :]
                    k = kp_ref[hd, rows, :]
                    v = vp_ref[hd // 2, rows, :]
                s = _dot_nt(q, k)
                p_ = jnp.exp2(s - jnp.max(s, axis=-1, keepdims=True))
                o = _dot(p_.astype(BF16), _keep(own, v)) / jnp.sum(p_, axis=-1, keepdims=True)
                out = o if out is None else out + o
            o_ref[rows, p * LANES:(p + 1) * LANES] = out.astype(BF16)


def _ctx_attention(qa, ka, va, qm, kp, vp, n_seq):
    rows = CTX_SEQS * SEQ
    heads = lambda n: pl.BlockSpec((n, rows, LANES), lambda b: (0, b, 0))
    width = NA_WIDTH + MLA_WIDTH
    return pl.pallas_call(
        _ctx_attn_kernel,
        grid=(n_seq // CTX_SEQS,),
        in_specs=[heads(2), heads(2), heads(2), heads(MLA_HEADS), heads(MLA_HEADS), heads(MLA_HEADS // 2)],
        out_specs=pl.BlockSpec((rows, width), lambda b: (b, 0)),
        out_shape=jax.ShapeDtypeStruct((n_seq * SEQ, width), BF16),
        compiler_params=_cparams(1),
        name="ctx_attention",
    )(qa, ka, va, qm, kp, vp)


def _lat_mla_kernel(q_ref, kc_ref, kl_ref, vc_ref, vl_ref, o_ref, s_ref, m_ref, p_ref):
    nc = kc_ref.shape[1]
    outs = {}

    def scores(hd):
        q = q_ref[hd]
        sc = _dot_nt(q, kc_ref[hd])
        sl = _dot_nt(q, kl_ref[hd])
        s_ref[hd % 2, :, 0:nc] = sc
        s_ref[hd % 2, :, nc:] = sl
        m_ref[hd % 2] = jnp.maximum(_lane_block_max(sc), _lane_block_max(sl))

    def softmax(hd):
        m = jnp.max(m_ref[hd % 2], axis=-1, keepdims=True)
        p_ref[hd % 2] = jnp.exp2(s_ref[hd % 2] - m).astype(BF16)

    def values(hd):
        own = _half_mask(hd % 2)
        o = _normalised(own, hd % 2, [_dot(p_ref[hd % 2, :, 0:nc], _with_ones(own, hd % 2, vc_ref[hd // 2])),
                                      _dot(p_ref[hd % 2, :, nc:], _with_ones(own, hd % 2, vl_ref[hd // 2]))])
        if hd % 2 == 0:
            outs[hd // 2] = o
        else:
            o_ref[:, (hd // 2) * LANES:(hd // 2 + 1) * LANES] = (outs.pop(hd // 2) + o).astype(BF16)

    for t in range(MLA_HEADS + 2):
        if t < MLA_HEADS:
            scores(t)
        if 0 <= t - 1 < MLA_HEADS:
            softmax(t - 1)
        if 0 <= t - 2 < MLA_HEADS:
            values(t - 2)


def _lat_mla(q, kc, vc, kl, vl, l, n_batch):
    nq = DEC_SEQ // TM
    n_keys = PAST_LEN + DEC_SEQ
    return pl.pallas_call(
        _lat_mla_kernel,
        grid=(n_batch, nq),
        in_specs=[pl.BlockSpec((MLA_HEADS, TM, LANES), lambda b, i: (0, b * nq + i, 0)),
                  pl.BlockSpec((None, MLA_HEADS, PAST_LEN, LANES), lambda b, i: (l, 0, b, 0)),
                  pl.BlockSpec((MLA_HEADS, DEC_SEQ, LANES), lambda b, i: (0, b, 0)),
                  pl.BlockSpec((None, MLA_HEADS // 2, PAST_LEN, LANES), lambda b, i: (l, 0, b, 0)),
                  pl.BlockSpec((MLA_HEADS // 2, DEC_SEQ, LANES), lambda b, i: (0, b, 0))],
        out_specs=pl.BlockSpec((TM, MLA_WIDTH), lambda b, i: (b * nq + i, 0)),
        out_shape=jax.ShapeDtypeStruct((n_batch * DEC_SEQ, MLA_WIDTH), BF16),
        scratch_shapes=[pltpu.VMEM((2, TM, n_keys), F32), pltpu.VMEM((2, TM, LANES), F32),
                        pltpu.VMEM((2, TM, n_keys), BF16)],
        compiler_params=_cparams(2),
        name="lat_mla",
    )(q, kc, kl, vc, vl)


def _na_bias_kernel(rpb_ref, o_ref, tab_ref):
    l = pl.program_id(0)
    hd = pl.program_id(1)
    n_dr = 2 * NA_WIN_R - 1
    n_dc = 2 * NA_WIN_C - 1
    base = (l * NA_HEADS + hd) * n_dr * n_dc
    qc = lax.broadcasted_iota(jnp.int32, (GRID_W, LANES), 0)
    kc = lax.broadcasted_iota(jnp.int32, (GRID_W, LANES), 1) % GRID_W
    cs = jnp.clip(qc - NA_WIN_C // 2, 0, GRID_W - NA_WIN_C)
    in_win = (kc >= cs) & (kc < cs + NA_WIN_C)
    neg = jnp.full((GRID_W, LANES), NEG_INF * LOG2E, F32)
    lane = lax.broadcasted_iota(jnp.int32, (GRID_W, LANES), 1)
    for dr in range(n_dr):
        vec = neg
        for dc in range(n_dc):
            vec = jnp.where(lane == dc, rpb_ref[base + dr * n_dc + dc] * LOG2E, vec)
        rot = pltpu.roll(vec, LANES - (NA_WIN_C - 1), axis=1, stride=1, stride_axis=0)
        both = jnp.where(lane < GRID_W, rot, pltpu.roll(rot, GRID_W, axis=1))
        tab_ref[dr] = jnp.where(in_win, both, neg)
    left = lax.broadcasted_iota(jnp.int32, (GRID_W, LANES), 1) < GRID_W
    n_krows = NA_KBLKS * NA_QBLK // GRID_W
    cases = ((0, lambda qr: 0), (-NA_QROWS, lambda qr: qr), (-2 * NA_QROWS, lambda qr: NA_QROWS))
    for c, (shift, first) in enumerate(cases):
        for qr in range(NA_QROWS):
            for kp in range(n_krows // 2):
                tiles = []
                for kr in (2 * kp, 2 * kp + 1):
                    ok = first(qr) <= kr < first(qr) + NA_WIN_R
                    tiles.append(tab_ref[kr - qr + shift + NA_WIN_R - 1] if ok else neg)
                o_ref[c, qr * GRID_W:(qr + 1) * GRID_W, kp * LANES:(kp + 1) * LANES] = (
                    jnp.where(left, tiles[0], tiles[1]))


def _na_bias(na_rpb):
    n_k = NA_KBLKS * NA_QBLK
    return pl.pallas_call(
        _na_bias_kernel,
        grid=(DEPTH, NA_HEADS),
        in_specs=[pl.BlockSpec(memory_space=pltpu.SMEM)],
        out_specs=pl.BlockSpec((None, 3, None, NA_QBLK, n_k), lambda l, h: (l, 0, h, 0, 0)),
        out_shape=jax.ShapeDtypeStruct((DEPTH, 3, NA_HEADS, NA_QBLK, n_k), F32),
        scratch_shapes=[pltpu.VMEM((2 * NA_WIN_R - 1, GRID_W, LANES), F32)],
        compiler_params=_cparams(2),
        name="na_bias",
    )(na_rpb.reshape(-1))


def _lane_block_max(s):
    return functools.reduce(jnp.maximum, [s[:, j * LANES:(j + 1) * LANES] for j in range(s.shape[1] // LANES)])


def _na_kernel(n_sub, q_ref, *refs):
    per = 2 * NA_KBLKS + 1
    subs = [refs[r * per:(r + 1) * per] for r in range(n_sub)]
    kc_ref, vc_ref, o_ref, s_ref, m_ref, p_ref = refs[n_sub * per:]
    items = [(r, hd) for r in range(n_sub) for hd in range(NA_HEADS)]
    outs = {}
    n_loc = NA_KBLKS * NA_QBLK

    def scores(t):
        r, hd = items[t]
        k_refs, bias_ref = subs[r][:NA_KBLKS], subs[r][-1]
        p = hd // 2
        q = _keep(_half_mask(hd % 2), q_ref[p, r * NA_QBLK:(r + 1) * NA_QBLK, :])
        m = None
        for i in range(NA_KBLKS):
            s = _dot_nt(q, k_refs[i][p]) + bias_ref[hd, :, i * NA_QBLK:(i + 1) * NA_QBLK]
            s_ref[t % 2, :, i * NA_QBLK:(i + 1) * NA_QBLK] = s
            m = _lane_block_max(s) if m is None else jnp.maximum(m, _lane_block_max(s))
        s = _dot(q, kc_ref[p * LANES:(p + 1) * LANES, :].astype(BF16))
        s_ref[t % 2, :, n_loc:] = s
        m_ref[t % 2] = jnp.maximum(m, _lane_block_max(s))

    def softmax(t):
        m = jnp.max(m_ref[t % 2], axis=-1, keepdims=True)
        p_ref[t % 2] = jnp.exp2(s_ref[t % 2] - m).astype(BF16)

    def values(t):
        r, hd = items[t]
        v_refs = subs[r][NA_KBLKS:2 * NA_KBLKS]
        p = hd // 2
        own = _half_mask(hd % 2)
        row = lax.broadcasted_iota(jnp.int32, (LANES, 1), 0)
        vc = vc_ref[p * LANES:(p + 1) * LANES, :].astype(BF16)
        vc = jnp.where((row // HALF) == hd % 2, vc, (row == HALF * (1 - hd % 2)).astype(BF16))
        parts = [_dot_nt(p_ref[t % 2, :, n_loc:], vc)]
        for i in range(NA_KBLKS):
            parts.append(_dot(p_ref[t % 2, :, i * NA_QBLK:(i + 1) * NA_QBLK],
                              _with_ones(own, hd % 2, v_refs[i][p])))
        o = _normalised(own, hd % 2, parts)
        if hd % 2 == 0:
            outs[(r, p)] = o
        else:
            o_ref[r * NA_QBLK:(r + 1) * NA_QBLK, p * LANES:(p + 1) * LANES] = (outs.pop((r, p)) + o).astype(BF16)

    for t in range(len(items) + 2):
        if t < len(items):
            scores(t)
        if 0 <= t - 1 < len(items):
            softmax(t - 1)
        if 0 <= t - 2 < len(items):
            values(t - 2)


def _na_attention(qa, ka, va, cache_k, cache_v, bias, l, n_batch):
    nblk = DEC_SEQ // NA_QBLK
    max_start = nblk - NA_KBLKS
    n_sub = NA_SUBS
    nstep = nblk // n_sub
    blk = lambda jj, r: jj * n_sub + r

    def kspec(r, i):
        return pl.BlockSpec((NA_HEADS // 2, NA_QBLK, LANES),
                            lambda jj, b: (0, b * nblk + jnp.clip(blk(jj, r) - 1, 0, max_start) + i, 0))

    case = lambda j: jnp.where(j == 0, 0, jnp.where(j == nblk - 1, 2, 1))
    cache_spec = pl.BlockSpec((None, None, NA_WIDTH, PAST_LEN), lambda jj, b: (b, l, 0, 0))
    in_specs = [pl.BlockSpec((NA_HEADS // 2, n_sub * NA_QBLK, LANES), lambda jj, b: (0, b * nstep + jj, 0))]
    args = [qa]
    for r in range(n_sub):
        in_specs += [kspec(r, i) for i in range(NA_KBLKS)] + [kspec(r, i) for i in range(NA_KBLKS)]
        in_specs.append(pl.BlockSpec((None, None, NA_HEADS, NA_QBLK, NA_KBLKS * NA_QBLK),
                                     lambda jj, b, r=r: (l, case(blk(jj, r)), 0, 0, 0)))
        args += [ka] * NA_KBLKS + [va] * NA_KBLKS + [bias]
    in_specs += [cache_spec, cache_spec]
    args += [cache_k, cache_v]
    return pl.pallas_call(
        functools.partial(_na_kernel, n_sub),
        grid=(nstep, n_batch),
        in_specs=in_specs,
        out_specs=pl.BlockSpec((n_sub * NA_QBLK, NA_WIDTH), lambda jj, b: (b * nstep + jj, 0)),
        out_shape=jax.ShapeDtypeStruct((n_batch * DEC_SEQ, NA_WIDTH), BF16),
        scratch_shapes=[pltpu.VMEM((2, NA_QBLK, (NA_KBLKS + 1) * NA_QBLK), F32),
                        pltpu.VMEM((2, NA_QBLK, LANES), F32),
                        pltpu.VMEM((2, NA_QBLK, (NA_KBLKS + 1) * NA_QBLK), BF16)],
        compiler_params=_cparams(2),
        name="na_attention",
    )(*args)


def _ffn_kernel(seq_len, final, widths, *refs):
    halo = seq_len > TM
    n_in = 3 if halo else 1
    x_refs = refs[:n_in]
    refs = refs[n_in:]
    part_refs = [refs[i * n_in:(i + 1) * n_in] for i in range(len(widths))]
    refs = refs[n_in * len(widths):]
    mod_ref, gffn_ref, wo_ref, win_ref, cw_ref, cb_ref, wout_ref = refs[:7]
    refs = refs[7:]
    if final:
        gfin_ref = refs[0]
        refs = refs[1:]
    o_ref, mix_ref, hext_ref, a_ref, act_ref = refs
    tm = o_ref.shape[0]
    pad = BF16_ROWS
    if halo:
        tiles_per_seq = seq_len // tm
        pos = pl.program_id(0) % tiles_per_seq
        pieces = [(0, 0, slice(0, pad), pos != 0), (pad, 1, slice(0, tm), None),
                  (pad + tm, 2, slice(0, pad), pos != tiles_per_seq - 1)]
        frames = []
        starts = [pad]
        seg_len = tm
    else:
        nseg = tm // seq_len
        starts = [pad + s * (seq_len + pad) for s in range(nseg)]
        seg_len = seq_len
        pieces = [(starts[s], 0, slice(s * seq_len, (s + 1) * seq_len), None) for s in range(nseg)]
        frames = [s * (seq_len + pad) for s in range(nseg + 1)]
    for f in frames:
        mix_ref[f:f + pad] = jnp.zeros((pad, D_MODEL), BF16)
        hext_ref[f:f + pad] = jnp.zeros((pad, D_MODEL), BF16)
    off = 0
    for n, prefs in zip(widths, part_refs):
        for row, src, rows, _ in pieces:
            mix_ref[row:row + rows.stop - rows.start, off:off + n] = prefs[src][rows, :]
        off += n
    r = _dot(mix_ref[...], wo_ref[...])
    g1 = mod_ref[2:3, :]
    scale = gffn_ref[...] * (1.0 + mod_ref[4:5, :])
    shift = mod_ref[3:4, :]
    for row, src, rows, keep in pieces:
        n_rows = rows.stop - rows.start
        x1 = x_refs[src][rows, :] + g1 * r[row:row + n_rows]
        if src == n_in // 2:
            o_ref[rows, :] = x1
        h2 = (x1 * lax.rsqrt(jnp.mean(x1 * x1, axis=-1, keepdims=True) + EPS) * scale + shift).astype(BF16)
        if keep is not None:
            h2 = jnp.where(keep, h2, jnp.zeros_like(h2))
        hext_ref[row:row + n_rows] = h2

    def col(c, lb):
        half = FF_LB // 2
        return (lb // half) * D_FF + c * FF_CHUNK + (lb % half) * LANES

    def up(c):
        hext = hext_ref[...]
        for part in range(2):
            a = _dot(hext, win_ref[:, part * D_FF + c * FF_CHUNK:part * D_FF + (c + 1) * FF_CHUNK])
            for j in range(FF_LB // 2):
                a_ref[c % 2, part * (FF_LB // 2) + j] = a[:, j * LANES:(j + 1) * LANES]

    def conv(c, lb, st):
        cs = slice(col(c, lb), col(c, lb) + LANES)
        return (a_ref[c % 2, lb, pl.ds(st - 1, seg_len, stride=1), :] * cw_ref[0:1, cs]
                + a_ref[c % 2, lb, st:st + seg_len, :] * cw_ref[1:2, cs]
                + a_ref[c % 2, lb, pl.ds(st + 1, seg_len, stride=1), :] * cw_ref[2:3, cs] + cb_ref[:, cs])

    up(0)
    for c in range(N_FF_CHUNKS):
        if c + 1 < N_FF_CHUNKS:
            up(c + 1)
        for s, st in enumerate(starts):
            for lb in range(FF_LB // 2):
                gate = conv(c, lb, st)
                val = conv(c, lb + FF_LB // 2, st)
                act_ref[s * seg_len:(s + 1) * seg_len, c * FF_CHUNK + lb * LANES:c * FF_CHUNK + (lb + 1) * LANES] = (
                    gate * jax.nn.sigmoid(gate) * val).astype(BF16)
    y = o_ref[...] + mod_ref[5:6, :] * _dot(act_ref[...], wout_ref[...])
    if final:
        y = _rms(y, gfin_ref[...])
    o_ref[...] = y


def _ffn(x, parts, mods, mod_row, pw, l, seq_len, g_final, name):
    t = x.shape[0]
    nt = t // TM
    hb = TM // BF16_ROWS
    n_hblk = t // BF16_ROWS
    final = g_final is not None
    halo = seq_len > TM
    ext = TM + 2 * BF16_ROWS if halo else TM + (TM // seq_len + 1) * BF16_ROWS
    widths = tuple(p.shape[1] for p in parts)

    def tiled(a):
        w = a.shape[1]
        main = pl.BlockSpec((TM, w), lambda i: (i, 0))
        if not halo:
            return [main], [a]
        prev = pl.BlockSpec((BF16_ROWS, w), lambda i: (jnp.maximum(i * hb - 1, 0), 0))
        nxt = pl.BlockSpec((BF16_ROWS, w), lambda i: (jnp.minimum((i + 1) * hb, n_hblk - 1), 0))
        return [prev, main, nxt], [a, a, a]

    in_specs, args = [], []
    for a in (x,) + tuple(parts):
        sp, ar = tiled(a)
        in_specs += sp
        args += ar
    in_specs += [pl.BlockSpec((None, 6, D_MODEL), lambda i: (mod_row(i), 0, 0)),
                 _layer_spec(l, 1, D_MODEL),
                 _layer_spec(l, D_MODEL, D_MODEL, single=True),
                 _layer_spec(l, D_MODEL, 2 * D_FF, single=True),
                 _layer_spec(l, 3, 2 * D_FF, single=True),
                 _layer_spec(l, 1, 2 * D_FF, single=True),
                 _layer_spec(l, D_FF, D_MODEL, single=True)]
    args += [mods, pw["g_ffn"], pw["w_out"], pw["w_ffn_in"], pw["conv_w"], pw["conv_b"], pw["w_ffn_out"]]
    if final:
        in_specs.append(pl.BlockSpec((1, D_MODEL), lambda i: (0, 0)))
        args.append(g_final)
    return pl.pallas_call(
        functools.partial(_ffn_kernel, seq_len, final, widths),
        grid=(nt,),
        in_specs=in_specs,
        out_specs=pl.BlockSpec((TM, D_MODEL), lambda i: (i, 0)),
        out_shape=jax.ShapeDtypeStruct((t, D_MODEL), F32),
        scratch_shapes=[pltpu.VMEM((ext, D_MODEL), BF16), pltpu.VMEM((ext, D_MODEL), BF16),
                        pltpu.VMEM((2, FF_LB, ext, LANES), F32), pltpu.VMEM((TM, D_FF), BF16)],
        compiler_params=_cparams(1),
        name=name,
    )(*args)


def _pack_mla_kernel(wuq_ref, wukv_ref, ouq_ref, oukv_ref):
    dq = MLA_NOPE + MLA_ROPE
    zq = jnp.zeros((MLA_Q_RANK, LANES - dq), F32)
    zk = jnp.zeros((MLA_KV_RANK, LANES - MLA_NOPE), F32)
    dkv = MLA_NOPE + MLA_V
    for hd in range(MLA_HEADS):
        ouq_ref[:, hd * LANES:(hd + 1) * LANES] = jnp.concatenate(
            [wuq_ref[:, hd * dq:(hd + 1) * dq], zq], axis=-1).astype(BF16)
        oukv_ref[:, hd * LANES:(hd + 1) * LANES] = jnp.concatenate(
            [wukv_ref[:, hd * dkv:hd * dkv + MLA_NOPE], zk], axis=-1).astype(BF16)
    voff = MLA_HEADS * LANES
    for p in range(MLA_HEADS // 2):
        oukv_ref[:, voff + p * LANES:voff + (p + 1) * LANES] = jnp.concatenate(
            [wukv_ref[:, (2 * p + j) * dkv + MLA_NOPE:(2 * p + j + 1) * dkv] for j in range(2)], axis=-1).astype(BF16)


def _pack_weights(w_in, w_uq, w_ukv, w_sgu, b_sgu, w_out, w_ffn_in, ffn_conv_w, ffn_conv_b, w_ffn_out,
                  g_mix, g_cq, g_ckv, g_sgu, g_ffn):
    nl = w_in.shape[0]
    wt = jnp.swapaxes(w_in, 1, 2).astype(BF16)
    zr = lambda n: jnp.zeros((nl, n, D_MODEL), BF16)
    w_in_p = jnp.concatenate([wt[:, :OFF_KR], zr(HALF), wt[:, OFF_KR:OFF_KR + MLA_ROPE],
                              zr(LANES - HALF - MLA_ROPE), wt[:, OFF_KR + MLA_ROPE:]], axis=1)
    w_uq_p, w_ukv_p = pl.pallas_call(
        _pack_mla_kernel,
        grid=(nl,),
        in_specs=[pl.BlockSpec((None,) + w_uq.shape[1:], lambda l: (l, 0, 0)),
                  pl.BlockSpec((None,) + w_ukv.shape[1:], lambda l: (l, 0, 0))],
        out_specs=[pl.BlockSpec((None, MLA_Q_RANK, MLA_HEADS * LANES), lambda l: (l, 0, 0)),
                   pl.BlockSpec((None, MLA_KV_RANK, KV_COLS), lambda l: (l, 0, 0))],
        out_shape=[jax.ShapeDtypeStruct((nl, MLA_Q_RANK, MLA_HEADS * LANES), BF16),
                   jax.ShapeDtypeStruct((nl, MLA_KV_RANK, KV_COLS), BF16)],
        compiler_params=_cparams(1),
        name="pack_w_mla",
    )(w_uq, w_ukv)
    b_sgu_p = jnp.repeat(jnp.swapaxes(b_sgu, 1, 2), SGU_WIDTH // SGU_GROUPS, axis=-1)
    return dict(
        w_in=w_in_p, w_uq=w_uq_p, w_ukv=w_ukv_p, w_sgu=w_sgu.astype(BF16), b_sgu=b_sgu_p,
        w_out=w_out.astype(BF16), w_ffn_in=w_ffn_in.astype(BF16), conv_w=ffn_conv_w,
        conv_b=ffn_conv_b[:, None, :], w_ffn_out=w_ffn_out.astype(BF16),
        g_mix=g_mix[:, None, :], g_cq=g_cq[:, None, :], g_ckv=g_ckv[:, None, :], g_sgu=g_sgu[:, None, :],
        g_ffn=g_ffn[:, None, :])


def _rope_tables(n_tokens):
    t = jnp.arange(n_tokens)[:, None]
    n_freq = MLA_ROPE // 4
    inv_freq = ROPE_THETA ** (-jnp.arange(n_freq, dtype=F32) / n_freq)
    j = jnp.arange(LANES)[None, :] - HALF
    rotary = (j >= 0) & (j < MLA_ROPE)
    pos = jnp.where(j < MLA_ROPE // 2, t // GRID_W, t % GRID_W).astype(F32)
    ang = pos * inv_freq[j % n_freq]
    sign = jnp.where(j % (2 * n_freq) < n_freq, -1.0, 1.0)
    cos = jnp.where(rotary, jnp.cos(ang), 1.0)
    sin = jnp.where(rotary, sign * jnp.sin(ang), 0.0)
    return cos, sin


def kernel(x_prompt, x_sample, cache_na_k, cache_na_v, cache_mla_ckv, cache_mla_krope, c, c_ctx, w_mod, b_mod,
           g_mix, w_in, na_rpb, g_cq, w_uq, g_ckv, w_ukv, g_sgu, w_sgu, b_sgu, w_out, g_ffn, w_ffn_in,
           ffn_conv_w, ffn_conv_b, w_ffn_out, g_final):
    n_ctx, n_lat = x_prompt.shape[0], x_sample.shape[0]
    t_ctx, t_lat = n_ctx * SEQ, n_lat * DEC_SEQ
    pw = _pack_weights(w_in, w_uq, w_ukv, w_sgu, b_sgu, w_out, w_ffn_in, ffn_conv_w, ffn_conv_b,
                       w_ffn_out, g_mix, g_cq, g_ckv, g_sgu, g_ffn)
    g_fin = g_final[None, :]

    cond = jnp.concatenate([c_ctx[None, :], c, jnp.zeros((8 - 1 - n_lat, D_MODEL), F32)], axis=0)
    mods = _modulation(cond, w_mod, b_mod)
    rope_tabs = _rope_tables(DEC_SEQ)
    na_bias = _na_bias(na_rpb)
    kr_pad = jnp.pad(cache_mla_krope, ((0, 0), (0, 0), (0, 0), (HALF, LANES - HALF - MLA_ROPE)))
    cache_kp, cache_vp = _cache_kv(cache_mla_ckv, kr_pad, pw["w_ukv"])
    channel_major = lambda a: jnp.transpose(a, (0, 1, 3, 4, 2)).reshape(n_lat, DEPTH, NA_WIDTH, PAST_LEN)
    cache_k = channel_major(cache_na_k)
    cache_v = channel_major(cache_na_v)

    lat_tiles = DEC_SEQ // TM
    proj_tiles = DEC_SEQ // PROJ_TM
    ctx_row = lambda i: 0
    lat_row = lambda i: 1 + i // lat_tiles
    lat_proj_row = lambda i: 1 + i // proj_tiles

    xp = x_prompt.reshape(t_ctx, D_MODEL)
    xs = x_sample.reshape(t_lat, D_MODEL)
    per_head = (n_ctx, DEPTH, NA_HEADS, HEAD_DIM, SEQ)
    ctx_out = (jnp.zeros(per_head, F32), jnp.zeros(per_head, F32), jnp.zeros((n_ctx, DEPTH, SEQ, MLA_KV_RANK), F32))
    new_kr = []
    for l in range(DEPTH):
        last = l == DEPTH - 1
        m = mods[l]
        o_ab, oc, *ctx_out, kr_f = _project(xp, m, ctx_row, pw, l, None, ctx_out, 1)
        new_kr.append(kr_f[:, HALF:HALF + MLA_ROPE])
        xp = _ffn(xp, (o_ab, oc), m, ctx_row, pw, l, SEQ, g_fin if last else None, "ctx_ffn")

        qa, ka, va, qm, kp, vp, oc = _project(xs, m, lat_proj_row, pw, l, rope_tabs, None, proj_tiles)
        o_a = _na_attention(qa, ka, va, cache_k, cache_v, na_bias, l, n_lat)
        o_b = _lat_mla(qm, cache_kp, cache_vp, kp, vp, l, n_lat)
        xs = _ffn(xs, (o_a, o_b, oc), m, lat_row, pw, l, DEC_SEQ, g_fin if last else None, "lat_ffn")

    token_major = lambda a: jnp.transpose(a, (0, 1, 4, 2, 3))
    new_k, new_v, new_ckv = ctx_out
    return (xp.reshape(n_ctx, SEQ, D_MODEL), xs.reshape(n_lat, DEC_SEQ, D_MODEL),
            token_major(new_k), token_major(new_v), new_ckv,
            jnp.stack([a.reshape(n_ctx, SEQ, MLA_ROPE) for a in new_kr], axis=1))
```
